```python
import jax, jax.numpy as jnp
from jax import lax
import numpy as np

D_MODEL = 1024
BATCH = 8
SEQ = 2048
DEPTH = 4

N_MIXERS = 2
N_A_LAYERS = (DEPTH + 1) // 2
N_B_LAYERS = DEPTH // 2

DN_HEADS = 8
DN_HEAD_DIM = 128
DN_KEY = DN_HEADS * DN_HEAD_DIM
DN_VAL = DN_HEADS * DN_HEAD_DIM
DN_QKV = 2 * DN_KEY + DN_VAL
DN_IN = DN_QKV + DN_VAL + 2 * DN_HEADS
DN_CONV = 4
DN_CHUNK = 64

CF_CH = D_MODEL
CF_KERNEL = 31

FF_DIM = 4 * D_MODEL

ALPHA = (2.0 * DEPTH) ** 0.25
BETA_INIT = (8.0 * DEPTH) ** -0.25
N_MOD = 6
LN_EPS = 1e-5
RMS_EPS = 1e-6
L2_EPS = 1e-6

kernel_name = "hybrid_gdn_conformer_deepnorm_adaln"


def layer_norm(x, g, b):
    xf = x.astype(jnp.float32)
    mu = jnp.mean(xf, axis=-1, keepdims=True)
    var = jnp.mean(jnp.square(xf - mu), axis=-1, keepdims=True)
    return ((xf - mu) * lax.rsqrt(var + LN_EPS) * g + b).astype(x.dtype)


def l2_normalize(x):
    xf = x.astype(jnp.float32)
    return xf * lax.rsqrt(jnp.sum(xf * xf, axis=-1, keepdims=True) + L2_EPS)


def causal_depthwise_conv(x, w):
    k, ch = w.shape
    return lax.conv_general_dilated(
        x, w.astype(x.dtype)[:, None, :], window_strides=(1,), padding=((k - 1, 0),),
        dimension_numbers=('NWC', 'WIO', 'NWC'), feature_group_count=ch)


def gated_delta_rule(q, k, v, g, beta):
    b, s, h, dk = q.shape
    dv = v.shape[-1]
    c = DN_CHUNK
    n = s // c
    f32 = jnp.float32

    def chunks(t):
        t = t.astype(f32).reshape((b, n, c, h) + t.shape[3:])
        return jnp.moveaxis(t, 3, 1)

    q, k, v, g, beta = chunks(q), chunks(k), chunks(v), chunks(g), chunks(beta)
    q = q * (dk ** -0.5)
    gam = jnp.cumsum(g, axis=-1)
    causal = jnp.tril(jnp.ones((c, c), dtype=bool))
    strict = jnp.tril(jnp.ones((c, c), dtype=bool), -1)
    diff = gam[..., :, None] - gam[..., None, :]
    decay = jnp.exp(jnp.where(causal, diff, -jnp.inf))

    kb = k * beta[..., None]
    a_kk = jnp.where(strict, jnp.einsum('bhncd,bhnsd->bhncs', kb, k) * decay, 0.0)
    eye = jnp.eye(c, dtype=f32)
    t_inv = lax.linalg.triangular_solve(eye + a_kk, jnp.broadcast_to(eye, a_kk.shape),
                                        left_side=True, lower=True)
    u = jnp.matmul(t_inv, v * beta[..., None])
    w = jnp.matmul(t_inv, kb * jnp.exp(gam)[..., None])
    a_qk = jnp.einsum('bhncd,bhnsd->bhncs', q, k) * decay
    q_dec = q * jnp.exp(gam)[..., None]
    k_dec = k * jnp.exp(gam[..., -1:] - gam)[..., None]
    g_last = jnp.exp(gam[..., -1])

    def step(state, xs):
        u_i, w_i, a_i, qd_i, kd_i, gl_i = xs
        v_new = u_i - jnp.einsum('bhck,bhkv->bhcv', w_i, state)
        o_i = (jnp.einsum('bhck,bhkv->bhcv', qd_i, state)
               + jnp.einsum('bhcs,bhsv->bhcv', a_i, v_new))
        state = state * gl_i[..., None, None] + jnp.einsum('bhck,bhcv->bhkv', kd_i, v_new)
        return state, o_i

    xs = tuple(jnp.moveaxis(t, 2, 0) for t in (u, w, a_qk, q_dec, k_dec, g_last))
    s0 = jnp.zeros((b, h, dk, dv), f32)
    _, o = lax.scan(step, s0, xs)
    return jnp.transpose(o, (1, 0, 3, 2, 4)).reshape(b, s, h, dv)


def deltanet_mixer(h, w_in, conv_w, a_log, dt_bias, norm_w, w_out):
    b, s, _ = h.shape
    proj = h @ w_in
    qkv, z, bt, at = jnp.split(proj, [DN_QKV, DN_QKV + DN_VAL, DN_QKV + DN_VAL + DN_HEADS], axis=-1)
    qkv = jax.nn.silu(causal_depthwise_conv(qkv, conv_w))
    q, k, v = jnp.split(qkv, [DN_KEY, 2 * DN_KEY], axis=-1)
    q = l2_normalize(q.reshape(b, s, DN_HEADS, DN_HEAD_DIM))
    k = l2_normalize(k.reshape(b, s, DN_HEADS, DN_HEAD_DIM))
    v = v.reshape(b, s, DN_HEADS, DN_HEAD_DIM)
    beta = jax.nn.sigmoid(bt.astype(jnp.float32))
    g = -jnp.exp(a_log.astype(jnp.float32)) * jax.nn.softplus(at.astype(jnp.float32) + dt_bias)
    o = gated_delta_rule(q, k, v, g, beta)
    o = o * lax.rsqrt(jnp.mean(o * o, axis=-1, keepdims=True) + RMS_EPS) * norm_w
    o = o * jax.nn.silu(z.reshape(b, s, DN_HEADS, DN_HEAD_DIM).astype(jnp.float32))
    return o.reshape(b, s, DN_VAL).astype(h.dtype) @ w_out


def conformer_conv_mixer(h, w_in, dw_w, dw_b, ln_g, ln_b, w_out):
    val, gate = jnp.split(h @ w_in, 2, axis=-1)
    u = val * jax.nn.sigmoid(gate)
    u = causal_depthwise_conv(u, dw_w) + dw_b
    u = jax.nn.silu(layer_norm(u, ln_g, ln_b))
    return u @ w_out


def sq_relu_mlp(h, w1, w2):
    return jnp.square(jax.nn.relu(h @ w1)) @ w2


def setup_inputs(seed: int = 0) -> dict:
    key = jax.random.key(seed)
    ks = jax.random.split(key, 24)
    nrm = jax.random.normal
    f32 = jnp.float32
    x = nrm(ks[0], (BATCH, SEQ, D_MODEL), f32)
    c = nrm(ks[1], (BATCH, D_MODEL), f32)
    ada_w = nrm(ks[2], (DEPTH, D_MODEL, N_MOD * D_MODEL), f32) * (0.1 * D_MODEL ** -0.5)
    ada_b = nrm(ks[3], (DEPTH, N_MOD * D_MODEL), f32) * 0.01
    ln_g = 1.0 + 0.01 * nrm(ks[4], (DEPTH, 2, D_MODEL), f32)
    ln_b = 0.01 * nrm(ks[5], (DEPTH, 2, D_MODEL), f32)
    dn_w_in = nrm(ks[6], (N_A_LAYERS, D_MODEL, DN_IN), f32) * D_MODEL ** -0.5
    dn_conv_w = nrm(ks[7], (N_A_LAYERS, DN_CONV, DN_QKV), f32) * DN_CONV ** -0.5
    dn_a_log = jnp.log(jax.random.uniform(ks[8], (N_A_LAYERS, DN_HEADS), f32, 1.0, 16.0))
    dt = jnp.exp(jax.random.uniform(ks[9], (N_A_LAYERS, DN_HEADS), f32, float(np.log(1e-3)), float(np.log(1e-1))))
    dn_dt_bias = dt + jnp.log(-jnp.expm1(-dt))
    dn_norm_w = 1.0 + 0.01 * nrm(ks[10], (N_A_LAYERS, DN_HEAD_DIM), f32)
    dn_w_out = nrm(ks[11], (N_A_LAYERS, DN_VAL, D_MODEL), f32) * (BETA_INIT * DN_VAL ** -0.5)
    cf_w_in = nrm(ks[12], (N_B_LAYERS, D_MODEL, 2 * CF_CH), f32) * D_MODEL ** -0.5
    cf_dw_w = nrm(ks[13], (N_B_LAYERS, CF_KERNEL, CF_CH), f32) * CF_KERNEL ** -0.5
    cf_dw_b = 0.01 * nrm(ks[14], (N_B_LAYERS, CF_CH), f32)
    cf_ln_g = 1.0 + 0.01 * nrm(ks[15], (N_B_LAYERS, CF_CH), f32)
    cf_ln_b = 0.01 * nrm(ks[16], (N_B_LAYERS, CF_CH), f32)
    cf_w_out = nrm(ks[17], (N_B_LAYERS, CF_CH, D_MODEL), f32) * (BETA_INIT * CF_CH ** -0.5)
    ff_w1 = nrm(ks[18], (DEPTH, D_MODEL, FF_DIM), f32) * D_MODEL ** -0.5
    ff_w2 = nrm(ks[19], (DEPTH, FF_DIM, D_MODEL), f32) * (BETA_INIT * FF_DIM ** -0.5)
    return {"x": x, "c": c, "ada_w": ada_w, "ada_b": ada_b, "ln_g": ln_g, "ln_b": ln_b,
            "dn_w_in": dn_w_in, "dn_conv_w": dn_conv_w, "dn_a_log": dn_a_log,
            "dn_dt_bias": dn_dt_bias, "dn_norm_w": dn_norm_w, "dn_w_out": dn_w_out,
            "cf_w_in": cf_w_in, "cf_dw_w": cf_dw_w, "cf_dw_b": cf_dw_b, "cf_ln_g": cf_ln_g,
            "cf_ln_b": cf_ln_b, "cf_w_out": cf_w_out, "ff_w1": ff_w1, "ff_w2": ff_w2}


def reference(x, c, ada_w, ada_b, ln_g, ln_b, dn_w_in, dn_conv_w, dn_a_log, dn_dt_bias,
              dn_norm_w, dn_w_out, cf_w_in, cf_dw_w, cf_dw_b, cf_ln_g, cf_ln_b, cf_w_out,
              ff_w1, ff_w2):
    cond = jax.nn.silu(c)
    for i in range(DEPTH):
        mod = cond @ ada_w[i] + ada_b[i]
        sh1, sc1, gt1, sh2, sc2, gt2 = [m[:, None, :] for m in jnp.split(mod, N_MOD, axis=-1)]
        h = x * (1.0 + sc1) + sh1
        j = i // N_MIXERS
        if i % N_MIXERS == 0:
            y = deltanet_mixer(h, dn_w_in[j], dn_conv_w[j], dn_a_log[j], dn_dt_bias[j],
                               dn_norm_w[j], dn_w_out[j])
        else:
            y = conformer_conv_mixer(h, cf_w_in[j], cf_dw_w[j], cf_dw_b[j], cf_ln_g[j],
                                     cf_ln_b[j], cf_w_out[j])
        x = layer_norm(ALPHA * x + (1.0 + gt1) * y, ln_g[i, 0], ln_b[i, 0])
        h = x * (1.0 + sc2) + sh2
        x = layer_norm(ALPHA * x + (1.0 + gt2) * sq_relu_mlp(h, ff_w1[i], ff_w2[i]), ln_g[i, 1], ln_b[i, 1])
    return x
```

```python
import functools

import jax
import jax.numpy as jnp
from jax import lax
from jax.experimental import pallas as pl
from jax.experimental.pallas import tpu as pltpu

F32 = jnp.float32
BF16 = jnp.bfloat16

D_MODEL = 1024
DEPTH = 4
N_MOD = 6
HEADS = 8
HEAD_DIM = 128
DN_CONV = 4
CHUNK = 64
BLOCK16 = 16
CF_KERNEL = 31
FF_DIM = 4 * D_MODEL
ALPHA = (2.0 * DEPTH) ** 0.25
LN_EPS = 1e-5
RMS_EPS = 1e-6
L2_EPS = 1e-6

LANES = 128
SUBLANES = 8
TM = 512
FF_TILE = 1024
CF_HALO = 32
DN_HALO = 8
VMEM_LIMIT = 56 * 1024 * 1024


def _dot(a, b):
    return jnp.dot(a, b, preferred_element_type=F32)


def _dot_nt(a, b):
    return lax.dot_general(a, b, (((1,), (1,)), ((), ())), preferred_element_type=F32)


def _dot_tn(a, b):
    return lax.dot_general(a, b, (((0,), (0,)), ((), ())), preferred_element_type=F32)


def _bdot(a, b):
    return _dot(a.astype(BF16), b.astype(BF16))


def _split3(x):
    x1 = x.astype(BF16)
    r1 = x - x1.astype(F32)
    x2 = r1.astype(BF16)
    x3 = (r1 - x2.astype(F32)).astype(BF16)
    return x1, x2, x3


def _dot_exact_lhs(a_bf16, x):
    x1, x2, x3 = _split3(x)
    return _dot(a_bf16, x3) + _dot(a_bf16, x2) + _dot(a_bf16, x1)


def _layer_norm(r, g, b):
    mu = jnp.mean(r, axis=-1, keepdims=True)
    d = r - mu
    var = jnp.mean(d * d, axis=-1, keepdims=True)
    return d * lax.rsqrt(var + LN_EPS) * g + b


def _silu(x):
    return x * jax.nn.sigmoid(x)


def _softplus(x):
    return jnp.maximum(x, 0.0) + jnp.log1p(jnp.exp(-jnp.abs(x)))


def _const_spec(shape):
    nd = len(shape)
    return pl.BlockSpec(shape, lambda *_: (0,) * nd, pipeline_mode=pl.Buffered(1))


def _mod_spec(row, tiles_per_batch):
    return pl.BlockSpec((1, 1, 1, D_MODEL), lambda t: (row, t // tiles_per_batch, 0, 0))


def _ada_kernel(c_ref, w_ref, b_ref, o_ref):
    cond = _silu(c_ref[...])
    o_ref[0] = _dot(cond, w_ref[0]) + b_ref[0]


def _ada_mods(c, ada_w, ada_b):
    batch = c.shape[0]
    out = pl.pallas_call(
        _ada_kernel,
        grid=(DEPTH, N_MOD),
        in_specs=[
            pl.BlockSpec((batch, D_MODEL), lambda i, j: (0, 0)),
            pl.BlockSpec((1, D_MODEL, D_MODEL), lambda i, j: (i, 0, j)),
            pl.BlockSpec((1, 1, D_MODEL), lambda i, j: (i, 0, j)),
        ],
        out_specs=pl.BlockSpec((1, batch, D_MODEL), lambda i, j: (i * N_MOD + j, 0, 0)),
        out_shape=jax.ShapeDtypeStruct((DEPTH * N_MOD, batch, D_MODEL), F32),
        compiler_params=pltpu.CompilerParams(dimension_semantics=("arbitrary", "arbitrary")),
        name="ada_mods",
    )(c, ada_w, ada_b.reshape(DEPTH, 1, N_MOD * D_MODEL))
    return out.reshape(DEPTH * N_MOD, batch, 1, D_MODEL)


def _post_kernel(x_ref, u_ref, gt1_ref, sh2_ref, sc2_ref, gt2_ref, wo_ref, lng_ref, lnb_ref,
                 w1_ref, w2_ref, o_ref):
    x = x_ref[...]
    y = _dot(u_ref[...], wo_ref[...])
    x1 = _layer_norm(ALPHA * x + (1.0 + gt1_ref[0, 0]) * y, lng_ref[0:1, :], lnb_ref[0:1, :])
    h = (x1 * (1.0 + sc2_ref[0, 0]) + sh2_ref[0, 0]).astype(BF16)
    acc = None
    for c in range(FF_DIM // FF_TILE):
        cols = slice(c * FF_TILE, (c + 1) * FF_TILE)
        a = jnp.maximum(_dot(h, w1_ref[:, cols]), 0.0)
        p = _dot((a * a).astype(BF16), w2_ref[cols, :])
        acc = p if acc is None else acc + p
    o_ref[...] = _layer_norm(ALPHA * x1 + (1.0 + gt2_ref[0, 0]) * acc, lng_ref[1:2, :], lnb_ref[1:2, :])


def _post(x2d, u2d, mods, layer, tiles_per_batch, w_out, ln_g, ln_b, w1, w2):
    rows = x2d.shape[0]
    row_spec = pl.BlockSpec((TM, D_MODEL), lambda t: (t, 0))
    base = layer * N_MOD
    return pl.pallas_call(
        _post_kernel,
        grid=(rows // TM,),
        in_specs=[
            row_spec, row_spec,
            _mod_spec(base + 2, tiles_per_batch), _mod_spec(base + 3, tiles_per_batch),
            _mod_spec(base + 4, tiles_per_batch), _mod_spec(base + 5, tiles_per_batch),
            _const_spec((D_MODEL, D_MODEL)), _const_spec((2, D_MODEL)), _const_spec((2, D_MODEL)),
            _const_spec((D_MODEL, FF_DIM)), _const_spec((FF_DIM, D_MODEL)),
        ],
        out_specs=row_spec,
        out_shape=jax.ShapeDtypeStruct((rows, D_MODEL), F32),
        compiler_params=pltpu.CompilerParams(dimension_semantics=("arbitrary",),
                                             vmem_limit_bytes=VMEM_LIMIT),
        name="post_mlp",
    )(x2d, u2d, mods, mods, mods, mods, w_out, ln_g, ln_b, w1, w2)


def _dn_in_kernel(x_ref, sh_ref, sc_ref, w_ref, wba_ref, cw_ref, alog_ref, dtb_ref,
                  q_ref, k_ref, v_ref, z_ref, bg_ref, ext_ref, *, tiles_per_batch):
    t = pl.program_id(0)

    @pl.when(t % tiles_per_batch == 0)
    def _():
        ext_ref[0:DN_HALO, :] = jnp.zeros((DN_HALO, 3 * D_MODEL), F32)

    h = (x_ref[...] * (1.0 + sc_ref[0, 0]) + sh_ref[0, 0]).astype(BF16)

    for grp, out_ref in enumerate((q_ref, k_ref, v_ref)):
        cols = slice(grp * D_MODEL, (grp + 1) * D_MODEL)
        ext_ref[DN_HALO:DN_HALO + TM, cols] = _dot(h, w_ref[:, cols])
        acc = None
        for j in range(DN_CONV):
            off = DN_HALO - (DN_CONV - 1) + j
            term = cw_ref[j:j + 1, cols] * ext_ref[off:off + TM, cols]
            acc = term if acc is None else acc + term
        ext_ref[0:DN_HALO, cols] = ext_ref[TM:TM + DN_HALO, cols]
        y = _silu(acc)
        if out_ref is v_ref:
            out_ref[...] = y
        else:
            for hd in range(HEADS):
                lanes = slice(hd * HEAD_DIM, (hd + 1) * HEAD_DIM)
                ys = y[:, lanes]
                ss = jnp.sum(ys * ys, axis=-1, keepdims=True)
                out_ref[:, lanes] = ys * lax.rsqrt(ss + L2_EPS)

    z_ref[...] = _dot(h, w_ref[:, 3 * D_MODEL:4 * D_MODEL])

    ba = _dot(h, wba_ref[...])
    beta = jax.nn.sigmoid(ba)
    g = -jnp.exp(alog_ref[...]) * _softplus(ba + dtb_ref[...])
    lane = lax.broadcasted_iota(jnp.int32, ba.shape, 1)
    bg_ref[...] = jnp.where(lane < HEADS, beta, g)


def _dn_in(x2d, mods, layer, tiles_per_batch, w_main, w_ba, conv_w, alog_vec, dtb_vec):
    rows = x2d.shape[0]
    row_spec = pl.BlockSpec((TM, D_MODEL), lambda t: (t, 0))
    base = layer * N_MOD
    out_sds = jax.ShapeDtypeStruct((rows, D_MODEL), F32)
    return pl.pallas_call(
        functools.partial(_dn_in_kernel, tiles_per_batch=tiles_per_batch),
        grid=(rows // TM,),
        in_specs=[
            row_spec, _mod_spec(base + 0, tiles_per_batch), _mod_spec(base + 1, tiles_per_batch),
            _const_spec((D_MODEL, 4 * D_MODEL)), _const_spec((D_MODEL, LANES)),
            _const_spec((DN_CONV, 3 * D_MODEL)), _const_spec((1, LANES)), _const_spec((1, LANES)),
        ],
        out_specs=[row_spec, row_spec, row_spec, row_spec, pl.BlockSpec((TM, LANES), lambda t: (t, 0))],
        out_shape=[out_sds, out_sds, out_sds, out_sds, jax.ShapeDtypeStruct((rows, LANES), F32)],
        scratch_shapes=[pltpu.VMEM((DN_HALO + TM, 3 * D_MODEL), F32)],
        compiler_params=pltpu.CompilerParams(dimension_semantics=("arbitrary",),
                                             vmem_limit_bytes=VMEM_LIMIT),
        name="dn_in",
    )(x2d, mods, mods, w_main, w_ba, conv_w, alog_vec, dtb_vec)


def _unit_lower_inverse_minus_eye(a):
    n = a.shape[0]
    ri = lax.broadcasted_iota(jnp.int32, (n, n), 0) // BLOCK16
    ci = lax.broadcasted_iota(jnp.int32, (n, n), 1) // BLOCK16
    d = jnp.where(ri == ci, a, 0.0)
    low = a - d
    d2 = _bdot(d, d)
    d4 = _bdot(d2, d2)
    d8 = _bdot(d4, d4)
    p = d2 - d - _bdot(d, d2)
    p = p + d4 + _bdot(p, d4)
    p = p + d8 + _bdot(p, d8)
    m = low + _bdot(p, low)
    m2 = _bdot(m, m)
    q = m2 - m - _bdot(m, m2)
    return q + p + _bdot(q, p)


def _dn_core_kernel(q_ref, k_ref, v_ref, z_ref, bg_ref, nw_ref, o_ref,
                    state_ref, gam_ref, egam_ref, edec_ref, egl_ref):
    @pl.when(pl.program_id(1) == 0)
    def _():
        state_ref[...] = jnp.zeros(state_ref.shape, F32)

    ri = lax.broadcasted_iota(jnp.int32, (TM, TM), 0)
    ci = lax.broadcasted_iota(jnp.int32, (TM, TM), 1)
    same_chunk = (ri // CHUNK) == (ci // CHUNK)
    cum_mat = jnp.where(same_chunk & (ci <= ri), 1.0, 0.0).astype(BF16)
    tot_mat = jnp.where(same_chunk, 1.0, 0.0).astype(BF16)
    bg = bg_ref[...]
    gam = _dot_exact_lhs(cum_mat, bg)
    gtot = _dot_exact_lhs(tot_mat, bg)
    gam_ref[...] = gam
    egam_ref[...] = jnp.exp(gam)
    edec_ref[...] = jnp.exp(gtot - gam)
    egl_ref[...] = jnp.exp(gtot)

    r_idx = lax.broadcasted_iota(jnp.int32, (CHUNK, CHUNK), 0)
    c_idx = lax.broadcasted_iota(jnp.int32, (CHUNK, CHUNK), 1)
    causal = c_idx <= r_idx
    strict = c_idx < r_idx
    scale = HEAD_DIM ** -0.5
    nw = nw_ref[...]

    def chunk_body(c, carry):
        r0 = pl.multiple_of(c * CHUNK, CHUNK)
        rows = pl.ds(r0, CHUNK)
        gam_c = gam_ref[rows, :]
        gam_t = gam_c.T
        bg_c = bg_ref[rows, :]
        egam_c = egam_ref[rows, :]
        edec_c = edec_ref[rows, :]
        egl_c = egl_ref[rows, :]
        for hd in range(HEADS):
            lanes = slice(hd * HEAD_DIM, (hd + 1) * HEAD_DIM)
            gl = HEADS + hd
            q = q_ref[rows, lanes]
            k = k_ref[rows, lanes]
            v = v_ref[rows, lanes]
            beta = bg_c[:, hd:hd + 1]
            eg = egam_c[:, gl:gl + 1]
            diff = gam_c[:, gl:gl + 1] - gam_t[gl:gl + 1, :]
            decay = jnp.exp(jnp.where(causal, diff, -jnp.inf))
            kb = k * beta
            k16 = k.astype(BF16)
            a_kk = jnp.where(strict, _dot_nt(kb.astype(BF16), k16) * decay, 0.0)
            a_qk = _dot_nt((q * scale).astype(BF16), k16) * decay
            tinv = _unit_lower_inverse_minus_eye(a_kk)
            vb = v * beta
            kbg = kb * eg
            u = vb + _bdot(tinv, vb)
            w = kbg + _bdot(tinv, kbg)
            state = state_ref[hd]
            s16 = state.astype(BF16)
            v_new = u - _dot(w.astype(BF16), s16)
            o = _dot((q * (eg * scale)).astype(BF16), s16) + _bdot(a_qk, v_new)
            k_dec = k * edec_c[:, gl:gl + 1]
            state_ref[hd] = state * egl_c[0:1, gl:gl + 1] + _dot_tn(k_dec.astype(BF16), v_new.astype(BF16))
            on = o * lax.rsqrt(jnp.mean(o * o, axis=-1, keepdims=True) + RMS_EPS) * nw
            o_ref[rows, lanes] = (on * _silu(z_ref[rows, lanes])).astype(BF16)
        return carry

    lax.fori_loop(0, TM // CHUNK, chunk_body, 0)


def _dn_core(q, k, v, z, bg, norm_w, batch, tiles_per_batch):
    rows = q.shape[0]
    row_spec = pl.BlockSpec((TM, D_MODEL), lambda b, s: (b * tiles_per_batch + s, 0))
    small = pltpu.VMEM((TM, LANES), F32)
    return pl.pallas_call(
        _dn_core_kernel,
        grid=(batch, tiles_per_batch),
        in_specs=[row_spec, row_spec, row_spec, row_spec,
                  pl.BlockSpec((TM, LANES), lambda b, s: (b * tiles_per_batch + s, 0)),
                  pl.BlockSpec((1, HEAD_DIM), lambda b, s: (0, 0))],
        out_specs=row_spec,
        out_shape=jax.ShapeDtypeStruct((rows, D_MODEL), BF16),
        scratch_shapes=[pltpu.VMEM((HEADS, HEAD_DIM, HEAD_DIM), F32), small, small, small, small],
        compiler_params=pltpu.CompilerParams(dimension_semantics=("arbitrary", "arbitrary"),
                                             vmem_limit_bytes=VMEM_LIMIT),
        name="dn_core",
    )(q, k, v, z, bg, norm_w)


def _cf_kernel(x_ref, sh_ref, sc_ref, w_ref, dw_ref, dwb_ref, g_ref, b_ref, o_ref, ext_ref,
               *, tiles_per_batch):
    t = pl.program_id(0)

    @pl.when(t % tiles_per_batch == 0)
    def _():
        ext_ref[0:CF_HALO, :] = jnp.zeros((CF_HALO, D_MODEL), F32)

    h = (x_ref[...] * (1.0 + sc_ref[0, 0]) + sh_ref[0, 0]).astype(BF16)
    val = _dot(h, w_ref[:, 0:D_MODEL])
    gate = _dot(h, w_ref[:, D_MODEL:2 * D_MODEL])
    ext_ref[CF_HALO:CF_HALO + TM, :] = val * jax.nn.sigmoid(gate)

    acc = None
    for j in range(CF_KERNEL):
        off = CF_HALO - (CF_KERNEL - 1) + j
        term = dw_ref[j:j + 1, :] * ext_ref[off:off + TM, :]
        acc = term if acc is None else acc + term
    ext_ref[0:CF_HALO, :] = ext_ref[TM:TM + CF_HALO, :]
    u = _layer_norm(acc + dwb_ref[...], g_ref[...], b_ref[...])
    o_ref[...] = _silu(u).astype(BF16)


def _cf_mix(x2d, mods, layer, tiles_per_batch, w_in, dw_w, dw_b, ln_g, ln_b):
    rows = x2d.shape[0]
    row_spec = pl.BlockSpec((TM, D_MODEL), lambda t: (t, 0))
    base = layer * N_MOD
    return pl.pallas_call(
        functools.partial(_cf_kernel, tiles_per_batch=tiles_per_batch),
        grid=(rows // TM,),
        in_specs=[
            row_spec, _mod_spec(base + 0, tiles_per_batch), _mod_spec(base + 1, tiles_per_batch),
            _const_spec((D_MODEL, 2 * D_MODEL)), _const_spec((CF_KERNEL, D_MODEL)),
            _const_spec((1, D_MODEL)), _const_spec((1, D_MODEL)), _const_spec((1, D_MODEL)),
        ],
        out_specs=row_spec,
        out_shape=jax.ShapeDtypeStruct((rows, D_MODEL), BF16),
        scratch_shapes=[pltpu.VMEM((CF_HALO + TM, D_MODEL), F32)],
        compiler_params=pltpu.CompilerParams(dimension_semantics=("arbitrary",),
                                             vmem_limit_bytes=VMEM_LIMIT),
        name="cf_mix",
    )(x2d, mods, mods, w_in, dw_w, dw_b, ln_g, ln_b)


def kernel(x, c, ada_w, ada_b, ln_g, ln_b, dn_w_in, dn_conv_w, dn_a_log, dn_dt_bias, dn_norm_w,
           dn_w_out, cf_w_in, cf_dw_w, cf_dw_b, cf_ln_g, cf_ln_b, cf_w_out, ff_w1, ff_w2):
    batch, seq, d = x.shape
    assert d == D_MODEL and seq % TM == 0 and TM % CHUNK == 0
    tiles_per_batch = seq // TM
    n_qkv = 3 * D_MODEL
    n_main = 4 * D_MODEL

    mods = _ada_mods(c, ada_w, ada_b)
    x2d = x.reshape(batch * seq, D_MODEL)
    for i in range(DEPTH):
        j = i // 2
        if i % 2 == 0:
            w_in = dn_w_in[j]
            w_main = w_in[:, :n_main].astype(BF16)
            w_ba = jnp.pad(w_in[:, n_main:], ((0, 0), (0, LANES - 2 * HEADS))).astype(BF16)
            gate_pad = (HEADS, LANES - 2 * HEADS)
            alog_vec = jnp.pad(dn_a_log[j], gate_pad).reshape(1, LANES)
            dtb_vec = jnp.pad(dn_dt_bias[j], gate_pad).reshape(1, LANES)
            q, k, v, z, bg = _dn_in(x2d, mods, i, tiles_per_batch, w_main, w_ba,
                                    dn_conv_w[j][:, :n_qkv], alog_vec, dtb_vec)
            u = _dn_core(q, k, v, z, bg, dn_norm_w[j].reshape(1, HEAD_DIM), batch, tiles_per_batch)
            w_out = dn_w_out[j]
        else:
            u = _cf_mix(x2d, mods, i, tiles_per_batch, cf_w_in[j].astype(BF16), cf_dw_w[j],
                        cf_dw_b[j].reshape(1, D_MODEL), cf_ln_g[j].reshape(1, D_MODEL),
                        cf_ln_b[j].reshape(1, D_MODEL))
            w_out = cf_w_out[j]
        x2d = _post(x2d, u, mods, i, tiles_per_batch, w_out.astype(BF16), ln_g[i], ln_b[i],
                    ff_w1[i].astype(BF16), ff_w2[i].astype(BF16))
    return x2d.reshape(batch, seq, D_MODEL)
```

```python
import functools

import jax
import jax.numpy as jnp
from jax import lax
from jax.experimental import pallas as pl
from jax.experimental.pallas import tpu as pltpu

F32 = jnp.float32
BF16 = jnp.bfloat16

D_MODEL = 1024
DEPTH = 4
N_MOD = 6
HEADS = 8
HEAD_DIM = 128
DN_CONV = 4
CHUNK = 64
BLOCK16 = 16
CF_KERNEL = 31
FF_DIM = 4 * D_MODEL
ALPHA = (2.0 * DEPTH) ** 0.25
LN_EPS = 1e-5
RMS_EPS = 1e-6
L2_EPS = 1e-6

LANES = 128
SUBLANES = 8
TM = 512
FF_TILE = 1024
CF_HALO = 32
DN_HALO = 8
PASS1_CHUNKS = 2
VMEM_LIMIT = 56 * 1024 * 1024


def _dot(a, b):
    return jnp.dot(a, b, preferred_element_type=F32)


def _dot_nt(a, b):
    return lax.dot_general(a, b, (((1,), (1,)), ((), ())), preferred_element_type=F32)


def _dot_tn(a, b):
    return lax.dot_general(a, b, (((0,), (0,)), ((), ())), preferred_element_type=F32)


def _bdot(a, b):
    return _dot(a.astype(BF16), b.astype(BF16))


def _split3(x):
    x1 = x.astype(BF16)
    r1 = x - x1.astype(F32)
    x2 = r1.astype(BF16)
    x3 = (r1 - x2.astype(F32)).astype(BF16)
    return x1, x2, x3


def _dot_exact_lhs(a_bf16, x):
    x1, x2, x3 = _split3(x)
    return _dot(a_bf16, x3) + _dot(a_bf16, x2) + _dot(a_bf16, x1)


def _layer_norm(r, g, b):
    mu = jnp.mean(r, axis=-1, keepdims=True)
    d = r - mu
    var = jnp.mean(d * d, axis=-1, keepdims=True)
    return d * lax.rsqrt(var + LN_EPS) * g + b


def _silu(x):
    return x * jax.nn.sigmoid(x)


def _softplus(x):
    return jnp.maximum(x, 0.0) + jnp.log1p(jnp.exp(-jnp.abs(x)))


def _const_spec(shape):
    nd = len(shape)
    return pl.BlockSpec(shape, lambda *_: (0,) * nd, pipeline_mode=pl.Buffered(1))


def _mod_spec(row, tiles_per_batch):
    return pl.BlockSpec((1, 1, 1, D_MODEL), lambda t: (row, t // tiles_per_batch, 0, 0))


def _ada_kernel(c_ref, w_ref, b_ref, o_ref):
    cond = _silu(c_ref[...])
    o_ref[0] = _dot(cond, w_ref[0]) + b_ref[0]


def _ada_mods(c, ada_w, ada_b):
    batch = c.shape[0]
    out = pl.pallas_call(
        _ada_kernel,
        grid=(DEPTH, N_MOD),
        in_specs=[
            pl.BlockSpec((batch, D_MODEL), lambda i, j: (0, 0)),
            pl.BlockSpec((1, D_MODEL, D_MODEL), lambda i, j: (i, 0, j)),
            pl.BlockSpec((1, 1, D_MODEL), lambda i, j: (i, 0, j)),
        ],
        out_specs=pl.BlockSpec((1, batch, D_MODEL), lambda i, j: (i * N_MOD + j, 0, 0)),
        out_shape=jax.ShapeDtypeStruct((DEPTH * N_MOD, batch, D_MODEL), F32),
        compiler_params=pltpu.CompilerParams(dimension_semantics=("arbitrary", "arbitrary")),
        name="ada_mods",
    )(c, ada_w, ada_b.reshape(DEPTH, 1, N_MOD * D_MODEL))
    return out.reshape(DEPTH * N_MOD, batch, 1, D_MODEL)


def _post_kernel(x_ref, u_ref, gt1_ref, sh2_ref, sc2_ref, gt2_ref, wo_ref, lng_ref, lnb_ref,
                 w1_ref, w2_ref, o_ref):
    x = x_ref[...]
    y = _dot(u_ref[...], wo_ref[...])
    x1 = _layer_norm(ALPHA * x + (1.0 + gt1_ref[0, 0]) * y, lng_ref[0:1, :], lnb_ref[0:1, :])
    h = (x1 * (1.0 + sc2_ref[0, 0]) + sh2_ref[0, 0]).astype(BF16)
    acc = None
    for c in range(FF_DIM // FF_TILE):
        cols = slice(c * FF_TILE, (c + 1) * FF_TILE)
        a = jnp.maximum(_dot(h, w1_ref[:, cols]), 0.0)
        p = _dot((a * a).astype(BF16), w2_ref[cols, :])
        acc = p if acc is None else acc + p
    o_ref[...] = _layer_norm(ALPHA * x1 + (1.0 + gt2_ref[0, 0]) * acc, lng_ref[1:2, :], lnb_ref[1:2, :])


def _post(x2d, u2d, mods, layer, tiles_per_batch, w_out, ln_g, ln_b, w1, w2):
    rows = x2d.shape[0]
    row_spec = pl.BlockSpec((TM, D_MODEL), lambda t: (t, 0))
    base = layer * N_MOD
    return pl.pallas_call(
        _post_kernel,
        grid=(rows // TM,),
        in_specs=[
            row_spec, row_spec,
            _mod_spec(base + 2, tiles_per_batch), _mod_spec(base + 3, tiles_per_batch),
            _mod_spec(base + 4, tiles_per_batch), _mod_spec(base + 5, tiles_per_batch),
            _const_spec((D_MODEL, D_MODEL)), _const_spec((2, D_MODEL)), _const_spec((2, D_MODEL)),
            _const_spec((D_MODEL, FF_DIM)), _const_spec((FF_DIM, D_MODEL)),
        ],
        out_specs=row_spec,
        out_shape=jax.ShapeDtypeStruct((rows, D_MODEL), F32),
        compiler_params=pltpu.CompilerParams(dimension_semantics=("arbitrary",),
                                             vmem_limit_bytes=VMEM_LIMIT),
        name="post_mlp",
    )(x2d, u2d, mods, mods, mods, mods, w_out, ln_g, ln_b, w1, w2)


def _dn_in_kernel(x_ref, sh_ref, sc_ref, w_ref, wba_ref, cw_ref, alog_ref, dtb_ref,
                  q_ref, k_ref, v_ref, z_ref, bg_ref, ext_ref, *, tiles_per_batch):
    t = pl.program_id(0)

    @pl.when(t % tiles_per_batch == 0)
    def _():
        ext_ref[0:DN_HALO, :] = jnp.zeros((DN_HALO, 3 * D_MODEL), F32)

    h = (x_ref[...] * (1.0 + sc_ref[0, 0]) + sh_ref[0, 0]).astype(BF16)

    for grp, out_ref in enumerate((q_ref, k_ref, v_ref)):
        cols = slice(grp * D_MODEL, (grp + 1) * D_MODEL)
        ext_ref[DN_HALO:DN_HALO + TM, cols] = _dot(h, w_ref[:, cols])
        acc = None
        for j in range(DN_CONV):
            off = DN_HALO - (DN_CONV - 1) + j
            term = cw_ref[j:j + 1, cols] * ext_ref[off:off + TM, cols]
            acc = term if acc is None else acc + term
        ext_ref[0:DN_HALO, cols] = ext_ref[TM:TM + DN_HALO, cols]
        y = _silu(acc)
        if out_ref is v_ref:
            out_ref[...] = y
        else:
            for hd in range(HEADS):
                lanes = slice(hd * HEAD_DIM, (hd + 1) * HEAD_DIM)
                ys = y[:, lanes]
                ss = jnp.sum(ys * ys, axis=-1, keepdims=True)
                out_ref[:, lanes] = ys * lax.rsqrt(ss + L2_EPS)

    z_ref[...] = _dot(h, w_ref[:, 3 * D_MODEL:4 * D_MODEL])

    ba = _dot(h, wba_ref[...])
    beta = jax.nn.sigmoid(ba)
    g = -jnp.exp(alog_ref[...]) * _softplus(ba + dtb_ref[...])
    lane = lax.broadcasted_iota(jnp.int32, ba.shape, 1)
    bg_ref[...] = jnp.where(lane < HEADS, beta, g)


def _dn_in(x2d, mods, layer, tiles_per_batch, w_main, w_ba, conv_w, alog_vec, dtb_vec):
    rows = x2d.shape[0]
    row_spec = pl.BlockSpec((TM, D_MODEL), lambda t: (t, 0))
    base = layer * N_MOD
    out_sds = jax.ShapeDtypeStruct((rows, D_MODEL), F32)
    return pl.pallas_call(
        functools.partial(_dn_in_kernel, tiles_per_batch=tiles_per_batch),
        grid=(rows // TM,),
        in_specs=[
            row_spec, _mod_spec(base + 0, tiles_per_batch), _mod_spec(base + 1, tiles_per_batch),
            _const_spec((D_MODEL, 4 * D_MODEL)), _const_spec((D_MODEL, LANES)),
            _const_spec((DN_CONV, 3 * D_MODEL)), _const_spec((1, LANES)), _const_spec((1, LANES)),
        ],
        out_specs=[row_spec, row_spec, row_spec, row_spec, pl.BlockSpec((TM, LANES), lambda t: (t, 0))],
        out_shape=[out_sds, out_sds, out_sds, out_sds, jax.ShapeDtypeStruct((rows, LANES), F32)],
        scratch_shapes=[pltpu.VMEM((DN_HALO + TM, 3 * D_MODEL), F32)],
        compiler_params=pltpu.CompilerParams(dimension_semantics=("arbitrary",),
                                             vmem_limit_bytes=VMEM_LIMIT),
        name="dn_in",
    )(x2d, mods, mods, w_main, w_ba, conv_w, alog_vec, dtb_vec)


def _inverse_minus_eye_many(a_list):
    ri = lax.broadcasted_iota(jnp.int32, (CHUNK, CHUNK), 0) // BLOCK16
    ci = lax.broadcasted_iota(jnp.int32, (CHUNK, CHUNK), 1) // BLOCK16
    on_diag = ri == ci
    b16 = lambda xs: [x.astype(BF16) for x in xs]
    mm = lambda xs, ys: [_dot(x, y) for x, y in zip(xs, ys)]

    d = [jnp.where(on_diag, a, 0.0) for a in a_list]
    low = [a - x for a, x in zip(a_list, d)]
    d_16 = b16(d)
    d2 = mm(d_16, d_16)
    d2_16 = b16(d2)
    d4 = mm(d2_16, d2_16)
    dd2 = mm(d_16, d2_16)
    p = [x2 - x - y for x, x2, y in zip(d, d2, dd2)]
    d4_16 = b16(d4)
    d8 = mm(d4_16, d4_16)
    pd4 = mm(b16(p), d4_16)
    p = [x + y + z for x, y, z in zip(p, d4, pd4)]
    pd8 = mm(b16(p), b16(d8))
    p = [x + y + z for x, y, z in zip(p, d8, pd8)]
    p_16 = b16(p)
    pl_ = mm(p_16, b16(low))
    m = [x + y for x, y in zip(low, pl_)]
    m_16 = b16(m)
    m2 = mm(m_16, m_16)
    mm2 = mm(m_16, b16(m2))
    q = [x2 - x - y for x, x2, y in zip(m, m2, mm2)]
    qp = mm(b16(q), p_16)
    return [x + y + z for x, y, z in zip(q, p, qp)]


def _dn_core_kernel(q_ref, k_ref, v_ref, z_ref, bg_ref, nw_ref, o_ref,
                    state_ref, gam_ref, egam_ref, edec_ref, egl_ref,
                    u_ref, w_ref, qd_ref, kdt_ref, a_ref):
    @pl.when(pl.program_id(1) == 0)
    def _():
        state_ref[...] = jnp.zeros(state_ref.shape, F32)

    ri = lax.broadcasted_iota(jnp.int32, (TM, TM), 0)
    ci = lax.broadcasted_iota(jnp.int32, (TM, TM), 1)
    same_chunk = (ri // CHUNK) == (ci // CHUNK)
    cum_mat = jnp.where(same_chunk & (ci <= ri), 1.0, 0.0).astype(BF16)
    tot_mat = jnp.where(same_chunk, 1.0, 0.0).astype(BF16)
    bg = bg_ref[...]
    gam = _dot_exact_lhs(cum_mat, bg)
    gtot = _dot_exact_lhs(tot_mat, bg)
    gam_ref[...] = gam
    egam_ref[...] = jnp.exp(gam)
    edec_ref[...] = jnp.exp(gtot - gam)
    egl_ref[...] = jnp.exp(gtot)

    r_idx = lax.broadcasted_iota(jnp.int32, (CHUNK, CHUNK), 0)
    c_idx = lax.broadcasted_iota(jnp.int32, (CHUNK, CHUNK), 1)
    causal = c_idx <= r_idx
    strict = c_idx < r_idx
    scale = HEAD_DIM ** -0.5
    wide = (CHUNK, HEAD_DIM)

    def independent_part(grp, carry):
        probs = []
        for ci_ in range(PASS1_CHUNKS):
            cidx = grp * PASS1_CHUNKS + ci_
            rows = pl.ds(pl.multiple_of(cidx * CHUNK, CHUNK), CHUNK)
            gam_c = gam_ref[rows, :]
            gam_t = gam_c.T
            bg_c = bg_ref[rows, :]
            egam_c = egam_ref[rows, :]
            edec_c = edec_ref[rows, :]
            for hd in range(HEADS):
                lanes = slice(hd * HEAD_DIM, (hd + 1) * HEAD_DIM)
                gl = HEADS + hd
                beta = jnp.broadcast_to(bg_c[:, hd:hd + 1], wide)
                eg = jnp.broadcast_to(egam_c[:, gl:gl + 1], wide)
                ed = jnp.broadcast_to(edec_c[:, gl:gl + 1], wide)
                gcol = jnp.broadcast_to(gam_c[:, gl:gl + 1], wide)[:, :CHUNK]
                decay = jnp.exp(jnp.where(causal, gcol - gam_t[gl:gl + 1, :], -jnp.inf))
                probs.append(dict(cidx=cidx, rows=rows, hd=hd, lanes=lanes, beta=beta, eg=eg, ed=ed,
                                  decay=decay, q=q_ref[rows, lanes], k=k_ref[rows, lanes],
                                  v=v_ref[rows, lanes]))
        for p in probs:
            p["k16"] = p["k"].astype(BF16)
            p["kq16"] = jnp.concatenate([p["k16"], (p["q"] * scale).astype(BF16)], axis=0)
        kkqk = [_dot_nt(p["kq16"], p["k16"]) for p in probs]
        a_list = []
        for p, s in zip(probs, kkqk):
            a_list.append(jnp.where(strict, s[:CHUNK] * p["beta"][:, :CHUNK] * p["decay"], 0.0))
            a_ref[p["hd"], p["rows"], :] = (s[CHUNK:] * p["decay"]).astype(BF16)
        tinv = _inverse_minus_eye_many(a_list)
        for p in probs:
            p["vb"] = p["v"] * p["beta"]
            p["kbg"] = p["k"] * p["beta"] * p["eg"]
        uw = [_dot(t.astype(BF16), jnp.concatenate([p["vb"], p["kbg"]], axis=1).astype(BF16))
              for p, t in zip(probs, tinv)]
        for p, r in zip(probs, uw):
            rows, lanes = p["rows"], p["lanes"]
            u_ref[rows, lanes] = p["vb"] + r[:, :HEAD_DIM]
            w_ref[rows, lanes] = (p["kbg"] + r[:, HEAD_DIM:]).astype(BF16)
            qd_ref[rows, lanes] = (p["q"] * (p["eg"] * scale)).astype(BF16)
            kdt_ref[p["hd"], p["cidx"]] = (p["k"] * p["ed"]).T.astype(BF16)
        return carry

    lax.fori_loop(0, TM // CHUNK // PASS1_CHUNKS, independent_part, 0)

    nw = nw_ref[...]

    def recurrent_part(c, carry):
        rows = pl.ds(pl.multiple_of(c * CHUNK, CHUNK), CHUNK)
        egl_c = egl_ref[rows, :]
        heads = range(HEADS)
        lanes = [slice(hd * HEAD_DIM, (hd + 1) * HEAD_DIM) for hd in heads]
        state = [state_ref[hd] for hd in heads]
        wq = [jnp.concatenate([w_ref[rows, lanes[hd]], qd_ref[rows, lanes[hd]]], axis=0) for hd in heads]
        r = [_dot(wq[hd], state[hd].astype(BF16)) for hd in heads]
        vn16 = [(u_ref[rows, lanes[hd]] - r[hd][:CHUNK]).astype(BF16) for hd in heads]
        av = [_dot(a_ref[hd, rows, :], vn16[hd]) for hd in heads]
        kv = [_dot(kdt_ref[hd, c], vn16[hd]) for hd in heads]
        for hd in heads:
            state_ref[hd] = state[hd] * egl_c[0:1, HEADS + hd:HEADS + hd + 1] + kv[hd]
            o = r[hd][CHUNK:] + av[hd]
            on = o * lax.rsqrt(jnp.mean(o * o, axis=-1, keepdims=True) + RMS_EPS) * nw
            o_ref[rows, lanes[hd]] = (on * _silu(z_ref[rows, lanes[hd]])).astype(BF16)
        return carry

    lax.fori_loop(0, TM // CHUNK, recurrent_part, 0)


def _dn_core(q, k, v, z, bg, norm_w, batch, tiles_per_batch):
    rows = q.shape[0]
    row_spec = pl.BlockSpec((TM, D_MODEL), lambda b, s: (b * tiles_per_batch + s, 0))
    small = pltpu.VMEM((TM, LANES), F32)
    return pl.pallas_call(
        _dn_core_kernel,
        grid=(batch, tiles_per_batch),
        in_specs=[row_spec, row_spec, row_spec, row_spec,
                  pl.BlockSpec((TM, LANES), lambda b, s: (b * tiles_per_batch + s, 0)),
                  pl.BlockSpec((1, HEAD_DIM), lambda b, s: (0, 0))],
        out_specs=row_spec,
        out_shape=jax.ShapeDtypeStruct((rows, D_MODEL), BF16),
        scratch_shapes=[
            pltpu.VMEM((HEADS, HEAD_DIM, HEAD_DIM), F32),
            small, small, small, small,
            pltpu.VMEM((TM, D_MODEL), F32),
            pltpu.VMEM((TM, D_MODEL), BF16),
            pltpu.VMEM((TM, D_MODEL), BF16),
            pltpu.VMEM((HEADS, TM // CHUNK, HEAD_DIM, CHUNK), BF16),
            pltpu.VMEM((HEADS, TM, CHUNK), BF16),
        ],
        compiler_params=pltpu.CompilerParams(dimension_semantics=("arbitrary", "arbitrary"),
                                             vmem_limit_bytes=VMEM_LIMIT),
        name="dn_core",
    )(q, k, v, z, bg, norm_w)


def _cf_kernel(x_ref, sh_ref, sc_ref, w_ref, dw_ref, dwb_ref, g_ref, b_ref, o_ref, ext_ref,
               *, tiles_per_batch):
    t = pl.program_id(0)

    @pl.when(t % tiles_per_batch == 0)
    def _():
        ext_ref[0:CF_HALO, :] = jnp.zeros((CF_HALO, D_MODEL), F32)

    h = (x_ref[...] * (1.0 + sc_ref[0, 0]) + sh_ref[0, 0]).astype(BF16)
    val = _dot(h, w_ref[:, 0:D_MODEL])
    gate = _dot(h, w_ref[:, D_MODEL:2 * D_MODEL])
    ext_ref[CF_HALO:CF_HALO + TM, :] = val * jax.nn.sigmoid(gate)

    acc = None
    for j in range(CF_KERNEL):
        off = CF_HALO - (CF_KERNEL - 1) + j
        term = dw_ref[j:j + 1, :] * ext_ref[off:off + TM, :]
        acc = term if acc is None else acc + term
    ext_ref[0:CF_HALO, :] = ext_ref[TM:TM + CF_HALO, :]
    u = _layer_norm(acc + dwb_ref[...], g_ref[...], b_ref[...])
    o_ref[...] = _silu(u).astype(BF16)


def _cf_mix(x2d, mods, layer, tiles_per_batch, w_in, dw_w, dw_b, ln_g, ln_b):
    rows = x2d.shape[0]
    row_spec = pl.BlockSpec((TM, D_MODEL), lambda t: (t, 0))
    base = layer * N_MOD
    return pl.pallas_call(
        functools.partial(_cf_kernel, tiles_per_batch=tiles_per_batch),
        grid=(rows // TM,),
        in_specs=[
            row_spec, _mod_spec(base + 0, tiles_per_batch), _mod_spec(base + 1, tiles_per_batch),
            _const_spec((D_MODEL, 2 * D_MODEL)), _const_spec((CF_KERNEL, D_MODEL)),
            _const_spec((1, D_MODEL)), _const_spec((1, D_MODEL)), _const_spec((1, D_MODEL)),
        ],
        out_specs=row_spec,
        out_shape=jax.ShapeDtypeStruct((rows, D_MODEL), BF16),
        scratch_shapes=[pltpu.VMEM((CF_HALO + TM, D_MODEL), F32)],
        compiler_params=pltpu.CompilerParams(dimension_semantics=("arbitrary",),
                                             vmem_limit_bytes=VMEM_LIMIT),
        name="cf_mix",
    )(x2d, mods, mods, w_in, dw_w, dw_b, ln_g, ln_b)


def kernel(x, c, ada_w, ada_b, ln_g, ln_b, dn_w_in, dn_conv_w, dn_a_log, dn_dt_bias, dn_norm_w,
           dn_w_out, cf_w_in, cf_dw_w, cf_dw_b, cf_ln_g, cf_ln_b, cf_w_out, ff_w1, ff_w2):
    batch, seq, d = x.shape
    assert d == D_MODEL and seq % TM == 0 and TM % CHUNK == 0
    tiles_per_batch = seq // TM
    n_qkv = 3 * D_MODEL
    n_main = 4 * D_MODEL

    mods = _ada_mods(c, ada_w, ada_b)
    x2d = x.reshape(batch * seq, D_MODEL)
    for i in range(DEPTH):
        j = i // 2
        if i % 2 == 0:
            w_in = dn_w_in[j]
            w_main = w_in[:, :n_main].astype(BF16)
            w_ba = jnp.pad(w_in[:, n_main:], ((0, 0), (0, LANES - 2 * HEADS))).astype(BF16)
            gate_pad = (HEADS, LANES - 2 * HEADS)
            alog_vec = jnp.pad(dn_a_log[j], gate_pad).reshape(1, LANES)
            dtb_vec = jnp.pad(dn_dt_bias[j], gate_pad).reshape(1, LANES)
            q, k, v, z, bg = _dn_in(x2d, mods, i, tiles_per_batch, w_main, w_ba,
                                    dn_conv_w[j][:, :n_qkv], alog_vec, dtb_vec)
            u = _dn_core(q, k, v, z, bg, dn_norm_w[j].reshape(1, HEAD_DIM), batch, tiles_per_batch)
            w_out = dn_w_out[j]
        else:
            u = _cf_mix(x2d, mods, i, tiles_per_batch, cf_w_in[j].astype(BF16), cf_dw_w[j],
                        cf_dw_b[j].reshape(1, D_MODEL), cf_ln_g[j].reshape(1, D_MODEL),
                        cf_ln_b[j].reshape(1, D_MODEL))
            w_out = cf_w_out[j]
        x2d = _post(x2d, u, mods, i, tiles_per_batch, w_out.astype(BF16), ln_g[i], ln_b[i],
                    ff_w1[i].astype(BF16), ff_w2[i].astype(BF16))
    return x2d.reshape(batch, seq, D_MODEL)
```

```python
import functools

import jax
import jax.numpy as jnp
from jax import lax
from jax.experimental import pallas as pl
from jax.experimental.pallas import tpu as pltpu

F32 = jnp.float32
BF16 = jnp.bfloat16

D_MODEL = 1024
DEPTH = 4
N_MOD = 6
HEADS = 8
HEAD_DIM = 128
DN_CONV = 4
CHUNK = 64
BLOCK16 = 16
CF_KERNEL = 31
FF_DIM = 4 * D_MODEL
ALPHA = (2.0 * DEPTH) ** 0.25
LN_EPS = 1e-5
RMS_EPS = 1e-6
L2_EPS = 1e-6

LANES = 128
SUBLANES = 8
TM = 512
FF_TILE = 1024
CF_HALO = 32
DN_HALO = 8
PASS1_CHUNKS = 2
VMEM_LIMIT = 56 * 1024 * 1024


def _dot(a, b):
    return jnp.dot(a, b, preferred_element_type=F32)


def _dot_nt(a, b):
    return lax.dot_general(a, b, (((1,), (1,)), ((), ())), preferred_element_type=F32)


def _dot_tn(a, b):
    return lax.dot_general(a, b, (((0,), (0,)), ((), ())), preferred_element_type=F32)


def _bdot(a, b):
    return _dot(a.astype(BF16), b.astype(BF16))


def _split3(x):
    x1 = x.astype(BF16)
    r1 = x - x1.astype(F32)
    x2 = r1.astype(BF16)
    x3 = (r1 - x2.astype(F32)).astype(BF16)
    return x1, x2, x3


def _dot_exact_lhs(a_bf16, x):
    x1, x2, x3 = _split3(x)
    return _dot(a_bf16, x3) + _dot(a_bf16, x2) + _dot(a_bf16, x1)


def _layer_norm(r, g, b):
    mu = jnp.mean(r, axis=-1, keepdims=True)
    d = r - mu
    var = jnp.mean(d * d, axis=-1, keepdims=True)
    return d * lax.rsqrt(var + LN_EPS) * g + b


def _silu(x):
    return x * jax.nn.sigmoid(x)


def _softplus(x):
    return jnp.maximum(x, 0.0) + jnp.log1p(jnp.exp(-jnp.abs(x)))


def _const_spec(shape):
    nd = len(shape)
    return pl.BlockSpec(shape, lambda *_: (0,) * nd, pipeline_mode=pl.Buffered(1))


def _mod_spec(row, tiles_per_batch):
    return pl.BlockSpec((1, 1, 1, D_MODEL), lambda t: (row, t // tiles_per_batch, 0, 0))


def _ada_kernel(c_ref, w_ref, b_ref, o_ref):
    cond = _silu(c_ref[...])
    o_ref[0] = _dot(cond, w_ref[0]) + b_ref[0]


def _ada_mods(c, ada_w, ada_b):
    batch = c.shape[0]
    out = pl.pallas_call(
        _ada_kernel,
        grid=(DEPTH, N_MOD),
        in_specs=[
            pl.BlockSpec((batch, D_MODEL), lambda i, j: (0, 0)),
            pl.BlockSpec((1, D_MODEL, D_MODEL), lambda i, j: (i, 0, j)),
            pl.BlockSpec((1, 1, D_MODEL), lambda i, j: (i, 0, j)),
        ],
        out_specs=pl.BlockSpec((1, batch, D_MODEL), lambda i, j: (i * N_MOD + j, 0, 0)),
        out_shape=jax.ShapeDtypeStruct((DEPTH * N_MOD, batch, D_MODEL), F32),
        compiler_params=pltpu.CompilerParams(dimension_semantics=("arbitrary", "arbitrary")),
        name="ada_mods",
    )(c, ada_w, ada_b.reshape(DEPTH, 1, N_MOD * D_MODEL))
    return out.reshape(DEPTH * N_MOD, batch, 1, D_MODEL)


def _post_kernel(x_ref, u_ref, gt1_ref, sh2_ref, sc2_ref, gt2_ref, wo_ref, lng_ref, lnb_ref,
                 w1_ref, w2_ref, o_ref):
    x = x_ref[...]
    y = _dot(u_ref[...], wo_ref[...])
    x1 = _layer_norm(ALPHA * x + (1.0 + gt1_ref[0, 0]) * y, lng_ref[0:1, :], lnb_ref[0:1, :])
    h = (x1 * (1.0 + sc2_ref[0, 0]) + sh2_ref[0, 0]).astype(BF16)
    acc = None
    for c in range(FF_DIM // FF_TILE):
        cols = slice(c * FF_TILE, (c + 1) * FF_TILE)
        a = jnp.maximum(_dot(h, w1_ref[:, cols]), 0.0)
        p = _dot((a * a).astype(BF16), w2_ref[cols, :])
        acc = p if acc is None else acc + p
    o_ref[...] = _layer_norm(ALPHA * x1 + (1.0 + gt2_ref[0, 0]) * acc, lng_ref[1:2, :], lnb_ref[1:2, :])


def _post(x2d, u2d, mods, layer, tiles_per_batch, w_out, ln_g, ln_b, w1, w2):
    rows = x2d.shape[0]
    row_spec = pl.BlockSpec((TM, D_MODEL), lambda t: (t, 0))
    base = layer * N_MOD
    return pl.pallas_call(
        _post_kernel,
        grid=(rows // TM,),
        in_specs=[
            row_spec, row_spec,
            _mod_spec(base + 2, tiles_per_batch), _mod_spec(base + 3, tiles_per_batch),
            _mod_spec(base + 4, tiles_per_batch), _mod_spec(base + 5, tiles_per_batch),
            _const_spec((D_MODEL, D_MODEL)), _const_spec((2, D_MODEL)), _const_spec((2, D_MODEL)),
            _const_spec((D_MODEL, FF_DIM)), _const_spec((FF_DIM, D_MODEL)),
        ],
        out_specs=row_spec,
        out_shape=jax.ShapeDtypeStruct((rows, D_MODEL), F32),
        compiler_params=pltpu.CompilerParams(dimension_semantics=("arbitrary",),
                                             vmem_limit_bytes=VMEM_LIMIT),
        name="post_mlp",
    )(x2d, u2d, mods, mods, mods, mods, w_out, ln_g, ln_b, w1, w2)


def _dn_in_kernel(x_ref, sh_ref, sc_ref, w_ref, wba_ref, cw_ref, alog_ref, dtb_ref,
                  q_ref, k_ref, v_ref, z_ref, bg_ref, ext_ref, *, tiles_per_batch):
    t = pl.program_id(0)

    @pl.when(t % tiles_per_batch == 0)
    def _():
        ext_ref[0:DN_HALO, :] = jnp.zeros((DN_HALO, 3 * D_MODEL), F32)

    h = (x_ref[...] * (1.0 + sc_ref[0, 0]) + sh_ref[0, 0]).astype(BF16)

    for grp, out_ref in enumerate((q_ref, k_ref, v_ref)):
        cols = slice(grp * D_MODEL, (grp + 1) * D_MODEL)
        ext_ref[DN_HALO:DN_HALO + TM, cols] = _dot(h, w_ref[:, cols])
        ext = ext_ref[:, cols]
        acc = None
        for j in range(DN_CONV):
            off = DN_HALO - (DN_CONV - 1) + j
            res = off % SUBLANES
            shifted = ext if res == 0 else pltpu.roll(ext, DN_HALO + TM - res, axis=0)
            term = cw_ref[j:j + 1, cols] * shifted[off - res:off - res + TM, :]
            acc = term if acc is None else acc + term
        ext_ref[0:DN_HALO, cols] = ext_ref[TM:TM + DN_HALO, cols]
        y = _silu(acc)
        if out_ref is v_ref:
            out_ref[...] = y
        else:
            for hd in range(HEADS):
                lanes = slice(hd * HEAD_DIM, (hd + 1) * HEAD_DIM)
                ys = y[:, lanes]
                ss = jnp.sum(ys * ys, axis=-1, keepdims=True)
                out_ref[:, lanes] = ys * lax.rsqrt(ss + L2_EPS)

    z_ref[...] = _dot(h, w_ref[:, 3 * D_MODEL:4 * D_MODEL])

    ba = _dot(h, wba_ref[...])
    beta = jax.nn.sigmoid(ba)
    g = -jnp.exp(alog_ref[...]) * _softplus(ba + dtb_ref[...])
    lane = lax.broadcasted_iota(jnp.int32, ba.shape, 1)
    bg_ref[...] = jnp.where(lane < HEADS, beta, g)


def _dn_in(x2d, mods, layer, tiles_per_batch, w_main, w_ba, conv_w, alog_vec, dtb_vec):
    rows = x2d.shape[0]
    row_spec = pl.BlockSpec((TM, D_MODEL), lambda t: (t, 0))
    base = layer * N_MOD
    out_sds = jax.ShapeDtypeStruct((rows, D_MODEL), F32)
    return pl.pallas_call(
        functools.partial(_dn_in_kernel, tiles_per_batch=tiles_per_batch),
        grid=(rows // TM,),
        in_specs=[
            row_spec, _mod_spec(base + 0, tiles_per_batch), _mod_spec(base + 1, tiles_per_batch),
            _const_spec((D_MODEL, 4 * D_MODEL)), _const_spec((D_MODEL, LANES)),
            _const_spec((DN_CONV, 3 * D_MODEL)), _const_spec((1, LANES)), _const_spec((1, LANES)),
        ],
        out_specs=[row_spec, row_spec, row_spec, row_spec, pl.BlockSpec((TM, LANES), lambda t: (t, 0))],
        out_shape=[out_sds, out_sds, out_sds, out_sds, jax.ShapeDtypeStruct((rows, LANES), F32)],
        scratch_shapes=[pltpu.VMEM((DN_HALO + TM, 3 * D_MODEL), F32)],
        compiler_params=pltpu.CompilerParams(dimension_semantics=("arbitrary",),
                                             vmem_limit_bytes=VMEM_LIMIT),
        name="dn_in",
    )(x2d, mods, mods, w_main, w_ba, conv_w, alog_vec, dtb_vec)


def _inverse_minus_eye_many(a_list):
    ri = lax.broadcasted_iota(jnp.int32, (CHUNK, CHUNK), 0) // BLOCK16
    ci = lax.broadcasted_iota(jnp.int32, (CHUNK, CHUNK), 1) // BLOCK16
    on_diag = ri == ci
    b16 = lambda xs: [x.astype(BF16) for x in xs]
    mm = lambda xs, ys: [_dot(x, y) for x, y in zip(xs, ys)]

    d = [jnp.where(on_diag, a, 0.0) for a in a_list]
    low = [a - x for a, x in zip(a_list, d)]
    d_16 = b16(d)
    d2 = mm(d_16, d_16)
    d2_16 = b16(d2)
    d4 = mm(d2_16, d2_16)
    dd2 = mm(d_16, d2_16)
    p = [x2 - x - y for x, x2, y in zip(d, d2, dd2)]
    d4_16 = b16(d4)
    d8 = mm(d4_16, d4_16)
    pd4 = mm(b16(p), d4_16)
    p = [x + y + z for x, y, z in zip(p, d4, pd4)]
    pd8 = mm(b16(p), b16(d8))
    p = [x + y + z for x, y, z in zip(p, d8, pd8)]
    p_16 = b16(p)
    pl_ = mm(p_16, b16(low))
    m = [x + y for x, y in zip(low, pl_)]
    m_16 = b16(m)
    m2 = mm(m_16, m_16)
    mm2 = mm(m_16, b16(m2))
    q = [x2 - x - y for x, x2, y in zip(m, m2, mm2)]
    qp = mm(b16(q), p_16)
    return [x + y + z for x, y, z in zip(q, p, qp)]


def _dn_core_kernel(q_ref, k_ref, v_ref, z_ref, bg_ref, nw_ref, o_ref,
                    state_ref, gam_ref, egam_ref, edec_ref, egl_ref,
                    u_ref, w_ref, qd_ref, kdt_ref, a_ref):
    @pl.when(pl.program_id(1) == 0)
    def _():
        state_ref[...] = jnp.zeros(state_ref.shape, F32)

    ri = lax.broadcasted_iota(jnp.int32, (TM, TM), 0)
    ci = lax.broadcasted_iota(jnp.int32, (TM, TM), 1)
    same_chunk = (ri // CHUNK) == (ci // CHUNK)
    cum_mat = jnp.where(same_chunk & (ci <= ri), 1.0, 0.0).astype(BF16)
    tot_mat = jnp.where(same_chunk, 1.0, 0.0).astype(BF16)
    bg = bg_ref[...]
    gam = _dot_exact_lhs(cum_mat, bg)
    gtot = _dot_exact_lhs(tot_mat, bg)
    gam_ref[...] = gam
    egam_ref[...] = jnp.exp(gam)
    edec_ref[...] = jnp.exp(gtot - gam)
    egl_ref[...] = jnp.exp(gtot)

    r_idx = lax.broadcasted_iota(jnp.int32, (CHUNK, CHUNK), 0)
    c_idx = lax.broadcasted_iota(jnp.int32, (CHUNK, CHUNK), 1)
    causal = c_idx <= r_idx
    strict = c_idx < r_idx
    scale = HEAD_DIM ** -0.5
    wide = (CHUNK, HEAD_DIM)

    def independent_part(grp, carry):
        probs = []
        for ci_ in range(PASS1_CHUNKS):
            cidx = grp * PASS1_CHUNKS + ci_
            rows = pl.ds(pl.multiple_of(cidx * CHUNK, CHUNK), CHUNK)
            gam_c = gam_ref[rows, :]
            gam_t = gam_c.T
            bg_c = bg_ref[rows, :]
            egam_c = egam_ref[rows, :]
            edec_c = edec_ref[rows, :]
            for hd in range(HEADS):
                lanes = slice(hd * HEAD_DIM, (hd + 1) * HEAD_DIM)
                gl = HEADS + hd
                beta = jnp.broadcast_to(bg_c[:, hd:hd + 1], wide)
                eg = jnp.broadcast_to(egam_c[:, gl:gl + 1], wide)
                ed = jnp.broadcast_to(edec_c[:, gl:gl + 1], wide)
                gcol = jnp.broadcast_to(gam_c[:, gl:gl + 1], wide)[:, :CHUNK]
                decay = jnp.exp(jnp.where(causal, gcol - gam_t[gl:gl + 1, :], -jnp.inf))
                probs.append(dict(cidx=cidx, rows=rows, hd=hd, lanes=lanes, beta=beta, eg=eg, ed=ed,
                                  decay=decay, q=q_ref[rows, lanes], k=k_ref[rows, lanes],
                                  v=v_ref[rows, lanes]))
        for p in probs:
            p["k16"] = p["k"].astype(BF16)
            p["kq16"] = jnp.concatenate([p["k16"], (p["q"] * scale).astype(BF16)], axis=0)
        kkqk = [_dot_nt(p["kq16"], p["k16"]) for p in probs]
        a_list = []
        for p, s in zip(probs, kkqk):
            a_list.append(jnp.where(strict, s[:CHUNK] * p["beta"][:, :CHUNK] * p["decay"], 0.0))
            a_ref[p["hd"], p["rows"], :] = (s[CHUNK:] * p["decay"]).astype(BF16)
        tinv = _inverse_minus_eye_many(a_list)
        for p in probs:
            p["vb"] = p["v"] * p["beta"]
            p["kbg"] = p["k"] * p["beta"] * p["eg"]
        uw = [_dot(t.astype(BF16), jnp.concatenate([p["vb"], p["kbg"]], axis=1).astype(BF16))
              for p, t in zip(probs, tinv)]
        for p, r in zip(probs, uw):
            rows, lanes = p["rows"], p["lanes"]
            u_ref[rows, lanes] = p["vb"] + r[:, :HEAD_DIM]
            w_ref[rows, lanes] = (p["kbg"] + r[:, HEAD_DIM:]).astype(BF16)
            qd_ref[rows, lanes] = (p["q"] * (p["eg"] * scale)).astype(BF16)
            kdt_ref[p["hd"], p["cidx"]] = (p["k"] * p["ed"]).T.astype(BF16)
        return carry

    lax.fori_loop(0, TM // CHUNK // PASS1_CHUNKS, independent_part, 0)

    nw = nw_ref[...]

    def recurrent_part(c, carry):
        rows = pl.ds(pl.multiple_of(c * CHUNK, CHUNK), CHUNK)
        egl_c = egl_ref[rows, :]
        heads = range(HEADS)
        lanes = [slice(hd * HEAD_DIM, (hd + 1) * HEAD_DIM) for hd in heads]
        state = [state_ref[hd] for hd in heads]
        wq = [jnp.concatenate([w_ref[rows, lanes[hd]], qd_ref[rows, lanes[hd]]], axis=0) for hd in heads]
        r = [_dot(wq[hd], state[hd].astype(BF16)) for hd in heads]
        vn16 = [(u_ref[rows, lanes[hd]] - r[hd][:CHUNK]).astype(BF16) for hd in heads]
        av = [_dot(a_ref[hd, rows, :], vn16[hd]) for hd in heads]
        kv = [_dot(kdt_ref[hd, c], vn16[hd]) for hd in heads]
        for hd in heads:
            state_ref[hd] = state[hd] * egl_c[0:1, HEADS + hd:HEADS + hd + 1] + kv[hd]
            o = r[hd][CHUNK:] + av[hd]
            on = o * lax.rsqrt(jnp.mean(o * o, axis=-1, keepdims=True) + RMS_EPS) * nw
            o_ref[rows, lanes[hd]] = (on * _silu(z_ref[rows, lanes[hd]])).astype(BF16)
        return carry

    lax.fori_loop(0, TM // CHUNK, recurrent_part, 0)


def _dn_core(q, k, v, z, bg, norm_w, batch, tiles_per_batch):
    rows = q.shape[0]
    row_spec = pl.BlockSpec((TM, D_MODEL), lambda b, s: (b * tiles_per_batch + s, 0))
    small = pltpu.VMEM((TM, LANES), F32)
    return pl.pallas_call(
        _dn_core_kernel,
        grid=(batch, tiles_per_batch),
        in_specs=[row_spec, row_spec, row_spec, row_spec,
                  pl.BlockSpec((TM, LANES), lambda b, s: (b * tiles_per_batch + s, 0)),
                  pl.BlockSpec((1, HEAD_DIM), lambda b, s: (0, 0))],
        out_specs=row_spec,
        out_shape=jax.ShapeDtypeStruct((rows, D_MODEL), BF16),
        scratch_shapes=[
            pltpu.VMEM((HEADS, HEAD_DIM, HEAD_DIM), F32),
            small, small, small, small,
            pltpu.VMEM((TM, D_MODEL), F32),
            pltpu.VMEM((TM, D_MODEL), BF16),
            pltpu.VMEM((TM, D_MODEL), BF16),
            pltpu.VMEM((HEADS, TM // CHUNK, HEAD_DIM, CHUNK), BF16),
            pltpu.VMEM((HEADS, TM, CHUNK), BF16),
        ],
        compiler_params=pltpu.CompilerParams(dimension_semantics=("arbitrary", "arbitrary"),
                                             vmem_limit_bytes=VMEM_LIMIT),
        name="dn_core",
    )(q, k, v, z, bg, norm_w)


def _cf_kernel(x_ref, sh_ref, sc_ref, w_ref, dw_ref, dwb_ref, g_ref, b_ref, o_ref, ext_ref,
               *, tiles_per_batch):
    t = pl.program_id(0)

    @pl.when(t % tiles_per_batch == 0)
    def _():
        ext_ref[0:CF_HALO, :] = jnp.zeros((CF_HALO, D_MODEL), F32)

    h = (x_ref[...] * (1.0 + sc_ref[0, 0]) + sh_ref[0, 0]).astype(BF16)
    val = _dot(h, w_ref[:, 0:D_MODEL])
    gate = _dot(h, w_ref[:, D_MODEL:2 * D_MODEL])
    ext_ref[CF_HALO:CF_HALO + TM, :] = val * jax.nn.sigmoid(gate)

    ext = ext_ref[...]
    acc = None
    for res in range(SUBLANES):
        shifted = ext if res == 0 else pltpu.roll(ext, CF_HALO + TM - res, axis=0)
        for j in range(CF_KERNEL):
            off = CF_HALO - (CF_KERNEL - 1) + j
            if off % SUBLANES != res:
                continue
            base = off - res
            term = dw_ref[j:j + 1, :] * shifted[base:base + TM, :]
            acc = term if acc is None else acc + term
    ext_ref[0:CF_HALO, :] = ext_ref[TM:TM + CF_HALO, :]
    u = _layer_norm(acc + dwb_ref[...], g_ref[...], b_ref[...])
    o_ref[...] = _silu(u).astype(BF16)


def _cf_mix(x2d, mods, layer, tiles_per_batch, w_in, dw_w, dw_b, ln_g, ln_b):
    rows = x2d.shape[0]
    row_spec = pl.BlockSpec((TM, D_MODEL), lambda t: (t, 0))
    base = layer * N_MOD
    return pl.pallas_call(
        functools.partial(_cf_kernel, tiles_per_batch=tiles_per_batch),
        grid=(rows // TM,),
        in_specs=[
            row_spec, _mod_spec(base + 0, tiles_per_batch), _mod_spec(base + 1, tiles_per_batch),
            _const_spec((D_MODEL, 2 * D_MODEL)), _const_spec((CF_KERNEL, D_MODEL)),
            _const_spec((1, D_MODEL)), _const_spec((1, D_MODEL)), _const_spec((1, D_MODEL)),
        ],
        out_specs=row_spec,
        out_shape=jax.ShapeDtypeStruct((rows, D_MODEL), BF16),
        scratch_shapes=[pltpu.VMEM((CF_HALO + TM, D_MODEL), F32)],
        compiler_params=pltpu.CompilerParams(dimension_semantics=("arbitrary",),
                                             vmem_limit_bytes=VMEM_LIMIT),
        name="cf_mix",
    )(x2d, mods, mods, w_in, dw_w, dw_b, ln_g, ln_b)


def kernel(x, c, ada_w, ada_b, ln_g, ln_b, dn_w_in, dn_conv_w, dn_a_log, dn_dt_bias, dn_norm_w,
           dn_w_out, cf_w_in, cf_dw_w, cf_dw_b, cf_ln_g, cf_ln_b, cf_w_out, ff_w1, ff_w2):
    batch, seq, d = x.shape
    assert d == D_MODEL and seq % TM == 0 and TM % CHUNK == 0
    tiles_per_batch = seq // TM
    n_qkv = 3 * D_MODEL
    n_main = 4 * D_MODEL

    mods = _ada_mods(c, ada_w, ada_b)
    x2d = x.reshape(batch * seq, D_MODEL)
    for i in range(DEPTH):
        j = i // 2
        if i % 2 == 0:
            w_in = dn_w_in[j]
            w_main = w_in[:, :n_main].astype(BF16)
            w_ba = jnp.pad(w_in[:, n_main:], ((0, 0), (0, LANES - 2 * HEADS))).astype(BF16)
            gate_pad = (HEADS, LANES - 2 * HEADS)
            alog_vec = jnp.pad(dn_a_log[j], gate_pad).reshape(1, LANES)
            dtb_vec = jnp.pad(dn_dt_bias[j], gate_pad).reshape(1, LANES)
            q, k, v, z, bg = _dn_in(x2d, mods, i, tiles_per_batch, w_main, w_ba,
                                    dn_conv_w[j][:, :n_qkv], alog_vec, dtb_vec)
            u = _dn_core(q, k, v, z, bg, dn_norm_w[j].reshape(1, HEAD_DIM), batch, tiles_per_batch)
            w_out = dn_w_out[j]
        else:
            u = _cf_mix(x2d, mods, i, tiles_per_batch, cf_w_in[j].astype(BF16), cf_dw_w[j],
                        cf_dw_b[j].reshape(1, D_MODEL), cf_ln_g[j].reshape(1, D_MODEL),
                        cf_ln_b[j].reshape(1, D_MODEL))
            w_out = cf_w_out[j]
        x2d = _post(x2d, u, mods, i, tiles_per_batch, w_out.astype(BF16), ln_g[i], ln_b[i],
                    ff_w1[i].astype(BF16), ff_w2[i].astype(BF16))
    return x2d.reshape(batch, seq, D_MODEL)
```

```python
import functools

import jax
import jax.numpy as jnp
from jax import lax
from jax.experimental import pallas as pl
from jax.experimental.pallas import tpu as pltpu

F32 = jnp.float32
BF16 = jnp.bfloat16

D_MODEL = 1024
DEPTH = 4
N_MOD = 6
HEADS = 8
HEAD_DIM = 128
DN_CONV = 4
CHUNK = 64
BLOCK16 = 16
CF_KERNEL = 31
FF_DIM = 4 * D_MODEL
ALPHA = (2.0 * DEPTH) ** 0.25
LN_EPS = 1e-5
RMS_EPS = 1e-6
L2_EPS = 1e-6

LANES = 128
SUBLANES = 8
TM = 512
FF_TILE = 1024
CF_HALO = 32
CF_PIECE = 128
DN_HALO = 8
PASS1_CHUNKS = 2
VMEM_LIMIT = 56 * 1024 * 1024


def _dot(a, b):
    return jnp.dot(a, b, preferred_element_type=F32)


def _dot_nt(a, b):
    return lax.dot_general(a, b, (((1,), (1,)), ((), ())), preferred_element_type=F32)


def _dot_tn(a, b):
    return lax.dot_general(a, b, (((0,), (0,)), ((), ())), preferred_element_type=F32)


def _bdot(a, b):
    return _dot(a.astype(BF16), b.astype(BF16))


def _split3(x):
    x1 = x.astype(BF16)
    r1 = x - x1.astype(F32)
    x2 = r1.astype(BF16)
    x3 = (r1 - x2.astype(F32)).astype(BF16)
    return x1, x2, x3


def _dot_exact_lhs(a_bf16, x):
    x1, x2, x3 = _split3(x)
    return _dot(a_bf16, x3) + _dot(a_bf16, x2) + _dot(a_bf16, x1)


def _layer_norm(r, g, b):
    mu = jnp.mean(r, axis=-1, keepdims=True)
    d = r - mu
    var = jnp.mean(d * d, axis=-1, keepdims=True)
    return d * lax.rsqrt(var + LN_EPS) * g + b


def _silu(x):
    return x * jax.nn.sigmoid(x)


def _softplus(x):
    return jnp.maximum(x, 0.0) + jnp.log1p(jnp.exp(-jnp.abs(x)))


def _const_spec(shape):
    nd = len(shape)
    return pl.BlockSpec(shape, lambda *_: (0,) * nd, pipeline_mode=pl.Buffered(1))


def _mod_spec(row, tiles_per_batch):
    return pl.BlockSpec((1, 1, 1, D_MODEL), lambda t: (row, t // tiles_per_batch, 0, 0))


def _ada_kernel(c_ref, w_ref, b_ref, o_ref):
    cond = _silu(c_ref[...])
    o_ref[0] = _dot(cond, w_ref[0]) + b_ref[0]


def _ada_mods(c, ada_w, ada_b):
    batch = c.shape[0]
    out = pl.pallas_call(
        _ada_kernel,
        grid=(DEPTH, N_MOD),
        in_specs=[
            pl.BlockSpec((batch, D_MODEL), lambda i, j: (0, 0)),
            pl.BlockSpec((1, D_MODEL, D_MODEL), lambda i, j: (i, 0, j)),
            pl.BlockSpec((1, 1, D_MODEL), lambda i, j: (i, 0, j)),
        ],
        out_specs=pl.BlockSpec((1, batch, D_MODEL), lambda i, j: (i * N_MOD + j, 0, 0)),
        out_shape=jax.ShapeDtypeStruct((DEPTH * N_MOD, batch, D_MODEL), F32),
        compiler_params=pltpu.CompilerParams(dimension_semantics=("arbitrary", "arbitrary")),
        name="ada_mods",
    )(c, ada_w, ada_b.reshape(DEPTH, 1, N_MOD * D_MODEL))
    return out.reshape(DEPTH * N_MOD, batch, 1, D_MODEL)


def _post_kernel(x_ref, u_ref, gt1_ref, sh2_ref, sc2_ref, gt2_ref, wo_ref, lng_ref, lnb_ref,
                 w1_ref, w2_ref, o_ref):
    x = x_ref[...]
    y = _dot(u_ref[...], wo_ref[...])
    x1 = _layer_norm(ALPHA * x + (1.0 + gt1_ref[0, 0]) * y, lng_ref[0:1, :], lnb_ref[0:1, :])
    h = (x1 * (1.0 + sc2_ref[0, 0]) + sh2_ref[0, 0]).astype(BF16)
    acc = None
    for c in range(FF_DIM // FF_TILE):
        cols = slice(c * FF_TILE, (c + 1) * FF_TILE)
        a = jnp.maximum(_dot(h, w1_ref[:, cols]), 0.0)
        p = _dot((a * a).astype(BF16), w2_ref[cols, :])
        acc = p if acc is None else acc + p
    o_ref[...] = _layer_norm(ALPHA * x1 + (1.0 + gt2_ref[0, 0]) * acc, lng_ref[1:2, :], lnb_ref[1:2, :])


def _post(x2d, u2d, mods, layer, tiles_per_batch, w_out, ln_g, ln_b, w1, w2):
    rows = x2d.shape[0]
    row_spec = pl.BlockSpec((TM, D_MODEL), lambda t: (t, 0))
    base = layer * N_MOD
    return pl.pallas_call(
        _post_kernel,
        grid=(rows // TM,),
        in_specs=[
            row_spec, row_spec,
            _mod_spec(base + 2, tiles_per_batch), _mod_spec(base + 3, tiles_per_batch),
            _mod_spec(base + 4, tiles_per_batch), _mod_spec(base + 5, tiles_per_batch),
            _const_spec((D_MODEL, D_MODEL)), _const_spec((2, D_MODEL)), _const_spec((2, D_MODEL)),
            _const_spec((D_MODEL, FF_DIM)), _const_spec((FF_DIM, D_MODEL)),
        ],
        out_specs=row_spec,
        out_shape=jax.ShapeDtypeStruct((rows, D_MODEL), F32),
        compiler_params=pltpu.CompilerParams(dimension_semantics=("arbitrary",),
                                             vmem_limit_bytes=VMEM_LIMIT),
        name="post_mlp",
    )(x2d, u2d, mods, mods, mods, mods, w_out, ln_g, ln_b, w1, w2)


def _dn_in_kernel(x_ref, sh_ref, sc_ref, w_ref, wba_ref, cw_ref, alog_ref, dtb_ref,
                  q_ref, k_ref, v_ref, z_ref, bg_ref, ext_ref, *, tiles_per_batch):
    t = pl.program_id(0)

    @pl.when(t % tiles_per_batch == 0)
    def _():
        ext_ref[0:DN_HALO, :] = jnp.zeros((DN_HALO, 3 * D_MODEL), F32)

    h = (x_ref[...] * (1.0 + sc_ref[0, 0]) + sh_ref[0, 0]).astype(BF16)

    for grp, out_ref in enumerate((q_ref, k_ref, v_ref)):
        cols = slice(grp * D_MODEL, (grp + 1) * D_MODEL)
        ext_ref[DN_HALO:DN_HALO + TM, cols] = _dot(h, w_ref[:, cols])
        ext = ext_ref[:, cols]
        acc = None
        for j in range(DN_CONV):
            off = DN_HALO - (DN_CONV - 1) + j
            res = off % SUBLANES
            shifted = ext if res == 0 else pltpu.roll(ext, DN_HALO + TM - res, axis=0)
            term = cw_ref[j:j + 1, cols] * shifted[off - res:off - res + TM, :]
            acc = term if acc is None else acc + term
        ext_ref[0:DN_HALO, cols] = ext_ref[TM:TM + DN_HALO, cols]
        y = _silu(acc)
        if out_ref is v_ref:
            out_ref[...] = y
        else:
            for hd in range(HEADS):
                lanes = slice(hd * HEAD_DIM, (hd + 1) * HEAD_DIM)
                ys = y[:, lanes]
                ss = jnp.sum(ys * ys, axis=-1, keepdims=True)
                out_ref[:, lanes] = ys * lax.rsqrt(ss + L2_EPS)

    z_ref[...] = _dot(h, w_ref[:, 3 * D_MODEL:4 * D_MODEL])

    ba = _dot(h, wba_ref[...])
    beta = jax.nn.sigmoid(ba)
    g = -jnp.exp(alog_ref[...]) * _softplus(ba + dtb_ref[...])
    lane = lax.broadcasted_iota(jnp.int32, ba.shape, 1)
    bg_ref[...] = jnp.where(lane < HEADS, beta, g)


def _dn_in(x2d, mods, layer, tiles_per_batch, w_main, w_ba, conv_w, alog_vec, dtb_vec):
    rows = x2d.shape[0]
    row_spec = pl.BlockSpec((TM, D_MODEL), lambda t: (t, 0))
    base = layer * N_MOD
    out_sds = jax.ShapeDtypeStruct((rows, D_MODEL), F32)
    return pl.pallas_call(
        functools.partial(_dn_in_kernel, tiles_per_batch=tiles_per_batch),
        grid=(rows // TM,),
        in_specs=[
            row_spec, _mod_spec(base + 0, tiles_per_batch), _mod_spec(base + 1, tiles_per_batch),
            _const_spec((D_MODEL, 4 * D_MODEL)), _const_spec((D_MODEL, LANES)),
            _const_spec((DN_CONV, 3 * D_MODEL)), _const_spec((1, LANES)), _const_spec((1, LANES)),
        ],
        out_specs=[row_spec, row_spec, row_spec, row_spec, pl.BlockSpec((TM, LANES), lambda t: (t, 0))],
        out_shape=[out_sds, out_sds, out_sds, out_sds, jax.ShapeDtypeStruct((rows, LANES), F32)],
        scratch_shapes=[pltpu.VMEM((DN_HALO + TM, 3 * D_MODEL), F32)],
        compiler_params=pltpu.CompilerParams(dimension_semantics=("arbitrary",),
                                             vmem_limit_bytes=VMEM_LIMIT),
        name="dn_in",
    )(x2d, mods, mods, w_main, w_ba, conv_w, alog_vec, dtb_vec)


def _inverse_minus_eye_many(a_list):
    ri = lax.broadcasted_iota(jnp.int32, (CHUNK, CHUNK), 0) // BLOCK16
    ci = lax.broadcasted_iota(jnp.int32, (CHUNK, CHUNK), 1) // BLOCK16
    on_diag = ri == ci
    b16 = lambda xs: [x.astype(BF16) for x in xs]
    mm = lambda xs, ys: [_dot(x, y) for x, y in zip(xs, ys)]

    d = [jnp.where(on_diag, a, 0.0) for a in a_list]
    low = [a - x for a, x in zip(a_list, d)]
    d_16 = b16(d)
    d2 = mm(d_16, d_16)
    d2_16 = b16(d2)
    d4 = mm(d2_16, d2_16)
    dd2 = mm(d_16, d2_16)
    p = [x2 - x - y for x, x2, y in zip(d, d2, dd2)]
    d4_16 = b16(d4)
    d8 = mm(d4_16, d4_16)
    pd4 = mm(b16(p), d4_16)
    p = [x + y + z for x, y, z in zip(p, d4, pd4)]
    pd8 = mm(b16(p), b16(d8))
    p = [x + y + z for x, y, z in zip(p, d8, pd8)]
    p_16 = b16(p)
    pl_ = mm(p_16, b16(low))
    m = [x + y for x, y in zip(low, pl_)]
    m_16 = b16(m)
    m2 = mm(m_16, m_16)
    mm2 = mm(m_16, b16(m2))
    q = [x2 - x - y for x, x2, y in zip(m, m2, mm2)]
    qp = mm(b16(q), p_16)
    return [x + y + z for x, y, z in zip(q, p, qp)]


def _dn_core_kernel(q_ref, k_ref, v_ref, z_ref, bg_ref, nw_ref, o_ref,
                    state_ref, gam_ref, egam_ref, edec_ref, egl_ref,
                    u_ref, w_ref, qd_ref, kdt_ref, a_ref):
    @pl.when(pl.program_id(1) == 0)
    def _():
        state_ref[...] = jnp.zeros(state_ref.shape, F32)

    ri = lax.broadcasted_iota(jnp.int32, (TM, TM), 0)
    ci = lax.broadcasted_iota(jnp.int32, (TM, TM), 1)
    same_chunk = (ri // CHUNK) == (ci // CHUNK)
    cum_mat = jnp.where(same_chunk & (ci <= ri), 1.0, 0.0).astype(BF16)
    tot_mat = jnp.where(same_chunk, 1.0, 0.0).astype(BF16)
    bg = bg_ref[...]
    gam = _dot_exact_lhs(cum_mat, bg)
    gtot = _dot_exact_lhs(tot_mat, bg)
    gam_ref[...] = gam
    egam_ref[...] = jnp.exp(gam)
    edec_ref[...] = jnp.exp(gtot - gam)
    egl_ref[...] = jnp.exp(gtot)

    r_idx = lax.broadcasted_iota(jnp.int32, (CHUNK, CHUNK), 0)
    c_idx = lax.broadcasted_iota(jnp.int32, (CHUNK, CHUNK), 1)
    causal = c_idx <= r_idx
    strict = c_idx < r_idx
    scale = HEAD_DIM ** -0.5
    wide = (CHUNK, HEAD_DIM)

    def independent_part(grp, carry):
        probs = []
        for ci_ in range(PASS1_CHUNKS):
            cidx = grp * PASS1_CHUNKS + ci_
            rows = pl.ds(pl.multiple_of(cidx * CHUNK, CHUNK), CHUNK)
            gam_c = gam_ref[rows, :]
            gam_t = gam_c.T
            bg_c = bg_ref[rows, :]
            egam_c = egam_ref[rows, :]
            edec_c = edec_ref[rows, :]
            for hd in range(HEADS):
                lanes = slice(hd * HEAD_DIM, (hd + 1) * HEAD_DIM)
                gl = HEADS + hd
                beta = jnp.broadcast_to(bg_c[:, hd:hd + 1], wide)
                eg = jnp.broadcast_to(egam_c[:, gl:gl + 1], wide)
                ed = jnp.broadcast_to(edec_c[:, gl:gl + 1], wide)
                gcol = jnp.broadcast_to(gam_c[:, gl:gl + 1], wide)[:, :CHUNK]
                decay = jnp.exp(jnp.where(causal, gcol - gam_t[gl:gl + 1, :], -jnp.inf))
                probs.append(dict(cidx=cidx, rows=rows, hd=hd, lanes=lanes, beta=beta, eg=eg, ed=ed,
                                  decay=decay, q=q_ref[rows, lanes], k=k_ref[rows, lanes],
                                  v=v_ref[rows, lanes]))
        for p in probs:
            p["k16"] = p["k"].astype(BF16)
            p["kq16"] = jnp.concatenate([p["k16"], (p["q"] * scale).astype(BF16)], axis=0)
        kkqk = [_dot_nt(p["kq16"], p["k16"]) for p in probs]
        a_list = []
        for p, s in zip(probs, kkqk):
            a_list.append(jnp.where(strict, s[:CHUNK] * p["beta"][:, :CHUNK] * p["decay"], 0.0))
            a_ref[p["hd"], p["rows"], :] = (s[CHUNK:] * p["decay"]).astype(BF16)
        tinv = _inverse_minus_eye_many(a_list)
        for p in probs:
            p["vb"] = p["v"] * p["beta"]
            p["kbg"] = p["k"] * p["beta"] * p["eg"]
        uw = [_dot(t.astype(BF16), jnp.concatenate([p["vb"], p["kbg"]], axis=1).astype(BF16))
              for p, t in zip(probs, tinv)]
        for p, r in zip(probs, uw):
            rows, lanes = p["rows"], p["lanes"]
            u_ref[rows, lanes] = p["vb"] + r[:, :HEAD_DIM]
            w_ref[rows, lanes] = (p["kbg"] + r[:, HEAD_DIM:]).astype(BF16)
            qd_ref[rows, lanes] = (p["q"] * (p["eg"] * scale)).astype(BF16)
            kdt_ref[p["hd"], p["cidx"]] = (p["k"] * p["ed"]).T.astype(BF16)
        return carry

    lax.fori_loop(0, TM // CHUNK // PASS1_CHUNKS, independent_part, 0)

    nw = nw_ref[...]

    def recurrent_part(c, carry):
        rows = pl.ds(pl.multiple_of(c * CHUNK, CHUNK), CHUNK)
        egl_c = egl_ref[rows, :]
        heads = range(HEADS)
        lanes = [slice(hd * HEAD_DIM, (hd + 1) * HEAD_DIM) for hd in heads]
        state = [state_ref[hd] for hd in heads]
        wq = [jnp.concatenate([w_ref[rows, lanes[hd]], qd_ref[rows, lanes[hd]]], axis=0) for hd in heads]
        r = [_dot(wq[hd], state[hd].astype(BF16)) for hd in heads]
        vn16 = [(u_ref[rows, lanes[hd]] - r[hd][:CHUNK]).astype(BF16) for hd in heads]
        av = [_dot(a_ref[hd, rows, :], vn16[hd]) for hd in heads]
        kv = [_dot(kdt_ref[hd, c], vn16[hd]) for hd in heads]
        for hd in heads:
            state_ref[hd] = state[hd] * egl_c[0:1, HEADS + hd:HEADS + hd + 1] + kv[hd]
            o = r[hd][CHUNK:] + av[hd]
            on = o * lax.rsqrt(jnp.mean(o * o, axis=-1, keepdims=True) + RMS_EPS) * nw
            o_ref[rows, lanes[hd]] = (on * _silu(z_ref[rows, lanes[hd]])).astype(BF16)
        return carry

    lax.fori_loop(0, TM // CHUNK, recurrent_part, 0)


def _dn_core(q, k, v, z, bg, norm_w, batch, tiles_per_batch):
    rows = q.shape[0]
    row_spec = pl.BlockSpec((TM, D_MODEL), lambda b, s: (b * tiles_per_batch + s, 0))
    small = pltpu.VMEM((TM, LANES), F32)
    return pl.pallas_call(
        _dn_core_kernel,
        grid=(batch, tiles_per_batch),
        in_specs=[row_spec, row_spec, row_spec, row_spec,
                  pl.BlockSpec((TM, LANES), lambda b, s: (b * tiles_per_batch + s, 0)),
                  pl.BlockSpec((1, HEAD_DIM), lambda b, s: (0, 0))],
        out_specs=row_spec,
        out_shape=jax.ShapeDtypeStruct((rows, D_MODEL), BF16),
        scratch_shapes=[
            pltpu.VMEM((HEADS, HEAD_DIM, HEAD_DIM), F32),
            small, small, small, small,
            pltpu.VMEM((TM, D_MODEL), F32),
            pltpu.VMEM((TM, D_MODEL), BF16),
            pltpu.VMEM((TM, D_MODEL), BF16),
            pltpu.VMEM((HEADS, TM // CHUNK, HEAD_DIM, CHUNK), BF16),
            pltpu.VMEM((HEADS, TM, CHUNK), BF16),
        ],
        compiler_params=pltpu.CompilerParams(dimension_semantics=("arbitrary", "arbitrary"),
                                             vmem_limit_bytes=VMEM_LIMIT),
        name="dn_core",
    )(q, k, v, z, bg, norm_w)


def _zero_after(v):
    bits = lax.bitcast_convert_type(v, jnp.uint32)
    z = lax.shift_right_logical(lax.shift_right_logical(bits, jnp.uint32(16)), jnp.uint32(16))
    return z[0, 0].astype(jnp.int32)


def _cf_conv_piece(ext_ref, dw_ref, res, piece, not_before):
    n = CF_PIECE + CF_HALO + SUBLANES
    start = pl.multiple_of(piece * CF_PIECE + _zero_after(not_before), SUBLANES)
    window = ext_ref[pl.ds(start, n), :]
    shifted = window if res == 0 else pltpu.roll(window, n - res, axis=0)
    acc = None
    for j in range(CF_KERNEL):
        off = CF_HALO - (CF_KERNEL - 1) + j
        if off % SUBLANES != res:
            continue
        term = dw_ref[j:j + 1, :] * shifted[off - res:off - res + CF_PIECE, :]
        acc = term if acc is None else acc + term
    return acc


def _cf_layer_kernel(x_ref, sh1_ref, sc1_ref, gt1_ref, sh2_ref, sc2_ref, gt2_ref,
                     win_ref, dw_ref, dwb_ref, cg_ref, cb_ref, wo_ref, lng_ref, lnb_ref, w1_ref, w2_ref,
                     o_ref, ext_ref, xs_ref, us_ref, *, n_tiles, tiles_per_batch):
    s = pl.program_id(0)
    t_mix = jnp.minimum(s, n_tiles - 1)

    @pl.when(s == 0)
    def _():
        xs_ref[...] = jnp.zeros(xs_ref.shape, F32)
        us_ref[...] = jnp.zeros(us_ref.shape, BF16)
        ext_ref[CF_HALO + TM:CF_HALO + TM + SUBLANES, :] = jnp.zeros((SUBLANES, D_MODEL), F32)

    @pl.when(t_mix % tiles_per_batch == 0)
    def _():
        ext_ref[0:CF_HALO, :] = jnp.zeros((CF_HALO, D_MODEL), F32)

    x = x_ref[...]
    h = (x * (1.0 + sc1_ref[0, 0]) + sh1_ref[0, 0]).astype(BF16)
    val = _dot(h, win_ref[:, 0:D_MODEL])
    gate = _dot(h, win_ref[:, D_MODEL:2 * D_MODEL])
    ext_ref[CF_HALO:CF_HALO + TM, :] = val * jax.nn.sigmoid(gate)

    x_old = xs_ref[...]
    y = _dot(us_ref[...], wo_ref[...])
    x1 = _layer_norm(ALPHA * x_old + (1.0 + gt1_ref[0, 0]) * y, lng_ref[0:1, :], lnb_ref[0:1, :])
    h2 = (x1 * (1.0 + sc2_ref[0, 0]) + sh2_ref[0, 0]).astype(BF16)

    n_chunks = FF_DIM // FF_TILE
    n_pieces = TM // CF_PIECE
    assert 2 * n_chunks == SUBLANES
    conv = [None] * n_pieces

    def conv_group_after(res, result):
        for piece in range(n_pieces):
            r0 = piece * CF_PIECE
            part = _cf_conv_piece(ext_ref, dw_ref, res, piece, result[r0:r0 + 1, 0:1])
            conv[piece] = part if conv[piece] is None else conv[piece] + part

    conv_group_after(0, y)
    acc = None
    for c in range(n_chunks):
        cols = slice(c * FF_TILE, (c + 1) * FF_TILE)
        d1 = _dot(h2, w1_ref[:, cols])
        conv_group_after(2 * c + 1, d1)
        a = jnp.maximum(d1, 0.0)
        p = _dot((a * a).astype(BF16), w2_ref[cols, :])
        if 2 * c + 2 < SUBLANES:
            conv_group_after(2 * c + 2, p)
        acc = p if acc is None else acc + p
    o_ref[...] = _layer_norm(ALPHA * x1 + (1.0 + gt2_ref[0, 0]) * acc, lng_ref[1:2, :], lnb_ref[1:2, :])

    ext_ref[0:CF_HALO, :] = ext_ref[TM:TM + CF_HALO, :]
    u = _layer_norm(jnp.concatenate(conv, axis=0) + dwb_ref[...], cg_ref[...], cb_ref[...])
    us_ref[...] = _silu(u).astype(BF16)
    xs_ref[...] = x


def _cf_layer(x2d, mods, layer, tiles_per_batch, w_in, dw_w, dw_b, cf_g, cf_b, w_out, ln_g, ln_b, w1, w2):
    rows = x2d.shape[0]
    n_tiles = rows // TM
    base = layer * N_MOD
    mix_tile = lambda s: jnp.minimum(s, n_tiles - 1)
    mlp_tile = lambda s: jnp.maximum(s - 1, 0)

    def mod_spec(row, tile_of_step):
        return pl.BlockSpec((1, 1, 1, D_MODEL), lambda s: (row, tile_of_step(s) // tiles_per_batch, 0, 0))

    return pl.pallas_call(
        functools.partial(_cf_layer_kernel, n_tiles=n_tiles, tiles_per_batch=tiles_per_batch),
        grid=(n_tiles + 1,),
        in_specs=[
            pl.BlockSpec((TM, D_MODEL), lambda s: (mix_tile(s), 0)),
            mod_spec(base + 0, mix_tile), mod_spec(base + 1, mix_tile),
            mod_spec(base + 2, mlp_tile), mod_spec(base + 3, mlp_tile),
            mod_spec(base + 4, mlp_tile), mod_spec(base + 5, mlp_tile),
            _const_spec((D_MODEL, 2 * D_MODEL)), _const_spec((CF_KERNEL, D_MODEL)),
            _const_spec((1, D_MODEL)), _const_spec((1, D_MODEL)), _const_spec((1, D_MODEL)),
            _const_spec((D_MODEL, D_MODEL)), _const_spec((2, D_MODEL)), _const_spec((2, D_MODEL)),
            _const_spec((D_MODEL, FF_DIM)), _const_spec((FF_DIM, D_MODEL)),
        ],
        out_specs=pl.BlockSpec((TM, D_MODEL), lambda s: (mlp_tile(s), 0)),
        out_shape=jax.ShapeDtypeStruct((rows, D_MODEL), F32),
        scratch_shapes=[pltpu.VMEM((CF_HALO + TM + SUBLANES, D_MODEL), F32),
                        pltpu.VMEM((TM, D_MODEL), F32),
                        pltpu.VMEM((TM, D_MODEL), BF16)],
        compiler_params=pltpu.CompilerParams(dimension_semantics=("arbitrary",),
                                             vmem_limit_bytes=VMEM_LIMIT),
        name="cf_layer",
    )(x2d, mods, mods, mods, mods, mods, mods, w_in, dw_w, dw_b, cf_g, cf_b, w_out, ln_g, ln_b, w1, w2)


def kernel(x, c, ada_w, ada_b, ln_g, ln_b, dn_w_in, dn_conv_w, dn_a_log, dn_dt_bias, dn_norm_w,
           dn_w_out, cf_w_in, cf_dw_w, cf_dw_b, cf_ln_g, cf_ln_b, cf_w_out, ff_w1, ff_w2):
    batch, seq, d = x.shape
    assert d == D_MODEL and seq % TM == 0 and TM % CHUNK == 0
    tiles_per_batch = seq // TM
    n_qkv = 3 * D_MODEL
    n_main = 4 * D_MODEL

    mods = _ada_mods(c, ada_w, ada_b)
    x2d = x.reshape(batch * seq, D_MODEL)
    for i in range(DEPTH):
        j = i // 2
        w1, w2 = ff_w1[i].astype(BF16), ff_w2[i].astype(BF16)
        if i % 2 == 0:
            w_in = dn_w_in[j]
            w_main = w_in[:, :n_main].astype(BF16)
            w_ba = jnp.pad(w_in[:, n_main:], ((0, 0), (0, LANES - 2 * HEADS))).astype(BF16)
            gate_pad = (HEADS, LANES - 2 * HEADS)
            alog_vec = jnp.pad(dn_a_log[j], gate_pad).reshape(1, LANES)
            dtb_vec = jnp.pad(dn_dt_bias[j], gate_pad).reshape(1, LANES)
            q, k, v, z, bg = _dn_in(x2d, mods, i, tiles_per_batch, w_main, w_ba,
                                    dn_conv_w[j][:, :n_qkv], alog_vec, dtb_vec)
            u = _dn_core(q, k, v, z, bg, dn_norm_w[j].reshape(1, HEAD_DIM), batch, tiles_per_batch)
            x2d = _post(x2d, u, mods, i, tiles_per_batch, dn_w_out[j].astype(BF16), ln_g[i], ln_b[i],
                        w1, w2)
        else:
            x2d = _cf_layer(x2d, mods, i, tiles_per_batch, cf_w_in[j].astype(BF16), cf_dw_w[j],
                            cf_dw_b[j].reshape(1, D_MODEL), cf_ln_g[j].reshape(1, D_MODEL),
                            cf_ln_b[j].reshape(1, D_MODEL), cf_w_out[j].astype(BF16), ln_g[i], ln_b[i],
                            w1, w2)
    return x2d.reshape(batch, seq, D_MODEL)
```

```python
import functools

import jax
import jax.numpy as jnp
from jax import lax
from jax.experimental import pallas as pl
from jax.experimental.pallas import tpu as pltpu

F32 = jnp.float32
BF16 = jnp.bfloat16

D_MODEL = 1024
DEPTH = 4
N_MOD = 6
HEADS = 8
HEAD_DIM = 128
DN_CONV = 4
CHUNK = 64
BLOCK16 = 16
CF_KERNEL = 31
FF_DIM = 4 * D_MODEL
ALPHA = (2.0 * DEPTH) ** 0.25
LN_EPS = 1e-5
RMS_EPS = 1e-6
L2_EPS = 1e-6

LANES = 128
SUBLANES = 8
TM = 512
FF_TILE = 1024
CF_HALO = 32
CF_PIECE = 128
DN_HALO = 8
PASS1_CHUNKS = 4
VMEM_LIMIT = 56 * 1024 * 1024


def _dot(a, b):
    return jnp.dot(a, b, preferred_element_type=F32)


def _dot_nt(a, b):
    return lax.dot_general(a, b, (((1,), (1,)), ((), ())), preferred_element_type=F32)


def _dot_tn(a, b):
    return lax.dot_general(a, b, (((0,), (0,)), ((), ())), preferred_element_type=F32)


def _bdot(a, b):
    return _dot(a.astype(BF16), b.astype(BF16))


def _split3(x):
    x1 = x.astype(BF16)
    r1 = x - x1.astype(F32)
    x2 = r1.astype(BF16)
    x3 = (r1 - x2.astype(F32)).astype(BF16)
    return x1, x2, x3


def _dot_exact_lhs(a_bf16, x):
    x1, x2, x3 = _split3(x)
    return _dot(a_bf16, x3) + _dot(a_bf16, x2) + _dot(a_bf16, x1)


def _layer_norm(r, g, b):
    mu = jnp.mean(r, axis=-1, keepdims=True)
    d = r - mu
    var = jnp.mean(d * d, axis=-1, keepdims=True)
    return d * lax.rsqrt(var + LN_EPS) * g + b


def _silu(x):
    return x * jax.nn.sigmoid(x)


def _softplus(x):
    return jnp.maximum(x, 0.0) + jnp.log1p(jnp.exp(-jnp.abs(x)))


def _const_spec(shape):
    nd = len(shape)
    return pl.BlockSpec(shape, lambda *_: (0,) * nd, pipeline_mode=pl.Buffered(1))


def _mod_spec(row, tiles_per_batch):
    return pl.BlockSpec((1, 1, 1, D_MODEL), lambda t: (row, t // tiles_per_batch, 0, 0))


def _ada_kernel(c_ref, w_ref, b_ref, o_ref):
    cond = _silu(c_ref[...])
    o_ref[0] = _dot(cond, w_ref[0]) + b_ref[0]


def _ada_mods(c, ada_w, ada_b):
    batch = c.shape[0]
    out = pl.pallas_call(
        _ada_kernel,
        grid=(DEPTH, N_MOD),
        in_specs=[
            pl.BlockSpec((batch, D_MODEL), lambda i, j: (0, 0)),
            pl.BlockSpec((1, D_MODEL, D_MODEL), lambda i, j: (i, 0, j)),
            pl.BlockSpec((1, 1, D_MODEL), lambda i, j: (i, 0, j)),
        ],
        out_specs=pl.BlockSpec((1, batch, D_MODEL), lambda i, j: (i * N_MOD + j, 0, 0)),
        out_shape=jax.ShapeDtypeStruct((DEPTH * N_MOD, batch, D_MODEL), F32),
        compiler_params=pltpu.CompilerParams(dimension_semantics=("arbitrary", "arbitrary")),
        name="ada_mods",
    )(c, ada_w, ada_b.reshape(DEPTH, 1, N_MOD * D_MODEL))
    return out.reshape(DEPTH * N_MOD, batch, 1, D_MODEL)


def _post_kernel(x_ref, u_ref, gt1_ref, sh2_ref, sc2_ref, gt2_ref, wo_ref, lng_ref, lnb_ref,
                 w1_ref, w2_ref, o_ref):
    x = x_ref[...]
    y = _dot(u_ref[...], wo_ref[...])
    x1 = _layer_norm(ALPHA * x + (1.0 + gt1_ref[0, 0]) * y, lng_ref[0:1, :], lnb_ref[0:1, :])
    h = (x1 * (1.0 + sc2_ref[0, 0]) + sh2_ref[0, 0]).astype(BF16)
    acc = None
    for c in range(FF_DIM // FF_TILE):
        cols = slice(c * FF_TILE, (c + 1) * FF_TILE)
        a = jnp.maximum(_dot(h, w1_ref[:, cols]), 0.0)
        p = _dot((a * a).astype(BF16), w2_ref[cols, :])
        acc = p if acc is None else acc + p
    o_ref[...] = _layer_norm(ALPHA * x1 + (1.0 + gt2_ref[0, 0]) * acc, lng_ref[1:2, :], lnb_ref[1:2, :])


def _post(x2d, u2d, mods, layer, tiles_per_batch, w_out, ln_g, ln_b, w1, w2):
    rows = x2d.shape[0]
    row_spec = pl.BlockSpec((TM, D_MODEL), lambda t: (t, 0))
    base = layer * N_MOD
    return pl.pallas_call(
        _post_kernel,
        grid=(rows // TM,),
        in_specs=[
            row_spec, row_spec,
            _mod_spec(base + 2, tiles_per_batch), _mod_spec(base + 3, tiles_per_batch),
            _mod_spec(base + 4, tiles_per_batch), _mod_spec(base + 5, tiles_per_batch),
            _const_spec((D_MODEL, D_MODEL)), _const_spec((2, D_MODEL)), _const_spec((2, D_MODEL)),
            _const_spec((D_MODEL, FF_DIM)), _const_spec((FF_DIM, D_MODEL)),
        ],
        out_specs=row_spec,
        out_shape=jax.ShapeDtypeStruct((rows, D_MODEL), F32),
        compiler_params=pltpu.CompilerParams(dimension_semantics=("arbitrary",),
                                             vmem_limit_bytes=VMEM_LIMIT),
        name="post_mlp",
    )(x2d, u2d, mods, mods, mods, mods, w_out, ln_g, ln_b, w1, w2)


def _dn_in_kernel(x_ref, sh_ref, sc_ref, w_ref, wba_ref, cw_ref, alog_ref, dtb_ref,
                  q_ref, k_ref, v_ref, z_ref, bg_ref, ext_ref, *, tiles_per_batch):
    t = pl.program_id(0)

    @pl.when(t % tiles_per_batch == 0)
    def _():
        ext_ref[0:DN_HALO, :] = jnp.zeros((DN_HALO, 3 * D_MODEL), F32)

    h = (x_ref[...] * (1.0 + sc_ref[0, 0]) + sh_ref[0, 0]).astype(BF16)

    for grp, out_ref in enumerate((q_ref, k_ref, v_ref)):
        cols = slice(grp * D_MODEL, (grp + 1) * D_MODEL)
        ext_ref[DN_HALO:DN_HALO + TM, cols] = _dot(h, w_ref[:, cols])
        ext = ext_ref[:, cols]
        acc = None
        for j in range(DN_CONV):
            off = DN_HALO - (DN_CONV - 1) + j
            res = off % SUBLANES
            shifted = ext if res == 0 else pltpu.roll(ext, DN_HALO + TM - res, axis=0)
            term = cw_ref[j:j + 1, cols] * shifted[off - res:off - res + TM, :]
            acc = term if acc is None else acc + term
        ext_ref[0:DN_HALO, cols] = ext_ref[TM:TM + DN_HALO, cols]
        y = _silu(acc)
        if out_ref is v_ref:
            out_ref[...] = y
        else:
            for hd in range(HEADS):
                lanes = slice(hd * HEAD_DIM, (hd + 1) * HEAD_DIM)
                ys = y[:, lanes]
                ss = jnp.sum(ys * ys, axis=-1, keepdims=True)
                out_ref[:, lanes] = ys * lax.rsqrt(ss + L2_EPS)

    z_ref[...] = _dot(h, w_ref[:, 3 * D_MODEL:4 * D_MODEL])

    ba = _dot(h, wba_ref[...])
    beta = jax.nn.sigmoid(ba)
    g = -jnp.exp(alog_ref[...]) * _softplus(ba + dtb_ref[...])
    lane = lax.broadcasted_iota(jnp.int32, ba.shape, 1)
    bg_ref[...] = jnp.where(lane < HEADS, beta, g)


def _dn_in(x2d, mods, layer, tiles_per_batch, w_main, w_ba, conv_w, alog_vec, dtb_vec):
    rows = x2d.shape[0]
    row_spec = pl.BlockSpec((TM, D_MODEL), lambda t: (t, 0))
    base = layer * N_MOD
    out_sds = jax.ShapeDtypeStruct((rows, D_MODEL), F32)
    return pl.pallas_call(
        functools.partial(_dn_in_kernel, tiles_per_batch=tiles_per_batch),
        grid=(rows // TM,),
        in_specs=[
            row_spec, _mod_spec(base + 0, tiles_per_batch), _mod_spec(base + 1, tiles_per_batch),
            _const_spec((D_MODEL, 4 * D_MODEL)), _const_spec((D_MODEL, LANES)),
            _const_spec((DN_CONV, 3 * D_MODEL)), _const_spec((1, LANES)), _const_spec((1, LANES)),
        ],
        out_specs=[row_spec, row_spec, row_spec, row_spec, pl.BlockSpec((TM, LANES), lambda t: (t, 0))],
        out_shape=[out_sds, out_sds, out_sds, out_sds, jax.ShapeDtypeStruct((rows, LANES), F32)],
        scratch_shapes=[pltpu.VMEM((DN_HALO + TM, 3 * D_MODEL), F32)],
        compiler_params=pltpu.CompilerParams(dimension_semantics=("arbitrary",),
                                             vmem_limit_bytes=VMEM_LIMIT),
        name="dn_in",
    )(x2d, mods, mods, w_main, w_ba, conv_w, alog_vec, dtb_vec)


def _inverse_minus_eye_many(a_list):
    ri = lax.broadcasted_iota(jnp.int32, (CHUNK, CHUNK), 0) // BLOCK16
    ci = lax.broadcasted_iota(jnp.int32, (CHUNK, CHUNK), 1) // BLOCK16
    on_diag = ri == ci
    b16 = lambda xs: [x.astype(BF16) for x in xs]
    mm = lambda xs, ys: [_dot(x, y) for x, y in zip(xs, ys)]

    d = [jnp.where(on_diag, a, 0.0) for a in a_list]
    low = [a - x for a, x in zip(a_list, d)]
    d_16 = b16(d)
    d2 = mm(d_16, d_16)
    d2_16 = b16(d2)
    d4 = mm(d2_16, d2_16)
    dd2 = mm(d_16, d2_16)
    p = [x2 - x - y for x, x2, y in zip(d, d2, dd2)]
    d4_16 = b16(d4)
    d8 = mm(d4_16, d4_16)
    pd4 = mm(b16(p), d4_16)
    p = [x + y + z for x, y, z in zip(p, d4, pd4)]
    pd8 = mm(b16(p), b16(d8))
    p = [x + y + z for x, y, z in zip(p, d8, pd8)]
    p_16 = b16(p)
    pl_ = mm(p_16, b16(low))
    m = [x + y for x, y in zip(low, pl_)]
    m_16 = b16(m)
    m2 = mm(m_16, m_16)
    mm2 = mm(m_16, b16(m2))
    q = [x2 - x - y for x, x2, y in zip(m, m2, mm2)]
    qp = mm(b16(q), p_16)
    return [x + y + z for x, y, z in zip(q, p, qp)]


def _dn_core_kernel(q_ref, k_ref, v_ref, z_ref, bg_ref, nw_ref, o_ref,
                    state_ref, gam_ref, egam_ref, edec_ref, egl_ref,
                    u_ref, w_ref, qd_ref, kdt_ref, a_ref):
    @pl.when(pl.program_id(1) == 0)
    def _():
        state_ref[...] = jnp.zeros(state_ref.shape, F32)

    ri = lax.broadcasted_iota(jnp.int32, (TM, TM), 0)
    ci = lax.broadcasted_iota(jnp.int32, (TM, TM), 1)
    same_chunk = (ri // CHUNK) == (ci // CHUNK)
    cum_mat = jnp.where(same_chunk & (ci <= ri), 1.0, 0.0).astype(BF16)
    tot_mat = jnp.where(same_chunk, 1.0, 0.0).astype(BF16)
    bg = bg_ref[...]
    gam = _dot_exact_lhs(cum_mat, bg)
    gtot = _dot_exact_lhs(tot_mat, bg)
    gam_ref[...] = gam
    egam_ref[...] = jnp.exp(gam)
    edec_ref[...] = jnp.exp(gtot - gam)
    egl_ref[...] = jnp.exp(gtot)

    r_idx = lax.broadcasted_iota(jnp.int32, (CHUNK, CHUNK), 0)
    c_idx = lax.broadcasted_iota(jnp.int32, (CHUNK, CHUNK), 1)
    causal = c_idx <= r_idx
    strict = c_idx < r_idx
    scale = HEAD_DIM ** -0.5
    wide = (CHUNK, HEAD_DIM)

    def independent_part(grp, carry):
        probs = []
        for ci_ in range(PASS1_CHUNKS):
            cidx = grp * PASS1_CHUNKS + ci_
            rows = pl.ds(pl.multiple_of(cidx * CHUNK, CHUNK), CHUNK)
            gam_c = gam_ref[rows, :]
            gam_t = gam_c.T
            bg_c = bg_ref[rows, :]
            egam_c = egam_ref[rows, :]
            edec_c = edec_ref[rows, :]
            for hd in range(HEADS):
                lanes = slice(hd * HEAD_DIM, (hd + 1) * HEAD_DIM)
                gl = HEADS + hd
                beta = jnp.broadcast_to(bg_c[:, hd:hd + 1], wide)
                eg = jnp.broadcast_to(egam_c[:, gl:gl + 1], wide)
                ed = jnp.broadcast_to(edec_c[:, gl:gl + 1], wide)
                gcol = jnp.broadcast_to(gam_c[:, gl:gl + 1], wide)[:, :CHUNK]
                decay = jnp.exp(jnp.where(causal, gcol - gam_t[gl:gl + 1, :], -jnp.inf))
                probs.append(dict(cidx=cidx, rows=rows, hd=hd, lanes=lanes, beta=beta, eg=eg, ed=ed,
                                  decay=decay, q=q_ref[rows, lanes], k=k_ref[rows, lanes],
                                  v=v_ref[rows, lanes]))
        for p in probs:
            p["k16"] = p["k"].astype(BF16)
            p["kq16"] = jnp.concatenate([p["k16"], (p["q"] * scale).astype(BF16)], axis=0)
        kkqk = [_dot_nt(p["kq16"], p["k16"]) for p in probs]
        a_list = []
        for p, s in zip(probs, kkqk):
            a_list.append(jnp.where(strict, s[:CHUNK] * p["beta"][:, :CHUNK] * p["decay"], 0.0))
            a_ref[p["hd"], p["rows"], :] = (s[CHUNK:] * p["decay"]).astype(BF16)
        tinv = _inverse_minus_eye_many(a_list)
        for p in probs:
            p["vb"] = p["v"] * p["beta"]
            p["kbg"] = p["k"] * p["beta"] * p["eg"]
        uw = [_dot(t.astype(BF16), jnp.concatenate([p["vb"], p["kbg"]], axis=1).astype(BF16))
              for p, t in zip(probs, tinv)]
        for p, r in zip(probs, uw):
            rows, lanes = p["rows"], p["lanes"]
            u_ref[rows, lanes] = p["vb"] + r[:, :HEAD_DIM]
            w_ref[rows, lanes] = (p["kbg"] + r[:, HEAD_DIM:]).astype(BF16)
            qd_ref[rows, lanes] = (p["q"] * (p["eg"] * scale)).astype(BF16)
            kdt_ref[p["hd"], p["cidx"]] = (p["k"] * p["ed"]).T.astype(BF16)
        return carry

    lax.fori_loop(0, TM // CHUNK // PASS1_CHUNKS, independent_part, 0)

    nw = nw_ref[...]

    def recurrent_part(c, carry):
        rows = pl.ds(pl.multiple_of(c * CHUNK, CHUNK), CHUNK)
        egl_c = egl_ref[rows, :]
        heads = range(HEADS)
        lanes = [slice(hd * HEAD_DIM, (hd + 1) * HEAD_DIM) for hd in heads]
        state = [state_ref[hd] for hd in heads]
        wq = [jnp.concatenate([w_ref[rows, lanes[hd]], qd_ref[rows, lanes[hd]]], axis=0) for hd in heads]
        r = [_dot(wq[hd], state[hd].astype(BF16)) for hd in heads]
        vn16 = [(u_ref[rows, lanes[hd]] - r[hd][:CHUNK]).astype(BF16) for hd in heads]
        av = [_dot(a_ref[hd, rows, :], vn16[hd]) for hd in heads]
        kv = [_dot(kdt_ref[hd, c], vn16[hd]) for hd in heads]
        for hd in heads:
            state_ref[hd] = state[hd] * egl_c[0:1, HEADS + hd:HEADS + hd + 1] + kv[hd]
            o = r[hd][CHUNK:] + av[hd]
            on = o * lax.rsqrt(jnp.mean(o * o, axis=-1, keepdims=True) + RMS_EPS) * nw
            o_ref[rows, lanes[hd]] = (on * _silu(z_ref[rows, lanes[hd]])).astype(BF16)
        return carry

    lax.fori_loop(0, TM // CHUNK, recurrent_part, 0)


def _dn_core(q, k, v, z, bg, norm_w, batch, tiles_per_batch):
    rows = q.shape[0]
    row_spec = pl.BlockSpec((TM, D_MODEL), lambda b, s: (b * tiles_per_batch + s, 0))
    small = pltpu.VMEM((TM, LANES), F32)
    return pl.pallas_call(
        _dn_core_kernel,
        grid=(batch, tiles_per_batch),
        in_specs=[row_spec, row_spec, row_spec, row_spec,
                  pl.BlockSpec((TM, LANES), lambda b, s: (b * tiles_per_batch + s, 0)),
                  pl.BlockSpec((1, HEAD_DIM), lambda b, s: (0, 0))],
        out_specs=row_spec,
        out_shape=jax.ShapeDtypeStruct((rows, D_MODEL), BF16),
        scratch_shapes=[
            pltpu.VMEM((HEADS, HEAD_DIM, HEAD_DIM), F32),
            small, small, small, small,
            pltpu.VMEM((TM, D_MODEL), F32),
            pltpu.VMEM((TM, D_MODEL), BF16),
            pltpu.VMEM((TM, D_MODEL), BF16),
            pltpu.VMEM((HEADS, TM // CHUNK, HEAD_DIM, CHUNK), BF16),
            pltpu.VMEM((HEADS, TM, CHUNK), BF16),
        ],
        compiler_params=pltpu.CompilerParams(dimension_semantics=("arbitrary", "arbitrary"),
                                             vmem_limit_bytes=VMEM_LIMIT),
        name="dn_core",
    )(q, k, v, z, bg, norm_w)


def _zero_after(v):
    bits = lax.bitcast_convert_type(v, jnp.uint32)
    z = lax.shift_right_logical(lax.shift_right_logical(bits, jnp.uint32(16)), jnp.uint32(16))
    return z[0, 0].astype(jnp.int32)


def _cf_conv_piece(ext_ref, dw_ref, res, piece, not_before):
    n = CF_PIECE + CF_HALO + SUBLANES
    start = pl.multiple_of(piece * CF_PIECE + _zero_after(not_before), SUBLANES)
    window = ext_ref[pl.ds(start, n), :]
    shifted = window if res == 0 else pltpu.roll(window, n - res, axis=0)
    acc = None
    for j in range(CF_KERNEL):
        off = CF_HALO - (CF_KERNEL - 1) + j
        if off % SUBLANES != res:
            continue
        term = dw_ref[j:j + 1, :] * shifted[off - res:off - res + CF_PIECE, :]
        acc = term if acc is None else acc + term
    return acc


def _cf_layer_kernel(x_ref, xres_ref, sh1_ref, sc1_ref, gt1_ref, sh2_ref, sc2_ref, gt2_ref,
                     win_ref, dw_ref, dwb_ref, cg_ref, cb_ref, wo_ref, lng_ref, lnb_ref, w1_ref, w2_ref,
                     o_ref, ext_ref, ext_next_ref, u_ref, u_next_ref, *, n_tiles, tiles_per_batch):
    s = pl.program_id(0)
    t_in = jnp.minimum(s, n_tiles - 1)
    first_of_batch = t_in % tiles_per_batch == 0

    @pl.when(s == 0)
    def _():
        ext_ref[...] = jnp.zeros(ext_ref.shape, F32)
        u_ref[...] = jnp.zeros(u_ref.shape, BF16)
        ext_next_ref[CF_HALO + TM:CF_HALO + TM + SUBLANES, :] = jnp.zeros((SUBLANES, D_MODEL), F32)

    @pl.when(first_of_batch)
    def _():
        ext_next_ref[0:CF_HALO, :] = jnp.zeros((CF_HALO, D_MODEL), F32)

    @pl.when(jnp.logical_not(first_of_batch))
    def _():
        ext_next_ref[0:CF_HALO, :] = ext_ref[TM:TM + CF_HALO, :]

    n_chunks = FF_DIM // FF_TILE
    n_pieces = TM // CF_PIECE
    conv = [None] * n_pieces
    todo = [(res, piece) for res in range(SUBLANES) for piece in range(n_pieces)]
    slots_seen = [0]

    def conv_pieces_after(result):
        for blk in range(n_pieces):
            slots_seen[0] += 1
            if slots_seen[0] % 5 == 0 or not todo:
                continue
            res, piece = todo.pop(0)
            r0 = blk * CF_PIECE
            part = _cf_conv_piece(ext_ref, dw_ref, res, piece, result[r0:r0 + 1, 0:1])
            conv[piece] = part if conv[piece] is None else conv[piece] + part

    y = _dot(u_ref[...], wo_ref[...])
    conv_pieces_after(y)

    h = (x_ref[...] * (1.0 + sc1_ref[0, 0]) + sh1_ref[0, 0]).astype(BF16)
    val = _dot(h, win_ref[:, 0:D_MODEL])
    conv_pieces_after(val)
    gate = _dot(h, win_ref[:, D_MODEL:2 * D_MODEL])
    conv_pieces_after(gate)
    ext_next_ref[CF_HALO:CF_HALO + TM, :] = val * jax.nn.sigmoid(gate)

    x1 = _layer_norm(ALPHA * xres_ref[...] + (1.0 + gt1_ref[0, 0]) * y, lng_ref[0:1, :], lnb_ref[0:1, :])
    h2 = (x1 * (1.0 + sc2_ref[0, 0]) + sh2_ref[0, 0]).astype(BF16)
    chunk_cols = [slice(c * FF_TILE, (c + 1) * FF_TILE) for c in range(n_chunks)]
    d1 = _dot(h2, w1_ref[:, chunk_cols[0]])
    conv_pieces_after(d1)
    acc = None
    for c in range(n_chunks):
        d1_next = None
        if c + 1 < n_chunks:
            d1_next = _dot(h2, w1_ref[:, chunk_cols[c + 1]])
            conv_pieces_after(d1_next)
        a = jnp.maximum(d1, 0.0)
        p = _dot((a * a).astype(BF16), w2_ref[chunk_cols[c], :])
        conv_pieces_after(p)
        acc = p if acc is None else acc + p
        d1 = d1_next
    assert not todo
    o_ref[...] = _layer_norm(ALPHA * x1 + (1.0 + gt2_ref[0, 0]) * acc, lng_ref[1:2, :], lnb_ref[1:2, :])

    cu = _layer_norm(jnp.concatenate(conv, axis=0) + dwb_ref[...], cg_ref[...], cb_ref[...])
    u_next_ref[...] = _silu(cu).astype(BF16)

    u_ref[...] = u_next_ref[...]
    ext_ref[...] = ext_next_ref[...]


def _cf_layer(x2d, mods, layer, tiles_per_batch, w_in, dw_w, dw_b, cf_g, cf_b, w_out, ln_g, ln_b, w1, w2):
    rows = x2d.shape[0]
    n_tiles = rows // TM
    base = layer * N_MOD
    in_tile = lambda s: jnp.minimum(s, n_tiles - 1)
    mlp_tile = lambda s: jnp.maximum(s - 2, 0)

    def mod_spec(row, tile_of_step):
        return pl.BlockSpec((1, 1, 1, D_MODEL), lambda s: (row, tile_of_step(s) // tiles_per_batch, 0, 0))

    ext_rows = CF_HALO + TM + SUBLANES
    return pl.pallas_call(
        functools.partial(_cf_layer_kernel, n_tiles=n_tiles, tiles_per_batch=tiles_per_batch),
        grid=(n_tiles + 2,),
        in_specs=[
            pl.BlockSpec((TM, D_MODEL), lambda s: (in_tile(s), 0)),
            pl.BlockSpec((TM, D_MODEL), lambda s: (mlp_tile(s), 0)),
            mod_spec(base + 0, in_tile), mod_spec(base + 1, in_tile),
            mod_spec(base + 2, mlp_tile), mod_spec(base + 3, mlp_tile),
            mod_spec(base + 4, mlp_tile), mod_spec(base + 5, mlp_tile),
            _const_spec((D_MODEL, 2 * D_MODEL)), _const_spec((CF_KERNEL, D_MODEL)),
            _const_spec((1, D_MODEL)), _const_spec((1, D_MODEL)), _const_spec((1, D_MODEL)),
            _const_spec((D_MODEL, D_MODEL)), _const_spec((2, D_MODEL)), _const_spec((2, D_MODEL)),
            _const_spec((D_MODEL, FF_DIM)), _const_spec((FF_DIM, D_MODEL)),
        ],
        out_specs=pl.BlockSpec((TM, D_MODEL), lambda s: (mlp_tile(s), 0)),
        out_shape=jax.ShapeDtypeStruct((rows, D_MODEL), F32),
        scratch_shapes=[pltpu.VMEM((ext_rows, D_MODEL), F32),
                        pltpu.VMEM((ext_rows, D_MODEL), F32),
                        pltpu.VMEM((TM, D_MODEL), BF16),
                        pltpu.VMEM((TM, D_MODEL), BF16)],
        compiler_params=pltpu.CompilerParams(dimension_semantics=("arbitrary",),
                                             vmem_limit_bytes=VMEM_LIMIT),
        name="cf_layer",
    )(x2d, x2d, mods, mods, mods, mods, mods, mods, w_in, dw_w, dw_b, cf_g, cf_b, w_out, ln_g, ln_b, w1, w2)


def kernel(x, c, ada_w, ada_b, ln_g, ln_b, dn_w_in, dn_conv_w, dn_a_log, dn_dt_bias, dn_norm_w,
           dn_w_out, cf_w_in, cf_dw_w, cf_dw_b, cf_ln_g, cf_ln_b, cf_w_out, ff_w1, ff_w2):
    batch, seq, d = x.shape
    assert d == D_MODEL and seq % TM == 0 and TM % CHUNK == 0
    tiles_per_batch = seq // TM
    n_qkv = 3 * D_MODEL
    n_main = 4 * D_MODEL

    mods = _ada_mods(c, ada_w, ada_b)
    x2d = x.reshape(batch * seq, D_MODEL)
    for i in range(DEPTH):
        j = i // 2
        w1, w2 = ff_w1[i].astype(BF16), ff_w2[i].astype(BF16)
        if i % 2 == 0:
            w_in = dn_w_in[j]
            w_main = w_in[:, :n_main].astype(BF16)
            w_ba = jnp.pad(w_in[:, n_main:], ((0, 0), (0, LANES - 2 * HEADS))).astype(BF16)
            gate_pad = (HEADS, LANES - 2 * HEADS)
            alog_vec = jnp.pad(dn_a_log[j], gate_pad).reshape(1, LANES)
            dtb_vec = jnp.pad(dn_dt_bias[j], gate_pad).reshape(1, LANES)
            q, k, v, z, bg = _dn_in(x2d, mods, i, tiles_per_batch, w_main, w_ba,
                                    dn_conv_w[j][:, :n_qkv], alog_vec, dtb_vec)
            u = _dn_core(q, k, v, z, bg, dn_norm_w[j].reshape(1, HEAD_DIM), batch, tiles_per_batch)
            x2d = _post(x2d, u, mods, i, tiles_per_batch, dn_w_out[j].astype(BF16), ln_g[i], ln_b[i],
                        w1, w2)
        else:
            x2d = _cf_layer(x2d, mods, i, tiles_per_batch, cf_w_in[j].astype(BF16), cf_dw_w[j],
                            cf_dw_b[j].reshape(1, D_MODEL), cf_ln_g[j].reshape(1, D_MODEL),
                            cf_ln_b[j].reshape(1, D_MODEL), cf_w_out[j].astype(BF16), ln_g[i], ln_b[i],
                            w1, w2)
    return x2d.reshape(batch, seq, D_MODEL)
```

```python
import functools

import jax
import jax.numpy as jnp
from jax import lax
from jax.experimental import pallas as pl
from jax.experimental.pallas import tpu as pltpu

F32 = jnp.float32
BF16 = jnp.bfloat16

D_MODEL = 1024
DEPTH = 4
N_MOD = 6
HEADS = 8
HEAD_DIM = 128
DN_CONV = 4
CHUNK = 64
BLOCK16 = 16
CF_KERNEL = 31
FF_DIM = 4 * D_MODEL
ALPHA = (2.0 * DEPTH) ** 0.25
LN_EPS = 1e-5
RMS_EPS = 1e-6
L2_EPS = 1e-6

LANES = 128
SUBLANES = 8
TM = 512
FF_TILE = 1024
CF_HALO = 32
CF_PIECE = 128
DN_HALO = 8
PASS1_CHUNKS = 4
VMEM_LIMIT = 56 * 1024 * 1024


def _dot(a, b):
    return jnp.dot(a, b, preferred_element_type=F32)


def _dot_nt(a, b):
    return lax.dot_general(a, b, (((1,), (1,)), ((), ())), preferred_element_type=F32)


def _dot_tn(a, b):
    return lax.dot_general(a, b, (((0,), (0,)), ((), ())), preferred_element_type=F32)


def _bdot(a, b):
    return _dot(a.astype(BF16), b.astype(BF16))


def _split3(x):
    x1 = x.astype(BF16)
    r1 = x - x1.astype(F32)
    x2 = r1.astype(BF16)
    x3 = (r1 - x2.astype(F32)).astype(BF16)
    return x1, x2, x3


def _dot_exact_lhs(a_bf16, x):
    x1, x2, x3 = _split3(x)
    return _dot(a_bf16, x3) + _dot(a_bf16, x2) + _dot(a_bf16, x1)


def _layer_norm(r, g, b):
    mu = jnp.mean(r, axis=-1, keepdims=True)
    d = r - mu
    var = jnp.mean(d * d, axis=-1, keepdims=True)
    return d * lax.rsqrt(var + LN_EPS) * g + b


def _silu(x):
    return x * jax.nn.sigmoid(x)


def _softplus(x):
    return jnp.maximum(x, 0.0) + jnp.log1p(jnp.exp(-jnp.abs(x)))


def _zero_after(v):
    bits = lax.bitcast_convert_type(v, jnp.uint32)
    z = lax.shift_right_logical(lax.shift_right_logical(bits, jnp.uint32(16)), jnp.uint32(16))
    return z[0, 0].astype(jnp.int32)


def _const_spec(shape):
    nd = len(shape)
    return pl.BlockSpec(shape, lambda *_: (0,) * nd, pipeline_mode=pl.Buffered(1))


def _mod_spec(row, tiles_per_batch):
    return pl.BlockSpec((1, 1, 1, D_MODEL), lambda t: (row, t // tiles_per_batch, 0, 0))


def _ada_kernel(c_ref, w_ref, b_ref, o_ref):
    cond = _silu(c_ref[...])
    o_ref[0] = _dot(cond, w_ref[0]) + b_ref[0]


def _ada_mods(c, ada_w, ada_b):
    batch = c.shape[0]
    out = pl.pallas_call(
        _ada_kernel,
        grid=(DEPTH, N_MOD),
        in_specs=[
            pl.BlockSpec((batch, D_MODEL), lambda i, j: (0, 0)),
            pl.BlockSpec((1, D_MODEL, D_MODEL), lambda i, j: (i, 0, j)),
            pl.BlockSpec((1, 1, D_MODEL), lambda i, j: (i, 0, j)),
        ],
        out_specs=pl.BlockSpec((1, batch, D_MODEL), lambda i, j: (i * N_MOD + j, 0, 0)),
        out_shape=jax.ShapeDtypeStruct((DEPTH * N_MOD, batch, D_MODEL), F32),
        compiler_params=pltpu.CompilerParams(dimension_semantics=("arbitrary", "arbitrary")),
        name="ada_mods",
    )(c, ada_w, ada_b.reshape(DEPTH, 1, N_MOD * D_MODEL))
    return out.reshape(DEPTH * N_MOD, batch, 1, D_MODEL)


def _post_kernel(x_ref, u_ref, gt1_ref, sh2_ref, sc2_ref, gt2_ref, wo_ref, lng_ref, lnb_ref,
                 w1_ref, w2_ref, o_ref):
    x = x_ref[...]
    y = _dot(u_ref[...], wo_ref[...])
    x1 = _layer_norm(ALPHA * x + (1.0 + gt1_ref[0, 0]) * y, lng_ref[0:1, :], lnb_ref[0:1, :])
    h = (x1 * (1.0 + sc2_ref[0, 0]) + sh2_ref[0, 0]).astype(BF16)
    n_chunks = FF_DIM // FF_TILE
    chunk_cols = [slice(c * FF_TILE, (c + 1) * FF_TILE) for c in range(n_chunks)]
    d1 = _dot(h, w1_ref[:, chunk_cols[0]])
    acc = None
    for c in range(n_chunks):
        d1_next = _dot(h, w1_ref[:, chunk_cols[c + 1]]) if c + 1 < n_chunks else None
        a = jnp.maximum(d1, 0.0)
        p = _dot((a * a).astype(BF16), w2_ref[chunk_cols[c], :])
        acc = p if acc is None else acc + p
        d1 = d1_next
    o_ref[...] = _layer_norm(ALPHA * x1 + (1.0 + gt2_ref[0, 0]) * acc, lng_ref[1:2, :], lnb_ref[1:2, :])


def _post(x2d, u2d, mods, layer, tiles_per_batch, w_out, ln_g, ln_b, w1, w2):
    rows = x2d.shape[0]
    row_spec = pl.BlockSpec((TM, D_MODEL), lambda t: (t, 0))
    base = layer * N_MOD
    return pl.pallas_call(
        _post_kernel,
        grid=(rows // TM,),
        in_specs=[
            row_spec, row_spec,
            _mod_spec(base + 2, tiles_per_batch), _mod_spec(base + 3, tiles_per_batch),
            _mod_spec(base + 4, tiles_per_batch), _mod_spec(base + 5, tiles_per_batch),
            _const_spec((D_MODEL, D_MODEL)), _const_spec((2, D_MODEL)), _const_spec((2, D_MODEL)),
            _const_spec((D_MODEL, FF_DIM)), _const_spec((FF_DIM, D_MODEL)),
        ],
        out_specs=row_spec,
        out_shape=jax.ShapeDtypeStruct((rows, D_MODEL), F32),
        compiler_params=pltpu.CompilerParams(dimension_semantics=("arbitrary",),
                                             vmem_limit_bytes=VMEM_LIMIT),
        name="post_mlp",
    )(x2d, u2d, mods, mods, mods, mods, w_out, ln_g, ln_b, w1, w2)


def _dn_in_kernel(x_ref, sh_ref, sc_ref, w_ref, wba_ref, cw_ref, alog_ref, dtb_ref,
                  q_ref, k_ref, v_ref, z_ref, bg_ref, ext_ref, *, tiles_per_batch):
    t = pl.program_id(0)

    @pl.when(t % tiles_per_batch == 0)
    def _():
        ext_ref[0:DN_HALO, :] = jnp.zeros((DN_HALO, 3 * D_MODEL), F32)

    h = (x_ref[...] * (1.0 + sc_ref[0, 0]) + sh_ref[0, 0]).astype(BF16)

    for grp, out_ref in enumerate((q_ref, k_ref, v_ref)):
        cols = slice(grp * D_MODEL, (grp + 1) * D_MODEL)
        ext_ref[DN_HALO:DN_HALO + TM, cols] = _dot(h, w_ref[:, cols])
        ext = ext_ref[:, cols]
        acc = None
        for j in range(DN_CONV):
            off = DN_HALO - (DN_CONV - 1) + j
            res = off % SUBLANES
            shifted = ext if res == 0 else pltpu.roll(ext, DN_HALO + TM - res, axis=0)
            term = cw_ref[j:j + 1, cols] * shifted[off - res:off - res + TM, :]
            acc = term if acc is None else acc + term
        ext_ref[0:DN_HALO, cols] = ext_ref[TM:TM + DN_HALO, cols]
        y = _silu(acc)
        if out_ref is v_ref:
            out_ref[...] = y.astype(BF16)
        else:
            for hd in range(HEADS):
                lanes = slice(hd * HEAD_DIM, (hd + 1) * HEAD_DIM)
                ys = y[:, lanes]
                ss = jnp.sum(ys * ys, axis=-1, keepdims=True)
                out_ref[:, lanes] = (ys * lax.rsqrt(ss + L2_EPS)).astype(BF16)

    z_ref[...] = _dot(h, w_ref[:, 3 * D_MODEL:4 * D_MODEL]).astype(BF16)

    ba = _dot(h, wba_ref[...])
    beta = jax.nn.sigmoid(ba)
    g = -jnp.exp(alog_ref[...]) * _softplus(ba + dtb_ref[...])
    lane = lax.broadcasted_iota(jnp.int32, ba.shape, 1)
    bg_ref[...] = jnp.where(lane < HEADS, beta, g)


def _dn_in(x2d, mods, layer, tiles_per_batch, w_main, w_ba, conv_w, alog_vec, dtb_vec):
    rows = x2d.shape[0]
    row_spec = pl.BlockSpec((TM, D_MODEL), lambda t: (t, 0))
    base = layer * N_MOD
    out_sds = jax.ShapeDtypeStruct((rows, D_MODEL), BF16)
    return pl.pallas_call(
        functools.partial(_dn_in_kernel, tiles_per_batch=tiles_per_batch),
        grid=(rows // TM,),
        in_specs=[
            row_spec, _mod_spec(base + 0, tiles_per_batch), _mod_spec(base + 1, tiles_per_batch),
            _const_spec((D_MODEL, 4 * D_MODEL)), _const_spec((D_MODEL, LANES)),
            _const_spec((DN_CONV, 3 * D_MODEL)), _const_spec((1, LANES)), _const_spec((1, LANES)),
        ],
        out_specs=[row_spec, row_spec, row_spec, row_spec, pl.BlockSpec((TM, LANES), lambda t: (t, 0))],
        out_shape=[out_sds, out_sds, out_sds, out_sds, jax.ShapeDtypeStruct((rows, LANES), F32)],
        scratch_shapes=[pltpu.VMEM((DN_HALO + TM, 3 * D_MODEL), F32)],
        compiler_params=pltpu.CompilerParams(dimension_semantics=("arbitrary",),
                                             vmem_limit_bytes=VMEM_LIMIT),
        name="dn_in",
    )(x2d, mods, mods, w_main, w_ba, conv_w, alog_vec, dtb_vec)


def _inverse_minus_eye_many(a_list):
    ri = lax.broadcasted_iota(jnp.int32, (CHUNK, CHUNK), 0) // BLOCK16
    ci = lax.broadcasted_iota(jnp.int32, (CHUNK, CHUNK), 1) // BLOCK16
    on_diag = ri == ci
    b16 = lambda xs: [x.astype(BF16) for x in xs]
    mm = lambda xs, ys: [_dot(x, y) for x, y in zip(xs, ys)]

    d = [jnp.where(on_diag, a, 0.0) for a in a_list]
    low = [a - x for a, x in zip(a_list, d)]
    d_16 = b16(d)
    d2 = mm(d_16, d_16)
    d2_16 = b16(d2)
    d4 = mm(d2_16, d2_16)
    dd2 = mm(d_16, d2_16)
    p = [x2 - x - y for x, x2, y in zip(d, d2, dd2)]
    d4_16 = b16(d4)
    d8 = mm(d4_16, d4_16)
    pd4 = mm(b16(p), d4_16)
    p = [x + y + z for x, y, z in zip(p, d4, pd4)]
    pd8 = mm(b16(p), b16(d8))
    p = [x + y + z for x, y, z in zip(p, d8, pd8)]
    p_16 = b16(p)
    pl_ = mm(p_16, b16(low))
    m = [x + y for x, y in zip(low, pl_)]
    m_16 = b16(m)
    m2 = mm(m_16, m_16)
    mm2 = mm(m_16, b16(m2))
    q = [x2 - x - y for x, x2, y in zip(m, m2, mm2)]
    qp = mm(b16(q), p_16)
    return [x + y + z for x, y, z in zip(q, p, qp)]


def _dn_core_kernel(q_ref, k_ref, v_ref, z_ref, bg_ref, nw_ref, o_ref,
                    state_ref, gam_ref, egam_ref, edec_ref, egl_ref,
                    u_ref, w_ref, qd_ref, kdt_ref, a_ref):
    @pl.when(pl.program_id(1) == 0)
    def _():
        state_ref[...] = jnp.zeros(state_ref.shape, F32)

    ri = lax.broadcasted_iota(jnp.int32, (TM, TM), 0)
    ci = lax.broadcasted_iota(jnp.int32, (TM, TM), 1)
    same_chunk = (ri // CHUNK) == (ci // CHUNK)
    cum_mat = jnp.where(same_chunk & (ci <= ri), 1.0, 0.0).astype(BF16)
    tot_mat = jnp.where(same_chunk, 1.0, 0.0).astype(BF16)
    bg = bg_ref[...]
    gam = _dot_exact_lhs(cum_mat, bg)
    gtot = _dot_exact_lhs(tot_mat, bg)
    gam_ref[...] = gam
    egam_ref[...] = jnp.exp(gam)
    edec_ref[...] = jnp.exp(gtot - gam)
    egl_ref[...] = jnp.exp(gtot)

    r_idx = lax.broadcasted_iota(jnp.int32, (CHUNK, CHUNK), 0)
    c_idx = lax.broadcasted_iota(jnp.int32, (CHUNK, CHUNK), 1)
    causal = c_idx <= r_idx
    strict = c_idx < r_idx
    scale = HEAD_DIM ** -0.5
    wide = (CHUNK, HEAD_DIM)

    def independent_part(grp, carry):
        probs = []
        for ci_ in range(PASS1_CHUNKS):
            cidx = grp * PASS1_CHUNKS + ci_
            rows = pl.ds(pl.multiple_of(cidx * CHUNK, CHUNK), CHUNK)
            gam_c = gam_ref[rows, :]
            gam_t = gam_c.T
            bg_c = bg_ref[rows, :]
            egam_c = egam_ref[rows, :]
            edec_c = edec_ref[rows, :]
            for hd in range(HEADS):
                lanes = slice(hd * HEAD_DIM, (hd + 1) * HEAD_DIM)
                gl = HEADS + hd
                beta = jnp.broadcast_to(bg_c[:, hd:hd + 1], wide)
                eg = jnp.broadcast_to(egam_c[:, gl:gl + 1], wide)
                ed = jnp.broadcast_to(edec_c[:, gl:gl + 1], wide)
                gcol = jnp.broadcast_to(gam_c[:, gl:gl + 1], wide)[:, :CHUNK]
                decay = jnp.exp(jnp.where(causal, gcol - gam_t[gl:gl + 1, :], -jnp.inf))
                probs.append(dict(cidx=cidx, rows=rows, hd=hd, lanes=lanes, beta=beta, eg=eg, ed=ed,
                                  decay=decay, q16=q_ref[rows, lanes], k16=k_ref[rows, lanes],
                                  v=v_ref[rows, lanes].astype(F32)))
        kkqk = [_dot_nt(jnp.concatenate([p["k16"], p["q16"]], axis=0), p["k16"])
                for p in probs]
        a_list = []
        for p, s in zip(probs, kkqk):
            a_list.append(jnp.where(strict, s[:CHUNK] * p["beta"][:, :CHUNK] * p["decay"], 0.0))
            a_ref[p["hd"], p["rows"], :] = (s[CHUNK:] * (p["decay"] * scale)).astype(BF16)
        tinv = _inverse_minus_eye_many(a_list)
        for p in probs:
            p["k"] = p["k16"].astype(F32)
            p["q"] = p["q16"].astype(F32)
            p["vb"] = p["v"] * p["beta"]
            p["kbg"] = p["k"] * p["beta"] * p["eg"]
        uw = [_dot(t.astype(BF16), jnp.concatenate([p["vb"], p["kbg"]], axis=1).astype(BF16))
              for p, t in zip(probs, tinv)]
        for p, r in zip(probs, uw):
            rows, lanes = p["rows"], p["lanes"]
            u_ref[rows, lanes] = p["vb"] + r[:, :HEAD_DIM]
            w_ref[rows, lanes] = (p["kbg"] + r[:, HEAD_DIM:]).astype(BF16)
            qd_ref[rows, lanes] = (p["q"] * (p["eg"] * scale)).astype(BF16)
            kdt_ref[p["hd"], p["cidx"]] = (p["k"] * p["ed"]).T.astype(BF16)
        return carry

    lax.fori_loop(0, TM // CHUNK // PASS1_CHUNKS, independent_part, 0)

    nw = nw_ref[...]

    def recurrent_part(c, carry):
        rows = pl.ds(pl.multiple_of(c * CHUNK, CHUNK), CHUNK)
        egl_c = egl_ref[rows, :]
        heads = range(HEADS)
        lanes = [slice(hd * HEAD_DIM, (hd + 1) * HEAD_DIM) for hd in heads]
        state = [state_ref[hd] for hd in heads]
        wq = [jnp.concatenate([w_ref[rows, lanes[hd]], qd_ref[rows, lanes[hd]]], axis=0) for hd in heads]
        r = [_dot(wq[hd], state[hd].astype(BF16)) for hd in heads]
        vn16 = [(u_ref[rows, lanes[hd]] - r[hd][:CHUNK]).astype(BF16) for hd in heads]
        av = [_dot(a_ref[hd, rows, :], vn16[hd]) for hd in heads]
        kv = [_dot(kdt_ref[hd, c], vn16[hd]) for hd in heads]
        for hd in heads:
            state_ref[hd] = state[hd] * egl_c[0:1, HEADS + hd:HEADS + hd + 1] + kv[hd]
            o = r[hd][CHUNK:] + av[hd]
            on = o * lax.rsqrt(jnp.mean(o * o, axis=-1, keepdims=True) + RMS_EPS) * nw
            o_ref[rows, lanes[hd]] = (on * _silu(z_ref[rows, lanes[hd]].astype(F32))).astype(BF16)
        return carry

    lax.fori_loop(0, TM // CHUNK, recurrent_part, 0)


def _dn_core(q, k, v, z, bg, norm_w, batch, tiles_per_batch):
    rows = q.shape[0]
    row_spec = pl.BlockSpec((TM, D_MODEL), lambda b, s: (b * tiles_per_batch + s, 0))
    small = pltpu.VMEM((TM, LANES), F32)
    return pl.pallas_call(
        _dn_core_kernel,
        grid=(batch, tiles_per_batch),
        in_specs=[row_spec, row_spec, row_spec, row_spec,
                  pl.BlockSpec((TM, LANES), lambda b, s: (b * tiles_per_batch + s, 0)),
                  pl.BlockSpec((1, HEAD_DIM), lambda b, s: (0, 0))],
        out_specs=row_spec,
        out_shape=jax.ShapeDtypeStruct((rows, D_MODEL), BF16),
        scratch_shapes=[
            pltpu.VMEM((HEADS, HEAD_DIM, HEAD_DIM), F32),
            small, small, small, small,
            pltpu.VMEM((TM, D_MODEL), F32),
            pltpu.VMEM((TM, D_MODEL), BF16),
            pltpu.VMEM((TM, D_MODEL), BF16),
            pltpu.VMEM((HEADS, TM // CHUNK, HEAD_DIM, CHUNK), BF16),
            pltpu.VMEM((HEADS, TM, CHUNK), BF16),
        ],
        compiler_params=pltpu.CompilerParams(dimension_semantics=("arbitrary", "arbitrary"),
                                             vmem_limit_bytes=VMEM_LIMIT),
        name="dn_core",
    )(q, k, v, z, bg, norm_w)


def _cf_conv_piece(ext_ref, dw_ref, res, piece, not_before):
    n = CF_PIECE + CF_HALO + SUBLANES
    start = pl.multiple_of(piece * CF_PIECE + _zero_after(not_before), SUBLANES)
    window = ext_ref[pl.ds(start, n), :]
    shifted = window if res == 0 else pltpu.roll(window, n - res, axis=0)
    acc = None
    for j in range(CF_KERNEL):
        off = CF_HALO - (CF_KERNEL - 1) + j
        if off % SUBLANES != res:
            continue
        term = dw_ref[j:j + 1, :] * shifted[off - res:off - res + CF_PIECE, :]
        acc = term if acc is None else acc + term
    return acc


def _cf_layer_kernel(x_ref, xres_ref, sh1_ref, sc1_ref, gt1_ref, sh2_ref, sc2_ref, gt2_ref,
                     win_ref, dw_ref, dwb_ref, cg_ref, cb_ref, wo_ref, lng_ref, lnb_ref, w1_ref, w2_ref,
                     o_ref, ext_ref, ext_next_ref, u_ref, u_next_ref, *, n_tiles, tiles_per_batch):
    s = pl.program_id(0)
    t_in = jnp.minimum(s, n_tiles - 1)
    first_of_batch = t_in % tiles_per_batch == 0

    @pl.when(s == 0)
    def _():
        ext_ref[...] = jnp.zeros(ext_ref.shape, F32)
        u_ref[...] = jnp.zeros(u_ref.shape, BF16)
        ext_next_ref[CF_HALO + TM:CF_HALO + TM + SUBLANES, :] = jnp.zeros((SUBLANES, D_MODEL), F32)

    @pl.when(first_of_batch)
    def _():
        ext_next_ref[0:CF_HALO, :] = jnp.zeros((CF_HALO, D_MODEL), F32)

    @pl.when(jnp.logical_not(first_of_batch))
    def _():
        ext_next_ref[0:CF_HALO, :] = ext_ref[TM:TM + CF_HALO, :]

    n_chunks = FF_DIM // FF_TILE
    n_pieces = TM // CF_PIECE
    conv = [None] * n_pieces
    todo = [(res, piece) for res in range(SUBLANES) for piece in range(n_pieces)]
    slots_seen = [0]

    def conv_pieces_after(result):
        for blk in range(n_pieces):
            slots_seen[0] += 1
            if slots_seen[0] % 5 == 0 or not todo:
                continue
            res, piece = todo.pop(0)
            r0 = blk * CF_PIECE
            part = _cf_conv_piece(ext_ref, dw_ref, res, piece, result[r0:r0 + 1, 0:1])
            conv[piece] = part if conv[piece] is None else conv[piece] + part

    y = _dot(u_ref[...], wo_ref[...])
    conv_pieces_after(y)

    h = (x_ref[...] * (1.0 + sc1_ref[0, 0]) + sh1_ref[0, 0]).astype(BF16)
    val = _dot(h, win_ref[:, 0:D_MODEL])
    conv_pieces_after(val)
    gate = _dot(h, win_ref[:, D_MODEL:2 * D_MODEL])
    conv_pieces_after(gate)
    ext_next_ref[CF_HALO:CF_HALO + TM, :] = val * jax.nn.sigmoid(gate)

    x1 = _layer_norm(ALPHA * xres_ref[...] + (1.0 + gt1_ref[0, 0]) * y, lng_ref[0:1, :], lnb_ref[0:1, :])
    h2 = (x1 * (1.0 + sc2_ref[0, 0]) + sh2_ref[0, 0]).astype(BF16)
    chunk_cols = [slice(c * FF_TILE, (c + 1) * FF_TILE) for c in range(n_chunks)]
    d1 = _dot(h2, w1_ref[:, chunk_cols[0]])
    conv_pieces_after(d1)
    acc = None
    for c in range(n_chunks):
        d1_next = None
        if c + 1 < n_chunks:
            d1_next = _dot(h2, w1_ref[:, chunk_cols[c + 1]])
            conv_pieces_after(d1_next)
        a = jnp.maximum(d1, 0.0)
        p = _dot((a * a).astype(BF16), w2_ref[chunk_cols[c], :])
        conv_pieces_after(p)
        acc = p if acc is None else acc + p
        d1 = d1_next
    assert not todo
    o_ref[...] = _layer_norm(ALPHA * x1 + (1.0 + gt2_ref[0, 0]) * acc, lng_ref[1:2, :], lnb_ref[1:2, :])

    cu = _layer_norm(jnp.concatenate(conv, axis=0) + dwb_ref[...], cg_ref[...], cb_ref[...])
    u_next_ref[...] = _silu(cu).astype(BF16)

    u_ref[...] = u_next_ref[...]
    ext_ref[...] = ext_next_ref[...]


def _cf_layer(x2d, mods, layer, tiles_per_batch, w_in, dw_w, dw_b, cf_g, cf_b, w_out, ln_g, ln_b, w1, w2):
    rows = x2d.shape[0]
    n_tiles = rows // TM
    base = layer * N_MOD
    in_tile = lambda s: jnp.minimum(s, n_tiles - 1)
    mlp_tile = lambda s: jnp.maximum(s - 2, 0)

    def mod_spec(row, tile_of_step):
        return pl.BlockSpec((1, 1, 1, D_MODEL), lambda s: (row, tile_of_step(s) // tiles_per_batch, 0, 0))

    ext_rows = CF_HALO + TM + SUBLANES
    return pl.pallas_call(
        functools.partial(_cf_layer_kernel, n_tiles=n_tiles, tiles_per_batch=tiles_per_batch),
        grid=(n_tiles + 2,),
        in_specs=[
            pl.BlockSpec((TM, D_MODEL), lambda s: (in_tile(s), 0)),
            pl.BlockSpec((TM, D_MODEL), lambda s: (mlp_tile(s), 0)),
            mod_spec(base + 0, in_tile), mod_spec(base + 1, in_tile),
            mod_spec(base + 2, mlp_tile), mod_spec(base + 3, mlp_tile),
            mod_spec(base + 4, mlp_tile), mod_spec(base + 5, mlp_tile),
            _const_spec((D_MODEL, 2 * D_MODEL)), _const_spec((CF_KERNEL, D_MODEL)),
            _const_spec((1, D_MODEL)), _const_spec((1, D_MODEL)), _const_spec((1, D_MODEL)),
            _const_spec((D_MODEL, D_MODEL)), _const_spec((2, D_MODEL)), _const_spec((2, D_MODEL)),
            _const_spec((D_MODEL, FF_DIM)), _const_spec((FF_DIM, D_MODEL)),
        ],
        out_specs=pl.BlockSpec((TM, D_MODEL), lambda s: (mlp_tile(s), 0)),
        out_shape=jax.ShapeDtypeStruct((rows, D_MODEL), F32),
        scratch_shapes=[pltpu.VMEM((ext_rows, D_MODEL), F32),
                        pltpu.VMEM((ext_rows, D_MODEL), F32),
                        pltpu.VMEM((TM, D_MODEL), BF16),
                        pltpu.VMEM((TM, D_MODEL), BF16)],
        compiler_params=pltpu.CompilerParams(dimension_semantics=("arbitrary",),
                                             vmem_limit_bytes=VMEM_LIMIT),
        name="cf_layer",
    )(x2d, x2d, mods, mods, mods, mods, mods, mods, w_in, dw_w, dw_b, cf_g, cf_b, w_out, ln_g, ln_b, w1, w2)


def kernel(x, c, ada_w, ada_b, ln_g, ln_b, dn_w_in, dn_conv_w, dn_a_log, dn_dt_bias, dn_norm_w,
           dn_w_out, cf_w_in, cf_dw_w, cf_dw_b, cf_ln_g, cf_ln_b, cf_w_out, ff_w1, ff_w2):
    batch, seq, d = x.shape
    assert d == D_MODEL and seq % TM == 0 and TM % CHUNK == 0
    tiles_per_batch = seq // TM
    n_qkv = 3 * D_MODEL
    n_main = 4 * D_MODEL

    mods = _ada_mods(c, ada_w, ada_b)
    x2d = x.reshape(batch * seq, D_MODEL)
    for i in range(DEPTH):
        j = i // 2
        w1, w2 = ff_w1[i].astype(BF16), ff_w2[i].astype(BF16)
        if i % 2 == 0:
            w_in = dn_w_in[j]
            w_main = w_in[:, :n_main].astype(BF16)
            w_ba = jnp.pad(w_in[:, n_main:], ((0, 0), (0, LANES - 2 * HEADS))).astype(BF16)
            gate_pad = (HEADS, LANES - 2 * HEADS)
            alog_vec = jnp.pad(dn_a_log[j], gate_pad).reshape(1, LANES)
            dtb_vec = jnp.pad(dn_dt_bias[j], gate_pad).reshape(1, LANES)
            q, k, v, z, bg = _dn_in(x2d, mods, i, tiles_per_batch, w_main, w_ba,
                                    dn_conv_w[j][:, :n_qkv], alog_vec, dtb_vec)
            u = _dn_core(q, k, v, z, bg, dn_norm_w[j].reshape(1, HEAD_DIM), batch, tiles_per_batch)
            x2d = _post(x2d, u, mods, i, tiles_per_batch, dn_w_out[j].astype(BF16), ln_g[i], ln_b[i],
                        w1, w2)
        else:
            x2d = _cf_layer(x2d, mods, i, tiles_per_batch, cf_w_in[j].astype(BF16), cf_dw_w[j],
                            cf_dw_b[j].reshape(1, D_MODEL), cf_ln_g[j].reshape(1, D_MODEL),
                            cf_ln_b[j].reshape(1, D_MODEL), cf_w_out[j].astype(BF16), ln_g[i], ln_b[i],
                            w1, w2)
    return x2d.reshape(batch, seq, D_MODEL)
```

```python
import functools

import jax
import jax.numpy as jnp
from jax import lax
from jax.experimental import pallas as pl
from jax.experimental.pallas import tpu as pltpu

F32 = jnp.float32
BF16 = jnp.bfloat16

D_MODEL = 1024
DEPTH = 4
N_MOD = 6
HEADS = 8
HEAD_DIM = 128
DN_CONV = 4
CHUNK = 64
BLOCK16 = 16
CF_KERNEL = 31
FF_DIM = 4 * D_MODEL
ALPHA = (2.0 * DEPTH) ** 0.25
LN_EPS = 1e-5
RMS_EPS = 1e-6
L2_EPS = 1e-6

LANES = 128
SUBLANES = 8
TM = 512
FF_TILE = 1024
CF_HALO = 32
CF_PIECE = 128
DN_HALO = 8
PASS1_CHUNKS = 4
VMEM_LIMIT = 56 * 1024 * 1024
CAST_BLOCK_ELEMS = 1024 * 1024


def _dot(a, b):
    return jnp.dot(a, b, preferred_element_type=F32)


def _dot_nt(a, b):
    return lax.dot_general(a, b, (((1,), (1,)), ((), ())), preferred_element_type=F32)


def _dot_tn(a, b):
    return lax.dot_general(a, b, (((0,), (0,)), ((), ())), preferred_element_type=F32)


def _bdot(a, b):
    return _dot(a.astype(BF16), b.astype(BF16))


def _split3(x):
    x1 = x.astype(BF16)
    r1 = x - x1.astype(F32)
    x2 = r1.astype(BF16)
    x3 = (r1 - x2.astype(F32)).astype(BF16)
    return x1, x2, x3


def _dot_exact_lhs(a_bf16, x):
    x1, x2, x3 = _split3(x)
    return _dot(a_bf16, x3) + _dot(a_bf16, x2) + _dot(a_bf16, x1)


def _layer_norm(r, g, b):
    mu = jnp.mean(r, axis=-1, keepdims=True)
    d = r - mu
    var = jnp.mean(d * d, axis=-1, keepdims=True)
    return d * lax.rsqrt(var + LN_EPS) * g + b


def _silu(x):
    return x * jax.nn.sigmoid(x)


def _softplus(x):
    return jnp.maximum(x, 0.0) + jnp.log1p(jnp.exp(-jnp.abs(x)))


def _zero_after(v):
    bits = lax.bitcast_convert_type(v, jnp.uint32)
    z = lax.shift_right_logical(lax.shift_right_logical(bits, jnp.uint32(16)), jnp.uint32(16))
    return z[0, 0].astype(jnp.int32)


def _const_spec(shape, layer=None):
    nd = len(shape)
    if layer is None:
        return pl.BlockSpec(shape, lambda *_: (0,) * nd, pipeline_mode=pl.Buffered(1))
    return pl.BlockSpec((None,) + tuple(shape), lambda *_: (layer,) + (0,) * nd,
                        pipeline_mode=pl.Buffered(1))


def _mod_spec(row, tiles_per_batch):
    return pl.BlockSpec((1, 1, 1, D_MODEL), lambda t: (row, t // tiles_per_batch, 0, 0))


def _ada_kernel(c_ref, w_ref, b_ref, o_ref):
    cond = _silu(c_ref[...])
    o_ref[0] = _dot(cond, w_ref[0]) + b_ref[0]


def _ada_mods(c, ada_w, ada_b):
    batch = c.shape[0]
    out = pl.pallas_call(
        _ada_kernel,
        grid=(DEPTH, N_MOD),
        in_specs=[
            pl.BlockSpec((batch, D_MODEL), lambda i, j: (0, 0)),
            pl.BlockSpec((1, D_MODEL, D_MODEL), lambda i, j: (i, 0, j)),
            pl.BlockSpec((1, 1, D_MODEL), lambda i, j: (i, 0, j)),
        ],
        out_specs=pl.BlockSpec((1, batch, D_MODEL), lambda i, j: (i * N_MOD + j, 0, 0)),
        out_shape=jax.ShapeDtypeStruct((DEPTH * N_MOD, batch, D_MODEL), F32),
        compiler_params=pltpu.CompilerParams(dimension_semantics=("arbitrary", "arbitrary")),
        name="ada_mods",
    )(c, ada_w, ada_b.reshape(DEPTH, 1, N_MOD * D_MODEL))
    return out.reshape(DEPTH * N_MOD, batch, 1, D_MODEL)


def _cast_kernel(w_ref, o_ref):
    o_ref[...] = w_ref[...].astype(BF16)


def _cast_stack(w, cols=None):
    n_layers, rows, all_cols = w.shape
    cols = all_cols if cols is None else cols
    block_rows = min(rows, CAST_BLOCK_ELEMS // cols)
    assert rows % block_rows == 0 and cols % LANES == 0
    spec = pl.BlockSpec((None, block_rows, cols), lambda l, i: (l, i, 0))
    return pl.pallas_call(
        _cast_kernel,
        grid=(n_layers, rows // block_rows),
        in_specs=[spec],
        out_specs=spec,
        out_shape=jax.ShapeDtypeStruct((n_layers, rows, cols), BF16),
        compiler_params=pltpu.CompilerParams(dimension_semantics=("arbitrary", "arbitrary")),
        name="cast_bf16",
    )(w)


def _post_kernel(x_ref, u_ref, gt1_ref, sh2_ref, sc2_ref, gt2_ref, wo_ref, lng_ref, lnb_ref,
                 w1_ref, w2_ref, o_ref):
    x = x_ref[...]
    y = _dot(u_ref[...], wo_ref[...])
    x1 = _layer_norm(ALPHA * x + (1.0 + gt1_ref[0, 0]) * y, lng_ref[0:1, :], lnb_ref[0:1, :])
    h = (x1 * (1.0 + sc2_ref[0, 0]) + sh2_ref[0, 0]).astype(BF16)
    n_chunks = FF_DIM // FF_TILE
    chunk_cols = [slice(c * FF_TILE, (c + 1) * FF_TILE) for c in range(n_chunks)]
    d1 = _dot(h, w1_ref[:, chunk_cols[0]])
    acc = None
    for c in range(n_chunks):
        d1_next = _dot(h, w1_ref[:, chunk_cols[c + 1]]) if c + 1 < n_chunks else None
        a = jnp.maximum(d1, 0.0)
        p = _dot((a * a).astype(BF16), w2_ref[chunk_cols[c], :])
        acc = p if acc is None else acc + p
        d1 = d1_next
    o_ref[...] = _layer_norm(ALPHA * x1 + (1.0 + gt2_ref[0, 0]) * acc, lng_ref[1:2, :], lnb_ref[1:2, :])


def _post(x2d, u2d, mods, layer, mixer, tiles_per_batch, w_out, ln_g, ln_b, w1, w2):
    rows = x2d.shape[0]
    row_spec = pl.BlockSpec((TM, D_MODEL), lambda t: (t, 0))
    base = layer * N_MOD
    return pl.pallas_call(
        _post_kernel,
        grid=(rows // TM,),
        in_specs=[
            row_spec, row_spec,
            _mod_spec(base + 2, tiles_per_batch), _mod_spec(base + 3, tiles_per_batch),
            _mod_spec(base + 4, tiles_per_batch), _mod_spec(base + 5, tiles_per_batch),
            _const_spec((D_MODEL, D_MODEL), mixer), _const_spec((2, D_MODEL)), _const_spec((2, D_MODEL)),
            _const_spec((D_MODEL, FF_DIM), layer), _const_spec((FF_DIM, D_MODEL), layer),
        ],
        out_specs=row_spec,
        out_shape=jax.ShapeDtypeStruct((rows, D_MODEL), F32),
        compiler_params=pltpu.CompilerParams(dimension_semantics=("arbitrary",),
                                             vmem_limit_bytes=VMEM_LIMIT),
        name="post_mlp",
    )(x2d, u2d, mods, mods, mods, mods, w_out, ln_g, ln_b, w1, w2)


def _dn_in_kernel(x_ref, sh_ref, sc_ref, w_ref, wba_ref, cw_ref, alog_ref, dtb_ref,
                  q_ref, k_ref, v_ref, z_ref, bg_ref, ext_ref, *, tiles_per_batch):
    t = pl.program_id(0)

    @pl.when(t % tiles_per_batch == 0)
    def _():
        ext_ref[0:DN_HALO, :] = jnp.zeros((DN_HALO, 3 * D_MODEL), F32)

    h = (x_ref[...] * (1.0 + sc_ref[0, 0]) + sh_ref[0, 0]).astype(BF16)

    for grp, out_ref in enumerate((q_ref, k_ref, v_ref)):
        cols = slice(grp * D_MODEL, (grp + 1) * D_MODEL)
        ext_ref[DN_HALO:DN_HALO + TM, cols] = _dot(h, w_ref[:, cols])
        ext = ext_ref[:, cols]
        acc = None
        for j in range(DN_CONV):
            off = DN_HALO - (DN_CONV - 1) + j
            res = off % SUBLANES
            shifted = ext if res == 0 else pltpu.roll(ext, DN_HALO + TM - res, axis=0)
            term = cw_ref[j:j + 1, cols] * shifted[off - res:off - res + TM, :]
            acc = term if acc is None else acc + term
        ext_ref[0:DN_HALO, cols] = ext_ref[TM:TM + DN_HALO, cols]
        y = _silu(acc)
        if out_ref is v_ref:
            out_ref[...] = y
        else:
            for hd in range(HEADS):
                lanes = slice(hd * HEAD_DIM, (hd + 1) * HEAD_DIM)
                ys = y[:, lanes]
                ss = jnp.sum(ys * ys, axis=-1, keepdims=True)
                out_ref[:, lanes] = ys * lax.rsqrt(ss + L2_EPS)

    z_ref[...] = _dot(h, w_ref[:, 3 * D_MODEL:4 * D_MODEL])

    ba = _dot(h, wba_ref[...])
    beta = jax.nn.sigmoid(ba)
    g = -jnp.exp(alog_ref[...]) * _softplus(ba + dtb_ref[...])
    lane = lax.broadcasted_iota(jnp.int32, ba.shape, 1)
    bg_ref[...] = jnp.where(lane < HEADS, beta, g)


def _dn_in(x2d, mods, layer, mixer, tiles_per_batch, w_main, w_ba, conv_w, alog_vec, dtb_vec):
    rows = x2d.shape[0]
    row_spec = pl.BlockSpec((TM, D_MODEL), lambda t: (t, 0))
    base = layer * N_MOD
    out_sds = jax.ShapeDtypeStruct((rows, D_MODEL), F32)
    return pl.pallas_call(
        functools.partial(_dn_in_kernel, tiles_per_batch=tiles_per_batch),
        grid=(rows // TM,),
        in_specs=[
            row_spec, _mod_spec(base + 0, tiles_per_batch), _mod_spec(base + 1, tiles_per_batch),
            _const_spec((D_MODEL, 4 * D_MODEL), mixer), _const_spec((D_MODEL, LANES)),
            _const_spec((DN_CONV, 3 * D_MODEL)), _const_spec((1, LANES)), _const_spec((1, LANES)),
        ],
        out_specs=[row_spec, row_spec, row_spec, row_spec, pl.BlockSpec((TM, LANES), lambda t: (t, 0))],
        out_shape=[out_sds, out_sds, out_sds, out_sds, jax.ShapeDtypeStruct((rows, LANES), F32)],
        scratch_shapes=[pltpu.VMEM((DN_HALO + TM, 3 * D_MODEL), F32)],
        compiler_params=pltpu.CompilerParams(dimension_semantics=("arbitrary",),
                                             vmem_limit_bytes=VMEM_LIMIT),
        name="dn_in",
    )(x2d, mods, mods, w_main, w_ba, conv_w, alog_vec, dtb_vec)


def _inverse_minus_eye_many(a_list):
    ri = lax.broadcasted_iota(jnp.int32, (CHUNK, CHUNK), 0) // BLOCK16
    ci = lax.broadcasted_iota(jnp.int32, (CHUNK, CHUNK), 1) // BLOCK16
    on_diag = ri == ci
    b16 = lambda xs: [x.astype(BF16) for x in xs]
    mm = lambda xs, ys: [_dot(x, y) for x, y in zip(xs, ys)]

    d = [jnp.where(on_diag, a, 0.0) for a in a_list]
    low = [a - x for a, x in zip(a_list, d)]
    d_16 = b16(d)
    d2 = mm(d_16, d_16)
    d2_16 = b16(d2)
    d4 = mm(d2_16, d2_16)
    dd2 = mm(d_16, d2_16)
    p = [x2 - x - y for x, x2, y in zip(d, d2, dd2)]
    d4_16 = b16(d4)
    d8 = mm(d4_16, d4_16)
    pd4 = mm(b16(p), d4_16)
    p = [x + y + z for x, y, z in zip(p, d4, pd4)]
    pd8 = mm(b16(p), b16(d8))
    p = [x + y + z for x, y, z in zip(p, d8, pd8)]
    p_16 = b16(p)
    pl_ = mm(p_16, b16(low))
    m = [x + y for x, y in zip(low, pl_)]
    m_16 = b16(m)
    m2 = mm(m_16, m_16)
    mm2 = mm(m_16, b16(m2))
    q = [x2 - x - y for x, x2, y in zip(m, m2, mm2)]
    qp = mm(b16(q), p_16)
    return [x + y + z for x, y, z in zip(q, p, qp)]


def _dn_core_kernel(q_ref, k_ref, v_ref, z_ref, bg_ref, nw_ref, o_ref,
                    state_ref, gam_ref, egam_ref, edec_ref, egl_ref,
                    u_ref, w_ref, qd_ref, kdt_ref, a_ref):
    @pl.when(pl.program_id(1) == 0)
    def _():
        state_ref[...] = jnp.zeros(state_ref.shape, F32)

    ri = lax.broadcasted_iota(jnp.int32, (TM, TM), 0)
    ci = lax.broadcasted_iota(jnp.int32, (TM, TM), 1)
    same_chunk = (ri // CHUNK) == (ci // CHUNK)
    cum_mat = jnp.where(same_chunk & (ci <= ri), 1.0, 0.0).astype(BF16)
    tot_mat = jnp.where(same_chunk, 1.0, 0.0).astype(BF16)
    bg = bg_ref[...]
    gam = _dot_exact_lhs(cum_mat, bg)
    gtot = _dot_exact_lhs(tot_mat, bg)
    gam_ref[...] = gam
    egam_ref[...] = jnp.exp(gam)
    edec_ref[...] = jnp.exp(gtot - gam)
    egl_ref[...] = jnp.exp(gtot)

    r_idx = lax.broadcasted_iota(jnp.int32, (CHUNK, CHUNK), 0)
    c_idx = lax.broadcasted_iota(jnp.int32, (CHUNK, CHUNK), 1)
    causal = c_idx <= r_idx
    strict = c_idx < r_idx
    scale = HEAD_DIM ** -0.5
    wide = (CHUNK, HEAD_DIM)

    def independent_part(grp, carry):
        probs = []
        for ci_ in range(PASS1_CHUNKS):
            cidx = grp * PASS1_CHUNKS + ci_
            rows = pl.ds(pl.multiple_of(cidx * CHUNK, CHUNK), CHUNK)
            gam_c = gam_ref[rows, :]
            gam_t = gam_c.T
            bg_c = bg_ref[rows, :]
            egam_c = egam_ref[rows, :]
            edec_c = edec_ref[rows, :]
            for hd in range(HEADS):
                lanes = slice(hd * HEAD_DIM, (hd + 1) * HEAD_DIM)
                gl = HEADS + hd
                beta = jnp.broadcast_to(bg_c[:, hd:hd + 1], wide)
                eg = jnp.broadcast_to(egam_c[:, gl:gl + 1], wide)
                ed = jnp.broadcast_to(edec_c[:, gl:gl + 1], wide)
                gcol = jnp.broadcast_to(gam_c[:, gl:gl + 1], wide)[:, :CHUNK]
                decay = jnp.exp(jnp.where(causal, gcol - gam_t[gl:gl + 1, :], -jnp.inf))
                probs.append(dict(cidx=cidx, rows=rows, hd=hd, lanes=lanes, beta=beta, eg=eg, ed=ed,
                                  decay=decay, q=q_ref[rows, lanes], k=k_ref[rows, lanes],
                                  v=v_ref[rows, lanes]))
        for p in probs:
            p["k16"] = p["k"].astype(BF16)
            p["kq16"] = jnp.concatenate([p["k16"], (p["q"] * scale).astype(BF16)], axis=0)
        kkqk = [_dot_nt(p["kq16"], p["k16"]) for p in probs]
        a_list = []
        for p, s in zip(probs, kkqk):
            a_list.append(jnp.where(strict, s[:CHUNK] * p["beta"][:, :CHUNK] * p["decay"], 0.0))
            a_ref[p["hd"], p["rows"], :] = (s[CHUNK:] * p["decay"]).astype(BF16)
        tinv = _inverse_minus_eye_many(a_list)
        for p in probs:
            p["vb"] = p["v"] * p["beta"]
            p["kbg"] = p["k"] * p["beta"] * p["eg"]
        uw = [_dot(t.astype(BF16), jnp.concatenate([p["vb"], p["kbg"]], axis=1).astype(BF16))
              for p, t in zip(probs, tinv)]
        for p, r in zip(probs, uw):
            rows, lanes = p["rows"], p["lanes"]
            u_ref[rows, lanes] = p["vb"] + r[:, :HEAD_DIM]
            w_ref[rows, lanes] = (p["kbg"] + r[:, HEAD_DIM:]).astype(BF16)
            qd_ref[rows, lanes] = (p["q"] * (p["eg"] * scale)).astype(BF16)
            kdt_ref[p["hd"], p["cidx"]] = (p["k"] * p["ed"]).T.astype(BF16)
        return carry

    lax.fori_loop(0, TM // CHUNK // PASS1_CHUNKS, independent_part, 0)

    nw = nw_ref[...]

    def recurrent_part(c, carry):
        rows = pl.ds(pl.multiple_of(c * CHUNK, CHUNK), CHUNK)
        egl_c = egl_ref[rows, :]
        heads = range(HEADS)
        lanes = [slice(hd * HEAD_DIM, (hd + 1) * HEAD_DIM) for hd in heads]
        state = [state_ref[hd] for hd in heads]
        wq = [jnp.concatenate([w_ref[rows, lanes[hd]], qd_ref[rows, lanes[hd]]], axis=0) for hd in heads]
        r = [_dot(wq[hd], state[hd].astype(BF16)) for hd in heads]
        vn16 = [(u_ref[rows, lanes[hd]] - r[hd][:CHUNK]).astype(BF16) for hd in heads]
        av = [_dot(a_ref[hd, rows, :], vn16[hd]) for hd in heads]
        kv = [_dot(kdt_ref[hd, c], vn16[hd]) for hd in heads]
        for hd in heads:
            state_ref[hd] = state[hd] * egl_c[0:1, HEADS + hd:HEADS + hd + 1] + kv[hd]
            o = r[hd][CHUNK:] + av[hd]
            on = o * lax.rsqrt(jnp.mean(o * o, axis=-1, keepdims=True) + RMS_EPS) * nw
            o_ref[rows, lanes[hd]] = (on * _silu(z_ref[rows, lanes[hd]])).astype(BF16)
        return carry

    lax.fori_loop(0, TM // CHUNK, recurrent_part, 0)


def _dn_core(q, k, v, z, bg, norm_w, batch, tiles_per_batch):
    rows = q.shape[0]
    row_spec = pl.BlockSpec((TM, D_MODEL), lambda b, s: (b * tiles_per_batch + s, 0))
    small = pltpu.VMEM((TM, LANES), F32)
    return pl.pallas_call(
        _dn_core_kernel,
        grid=(batch, tiles_per_batch),
        in_specs=[row_spec, row_spec, row_spec, row_spec,
                  pl.BlockSpec((TM, LANES), lambda b, s: (b * tiles_per_batch + s, 0)),
                  pl.BlockSpec((1, HEAD_DIM), lambda b, s: (0, 0))],
        out_specs=row_spec,
        out_shape=jax.ShapeDtypeStruct((rows, D_MODEL), BF16),
        scratch_shapes=[
            pltpu.VMEM((HEADS, HEAD_DIM, HEAD_DIM), F32),
            small, small, small, small,
            pltpu.VMEM((TM, D_MODEL), F32),
            pltpu.VMEM((TM, D_MODEL), BF16),
            pltpu.VMEM((TM, D_MODEL), BF16),
            pltpu.VMEM((HEADS, TM // CHUNK, HEAD_DIM, CHUNK), BF16),
            pltpu.VMEM((HEADS, TM, CHUNK), BF16),
        ],
        compiler_params=pltpu.CompilerParams(dimension_semantics=("arbitrary", "arbitrary"),
                                             vmem_limit_bytes=VMEM_LIMIT),
        name="dn_core",
    )(q, k, v, z, bg, norm_w)


def _cf_conv_piece(ext_ref, dw_ref, res, piece, not_before):
    n = CF_PIECE + CF_HALO + SUBLANES
    start = pl.multiple_of(piece * CF_PIECE + _zero_after(not_before), SUBLANES)
    window = ext_ref[pl.ds(start, n), :]
    shifted = window if res == 0 else pltpu.roll(window, n - res, axis=0)
    acc = None
    for j in range(CF_KERNEL):
        off = CF_HALO - (CF_KERNEL - 1) + j
        if off % SUBLANES != res:
            continue
        term = dw_ref[j:j + 1, :] * shifted[off - res:off - res + CF_PIECE, :]
        acc = term if acc is None else acc + term
    return acc


def _cf_layer_kernel(x_ref, xres_ref, sh1_ref, sc1_ref, gt1_ref, sh2_ref, sc2_ref, gt2_ref,
                     win_ref, dw_ref, dwb_ref, cg_ref, cb_ref, wo_ref, lng_ref, lnb_ref, w1_ref, w2_ref,
                     o_ref, ext_ref, ext_next_ref, u_ref, u_next_ref, *, n_tiles, tiles_per_batch):
    s = pl.program_id(0)
    t_in = jnp.minimum(s, n_tiles - 1)
    first_of_batch = t_in % tiles_per_batch == 0

    @pl.when(s == 0)
    def _():
        ext_ref[...] = jnp.zeros(ext_ref.shape, F32)
        u_ref[...] = jnp.zeros(u_ref.shape, BF16)
        ext_next_ref[CF_HALO + TM:CF_HALO + TM + SUBLANES, :] = jnp.zeros((SUBLANES, D_MODEL), F32)

    @pl.when(first_of_batch)
    def _():
        ext_next_ref[0:CF_HALO, :] = jnp.zeros((CF_HALO, D_MODEL), F32)

    @pl.when(jnp.logical_not(first_of_batch))
    def _():
        ext_next_ref[0:CF_HALO, :] = ext_ref[TM:TM + CF_HALO, :]

    n_chunks = FF_DIM // FF_TILE
    n_pieces = TM // CF_PIECE
    conv = [None] * n_pieces
    todo = [(res, piece) for res in range(SUBLANES) for piece in range(n_pieces)]
    slots_seen = [0]

    def conv_pieces_after(result):
        for blk in range(n_pieces):
            slots_seen[0] += 1
            if slots_seen[0] % 5 == 0 or not todo:
                continue
            res, piece = todo.pop(0)
            r0 = blk * CF_PIECE
            part = _cf_conv_piece(ext_ref, dw_ref, res, piece, result[r0:r0 + 1, 0:1])
            conv[piece] = part if conv[piece] is None else conv[piece] + part

    y = _dot(u_ref[...], wo_ref[...])
    conv_pieces_after(y)

    h = (x_ref[...] * (1.0 + sc1_ref[0, 0]) + sh1_ref[0, 0]).astype(BF16)
    val = _dot(h, win_ref[:, 0:D_MODEL])
    conv_pieces_after(val)
    gate = _dot(h, win_ref[:, D_MODEL:2 * D_MODEL])
    conv_pieces_after(gate)
    ext_next_ref[CF_HALO:CF_HALO + TM, :] = val * jax.nn.sigmoid(gate)

    x1 = _layer_norm(ALPHA * xres_ref[...] + (1.0 + gt1_ref[0, 0]) * y, lng_ref[0:1, :], lnb_ref[0:1, :])
    h2 = (x1 * (1.0 + sc2_ref[0, 0]) + sh2_ref[0, 0]).astype(BF16)
    chunk_cols = [slice(c * FF_TILE, (c + 1) * FF_TILE) for c in range(n_chunks)]
    d1 = _dot(h2, w1_ref[:, chunk_cols[0]])
    conv_pieces_after(d1)
    acc = None
    for c in range(n_chunks):
        d1_next = None
        if c + 1 < n_chunks:
            d1_next = _dot(h2, w1_ref[:, chunk_cols[c + 1]])
            conv_pieces_after(d1_next)
        a = jnp.maximum(d1, 0.0)
        p = _dot((a * a).astype(BF16), w2_ref[chunk_cols[c], :])
        conv_pieces_after(p)
        acc = p if acc is None else acc + p
        d1 = d1_next
    assert not todo
    o_ref[...] = _layer_norm(ALPHA * x1 + (1.0 + gt2_ref[0, 0]) * acc, lng_ref[1:2, :], lnb_ref[1:2, :])

    cu = _layer_norm(jnp.concatenate(conv, axis=0) + dwb_ref[...], cg_ref[...], cb_ref[...])
    u_next_ref[...] = _silu(cu).astype(BF16)

    u_ref[...] = u_next_ref[...]
    ext_ref[...] = ext_next_ref[...]


def _cf_layer(x2d, mods, layer, mixer, tiles_per_batch, w_in, dw_w, dw_b, cf_g, cf_b, w_out, ln_g, ln_b,
              w1, w2):
    rows = x2d.shape[0]
    n_tiles = rows // TM
    base = layer * N_MOD
    in_tile = lambda s: jnp.minimum(s, n_tiles - 1)
    mlp_tile = lambda s: jnp.maximum(s - 2, 0)

    def mod_spec(row, tile_of_step):
        return pl.BlockSpec((1, 1, 1, D_MODEL), lambda s: (row, tile_of_step(s) // tiles_per_batch, 0, 0))

    ext_rows = CF_HALO + TM + SUBLANES
    return pl.pallas_call(
        functools.partial(_cf_layer_kernel, n_tiles=n_tiles, tiles_per_batch=tiles_per_batch),
        grid=(n_tiles + 2,),
        in_specs=[
            pl.BlockSpec((TM, D_MODEL), lambda s: (in_tile(s), 0)),
            pl.BlockSpec((TM, D_MODEL), lambda s: (mlp_tile(s), 0)),
            mod_spec(base + 0, in_tile), mod_spec(base + 1, in_tile),
            mod_spec(base + 2, mlp_tile), mod_spec(base + 3, mlp_tile),
            mod_spec(base + 4, mlp_tile), mod_spec(base + 5, mlp_tile),
            _const_spec((D_MODEL, 2 * D_MODEL), mixer), _const_spec((CF_KERNEL, D_MODEL)),
            _const_spec((1, D_MODEL)), _const_spec((1, D_MODEL)), _const_spec((1, D_MODEL)),
            _const_spec((D_MODEL, D_MODEL), mixer), _const_spec((2, D_MODEL)), _const_spec((2, D_MODEL)),
            _const_spec((D_MODEL, FF_DIM), layer), _const_spec((FF_DIM, D_MODEL), layer),
        ],
        out_specs=pl.BlockSpec((TM, D_MODEL), lambda s: (mlp_tile(s), 0)),
        out_shape=jax.ShapeDtypeStruct((rows, D_MODEL), F32),
        scratch_shapes=[pltpu.VMEM((ext_rows, D_MODEL), F32),
                        pltpu.VMEM((ext_rows, D_MODEL), F32),
                        pltpu.VMEM((TM, D_MODEL), BF16),
                        pltpu.VMEM((TM, D_MODEL), BF16)],
        compiler_params=pltpu.CompilerParams(dimension_semantics=("arbitrary",),
                                             vmem_limit_bytes=VMEM_LIMIT),
        name="cf_layer",
    )(x2d, x2d, mods, mods, mods, mods, mods, mods, w_in, dw_w, dw_b, cf_g, cf_b, w_out, ln_g, ln_b, w1, w2)


def kernel(x, c, ada_w, ada_b, ln_g, ln_b, dn_w_in, dn_conv_w, dn_a_log, dn_dt_bias, dn_norm_w,
           dn_w_out, cf_w_in, cf_dw_w, cf_dw_b, cf_ln_g, cf_ln_b, cf_w_out, ff_w1, ff_w2):
    batch, seq, d = x.shape
    assert d == D_MODEL and seq % TM == 0 and TM % CHUNK == 0
    tiles_per_batch = seq // TM
    n_qkv = 3 * D_MODEL
    n_main = 4 * D_MODEL

    mods = _ada_mods(c, ada_w, ada_b)
    w1, w2 = _cast_stack(ff_w1), _cast_stack(ff_w2)
    dn_main, dn_out = _cast_stack(dn_w_in, cols=n_main), _cast_stack(dn_w_out)
    cf_in, cf_out = _cast_stack(cf_w_in), _cast_stack(cf_w_out)
    x2d = x.reshape(batch * seq, D_MODEL)
    for i in range(DEPTH):
        j = i // 2
        if i % 2 == 0:
            w_ba = jnp.pad(dn_w_in[j][:, n_main:], ((0, 0), (0, LANES - 2 * HEADS))).astype(BF16)
            gate_pad = (HEADS, LANES - 2 * HEADS)
            alog_vec = jnp.pad(dn_a_log[j], gate_pad).reshape(1, LANES)
            dtb_vec = jnp.pad(dn_dt_bias[j], gate_pad).reshape(1, LANES)
            q, k, v, z, bg = _dn_in(x2d, mods, i, j, tiles_per_batch, dn_main, w_ba,
                                    dn_conv_w[j][:, :n_qkv], alog_vec, dtb_vec)
            u = _dn_core(q, k, v, z, bg, dn_norm_w[j].reshape(1, HEAD_DIM), batch, tiles_per_batch)
            x2d = _post(x2d, u, mods, i, j, tiles_per_batch, dn_out, ln_g[i], ln_b[i], w1, w2)
        else:
            x2d = _cf_layer(x2d, mods, i, j, tiles_per_batch, cf_in, cf_dw_w[j],
                            cf_dw_b[j].reshape(1, D_MODEL), cf_ln_g[j].reshape(1, D_MODEL),
                            cf_ln_b[j].reshape(1, D_MODEL), cf_out, ln_g[i], ln_b[i], w1, w2)
    return x2d.reshape(batch, seq, D_MODEL)
```

```python
import functools

import jax
import jax.numpy as jnp
from jax import lax
from jax.experimental import pallas as pl
from jax.experimental.pallas import tpu as pltpu

F32 = jnp.float32
BF16 = jnp.bfloat16

D_MODEL = 1024
DEPTH = 4
N_MOD = 6
HEADS = 8
HEAD_DIM = 128
DN_CONV = 4
CHUNK = 64
BLOCK16 = 16
CF_KERNEL = 31
FF_DIM = 4 * D_MODEL
ALPHA = (2.0 * DEPTH) ** 0.25
LN_EPS = 1e-5
RMS_EPS = 1e-6
L2_EPS = 1e-6

LANES = 128
SUBLANES = 8
TM = 512
FF_TILE = 1024
CF_HALO = 32
CF_PIECE = 128
DN_HALO = 8
PASS1_CHUNKS = 4
VMEM_LIMIT = 56 * 1024 * 1024
CAST_BLOCK_ELEMS = 1024 * 1024


def _dot(a, b):
    return jnp.dot(a, b, preferred_element_type=F32)


def _dot_nt(a, b):
    return lax.dot_general(a, b, (((1,), (1,)), ((), ())), preferred_element_type=F32)


def _dot_tn(a, b):
    return lax.dot_general(a, b, (((0,), (0,)), ((), ())), preferred_element_type=F32)


def _bdot(a, b):
    return _dot(a.astype(BF16), b.astype(BF16))


def _split3(x):
    x1 = x.astype(BF16)
    r1 = x - x1.astype(F32)
    x2 = r1.astype(BF16)
    x3 = (r1 - x2.astype(F32)).astype(BF16)
    return x1, x2, x3


def _dot_exact_lhs(a_bf16, x):
    x1, x2, x3 = _split3(x)
    return _dot(a_bf16, x3) + _dot(a_bf16, x2) + _dot(a_bf16, x1)


def _layer_norm(r, g, b):
    mu = jnp.mean(r, axis=-1, keepdims=True)
    d = r - mu
    var = jnp.mean(d * d, axis=-1, keepdims=True)
    return d * lax.rsqrt(var + LN_EPS) * g + b


def _silu(x):
    return x * jax.nn.sigmoid(x)


def _softplus(x):
    return jnp.maximum(x, 0.0) + jnp.log1p(jnp.exp(-jnp.abs(x)))


def _zero_after(v):
    bits = lax.bitcast_convert_type(v, jnp.uint32)
    z = lax.shift_right_logical(lax.shift_right_logical(bits, jnp.uint32(16)), jnp.uint32(16))
    return z[0, 0].astype(jnp.int32)


def _const_spec(shape, layer=None):
    nd = len(shape)
    if layer is None:
        return pl.BlockSpec(shape, lambda *_: (0,) * nd, pipeline_mode=pl.Buffered(1))
    return pl.BlockSpec((None,) + tuple(shape), lambda *_: (layer,) + (0,) * nd,
                        pipeline_mode=pl.Buffered(1))


def _mod_spec(row, tiles_per_batch):
    return pl.BlockSpec((1, 1, 1, D_MODEL), lambda t: (row, t // tiles_per_batch, 0, 0))


def _ada_kernel(c_ref, w_ref, b_ref, o_ref):
    cond = _silu(c_ref[...])
    o_ref[0] = _dot(cond, w_ref[0]) + b_ref[0]


def _ada_mods(c, ada_w, ada_b):
    batch = c.shape[0]
    out = pl.pallas_call(
        _ada_kernel,
        grid=(DEPTH, N_MOD),
        in_specs=[
            pl.BlockSpec((batch, D_MODEL), lambda i, j: (0, 0)),
            pl.BlockSpec((1, D_MODEL, D_MODEL), lambda i, j: (i, 0, j)),
            pl.BlockSpec((1, 1, D_MODEL), lambda i, j: (i, 0, j)),
        ],
        out_specs=pl.BlockSpec((1, batch, D_MODEL), lambda i, j: (i * N_MOD + j, 0, 0)),
        out_shape=jax.ShapeDtypeStruct((DEPTH * N_MOD, batch, D_MODEL), F32),
        compiler_params=pltpu.CompilerParams(dimension_semantics=("arbitrary", "arbitrary")),
        name="ada_mods",
    )(c, ada_w, ada_b.reshape(DEPTH, 1, N_MOD * D_MODEL))
    return out.reshape(DEPTH * N_MOD, batch, 1, D_MODEL)


def _cast_kernel(w_ref, o_ref):
    o_ref[...] = w_ref[:, 0:o_ref.shape[-1]].astype(BF16)


def _cast_stack(w, cols=None):
    n_layers, rows, all_cols = w.shape
    cols = all_cols if cols is None else cols
    block_rows = rows
    while block_rows * all_cols > CAST_BLOCK_ELEMS:
        block_rows //= 2
    assert rows % block_rows == 0 and block_rows % SUBLANES == 0 and cols % LANES == 0
    return pl.pallas_call(
        _cast_kernel,
        grid=(n_layers, rows // block_rows),
        in_specs=[pl.BlockSpec((None, block_rows, all_cols), lambda l, i: (l, i, 0))],
        out_specs=pl.BlockSpec((None, block_rows, cols), lambda l, i: (l, i, 0)),
        out_shape=jax.ShapeDtypeStruct((n_layers, rows, cols), BF16),
        compiler_params=pltpu.CompilerParams(dimension_semantics=("arbitrary", "arbitrary")),
        name="cast_bf16",
    )(w)


def _post_kernel(x_ref, u_ref, gt1_ref, sh2_ref, sc2_ref, gt2_ref, wo_ref, lng_ref, lnb_ref,
                 w1_ref, w2_ref, o_ref):
    x = x_ref[...]
    y = _dot(u_ref[...], wo_ref[...])
    x1 = _layer_norm(ALPHA * x + (1.0 + gt1_ref[0, 0]) * y, lng_ref[0:1, :], lnb_ref[0:1, :])
    h = (x1 * (1.0 + sc2_ref[0, 0]) + sh2_ref[0, 0]).astype(BF16)
    n_chunks = FF_DIM // FF_TILE
    chunk_cols = [slice(c * FF_TILE, (c + 1) * FF_TILE) for c in range(n_chunks)]
    d1 = _dot(h, w1_ref[:, chunk_cols[0]])
    acc = None
    for c in range(n_chunks):
        d1_next = _dot(h, w1_ref[:, chunk_cols[c + 1]]) if c + 1 < n_chunks else None
        a = jnp.maximum(d1, 0.0)
        p = _dot((a * a).astype(BF16), w2_ref[chunk_cols[c], :])
        acc = p if acc is None else acc + p
        d1 = d1_next
    o_ref[...] = _layer_norm(ALPHA * x1 + (1.0 + gt2_ref[0, 0]) * acc, lng_ref[1:2, :], lnb_ref[1:2, :])


def _post(x2d, u2d, mods, layer, mixer, tiles_per_batch, w_out, ln_g, ln_b, w1, w2):
    rows = x2d.shape[0]
    row_spec = pl.BlockSpec((TM, D_MODEL), lambda t: (t, 0))
    base = layer * N_MOD
    return pl.pallas_call(
        _post_kernel,
        grid=(rows // TM,),
        in_specs=[
            row_spec, row_spec,
            _mod_spec(base + 2, tiles_per_batch), _mod_spec(base + 3, tiles_per_batch),
            _mod_spec(base + 4, tiles_per_batch), _mod_spec(base + 5, tiles_per_batch),
            _const_spec((D_MODEL, D_MODEL), mixer), _const_spec((2, D_MODEL)), _const_spec((2, D_MODEL)),
            _const_spec((D_MODEL, FF_DIM), layer), _const_spec((FF_DIM, D_MODEL), layer),
        ],
        out_specs=row_spec,
        out_shape=jax.ShapeDtypeStruct((rows, D_MODEL), F32),
        compiler_params=pltpu.CompilerParams(dimension_semantics=("arbitrary",),
                                             vmem_limit_bytes=VMEM_LIMIT),
        name="post_mlp",
    )(x2d, u2d, mods, mods, mods, mods, w_out, ln_g, ln_b, w1, w2)


def _dn_in_kernel(x_ref, sh_ref, sc_ref, w_ref, wba_ref, cw_ref, alog_ref, dtb_ref,
                  q_ref, k_ref, v_ref, z_ref, bg_ref, ext_ref, *, tiles_per_batch):
    t = pl.program_id(0)

    @pl.when(t % tiles_per_batch == 0)
    def _():
        ext_ref[0:DN_HALO, :] = jnp.zeros((DN_HALO, 3 * D_MODEL), F32)

    h = (x_ref[...] * (1.0 + sc_ref[0, 0]) + sh_ref[0, 0]).astype(BF16)

    for grp, out_ref in enumerate((q_ref, k_ref, v_ref)):
        cols = slice(grp * D_MODEL, (grp + 1) * D_MODEL)
        ext_ref[DN_HALO:DN_HALO + TM, cols] = _dot(h, w_ref[:, cols])
        ext = ext_ref[:, cols]
        acc = None
        for j in range(DN_CONV):
            off = DN_HALO - (DN_CONV - 1) + j
            res = off % SUBLANES
            shifted = ext if res == 0 else pltpu.roll(ext, DN_HALO + TM - res, axis=0)
            term = cw_ref[j:j + 1, cols] * shifted[off - res:off - res + TM, :]
            acc = term if acc is None else acc + term
        ext_ref[0:DN_HALO, cols] = ext_ref[TM:TM + DN_HALO, cols]
        y = _silu(acc)
        if out_ref is v_ref:
            out_ref[...] = y
        else:
            for hd in range(HEADS):
                lanes = slice(hd * HEAD_DIM, (hd + 1) * HEAD_DIM)
                ys = y[:, lanes]
                ss = jnp.sum(ys * ys, axis=-1, keepdims=True)
                out_ref[:, lanes] = ys * lax.rsqrt(ss + L2_EPS)

    z_ref[...] = _dot(h, w_ref[:, 3 * D_MODEL:4 * D_MODEL])

    ba = _dot(h, wba_ref[...])
    beta = jax.nn.sigmoid(ba)
    g = -jnp.exp(alog_ref[...]) * _softplus(ba + dtb_ref[...])
    lane = lax.broadcasted_iota(jnp.int32, ba.shape, 1)
    bg_ref[...] = jnp.where(lane < HEADS, beta, g)


def _dn_in(x2d, mods, layer, mixer, tiles_per_batch, w_main, w_ba, conv_w, alog_vec, dtb_vec):
    rows = x2d.shape[0]
    row_spec = pl.BlockSpec((TM, D_MODEL), lambda t: (t, 0))
    base = layer * N_MOD
    out_sds = jax.ShapeDtypeStruct((rows, D_MODEL), F32)
    return pl.pallas_call(
        functools.partial(_dn_in_kernel, tiles_per_batch=tiles_per_batch),
        grid=(rows // TM,),
        in_specs=[
            row_spec, _mod_spec(base + 0, tiles_per_batch), _mod_spec(base + 1, tiles_per_batch),
            _const_spec((D_MODEL, 4 * D_MODEL), mixer), _const_spec((D_MODEL, LANES)),
            _const_spec((DN_CONV, 3 * D_MODEL)), _const_spec((1, LANES)), _const_spec((1, LANES)),
        ],
        out_specs=[row_spec, row_spec, row_spec, row_spec, pl.BlockSpec((TM, LANES), lambda t: (t, 0))],
        out_shape=[out_sds, out_sds, out_sds, out_sds, jax.ShapeDtypeStruct((rows, LANES), F32)],
        scratch_shapes=[pltpu.VMEM((DN_HALO + TM, 3 * D_MODEL), F32)],
        compiler_params=pltpu.CompilerParams(dimension_semantics=("arbitrary",),
                                             vmem_limit_bytes=VMEM_LIMIT),
        name="dn_in",
    )(x2d, mods, mods, w_main, w_ba, conv_w, alog_vec, dtb_vec)


def _inverse_minus_eye_many(a_list):
    ri = lax.broadcasted_iota(jnp.int32, (CHUNK, CHUNK), 0) // BLOCK16
    ci = lax.broadcasted_iota(jnp.int32, (CHUNK, CHUNK), 1) // BLOCK16
    on_diag = ri == ci
    b16 = lambda xs: [x.astype(BF16) for x in xs]
    mm = lambda xs, ys: [_dot(x, y) for x, y in zip(xs, ys)]

    d = [jnp.where(on_diag, a, 0.0) for a in a_list]
    low = [a - x for a, x in zip(a_list, d)]
    d_16 = b16(d)
    d2 = mm(d_16, d_16)
    d2_16 = b16(d2)
    d4 = mm(d2_16, d2_16)
    dd2 = mm(d_16, d2_16)
    p = [x2 - x - y for x, x2, y in zip(d, d2, dd2)]
    d4_16 = b16(d4)
    d8 = mm(d4_16, d4_16)
    pd4 = mm(b16(p), d4_16)
    p = [x + y + z for x, y, z in zip(p, d4, pd4)]
    pd8 = mm(b16(p), b16(d8))
    p = [x + y + z for x, y, z in zip(p, d8, pd8)]
    p_16 = b16(p)
    pl_ = mm(p_16, b16(low))
    m = [x + y for x, y in zip(low, pl_)]
    m_16 = b16(m)
    m2 = mm(m_16, m_16)
    mm2 = mm(m_16, b16(m2))
    q = [x2 - x - y for x, x2, y in zip(m, m2, mm2)]
    qp = mm(b16(q), p_16)
    return [x + y + z for x, y, z in zip(q, p, qp)]


def _dn_core_kernel(q_ref, k_ref, v_ref, z_ref, bg_ref, nw_ref, o_ref,
                    state_ref, gam_ref, egam_ref, edec_ref, egl_ref,
                    u_ref, w_ref, qd_ref, kdt_ref, a_ref):
    @pl.when(pl.program_id(1) == 0)
    def _():
        state_ref[...] = jnp.zeros(state_ref.shape, F32)

    ri = lax.broadcasted_iota(jnp.int32, (TM, TM), 0)
    ci = lax.broadcasted_iota(jnp.int32, (TM, TM), 1)
    same_chunk = (ri // CHUNK) == (ci // CHUNK)
    cum_mat = jnp.where(same_chunk & (ci <= ri), 1.0, 0.0).astype(BF16)
    tot_mat = jnp.where(same_chunk, 1.0, 0.0).astype(BF16)
    bg = bg_ref[...]
    gam = _dot_exact_lhs(cum_mat, bg)
    gtot = _dot_exact_lhs(tot_mat, bg)
    gam_ref[...] = gam
    egam_ref[...] = jnp.exp(gam)
    edec_ref[...] = jnp.exp(gtot - gam)
    egl_ref[...] = jnp.exp(gtot)

    r_idx = lax.broadcasted_iota(jnp.int32, (CHUNK, CHUNK), 0)
    c_idx = lax.broadcasted_iota(jnp.int32, (CHUNK, CHUNK), 1)
    causal = c_idx <= r_idx
    strict = c_idx < r_idx
    scale = HEAD_DIM ** -0.5
    wide = (CHUNK, HEAD_DIM)

    def independent_part(grp, carry):
        probs = []
        for ci_ in range(PASS1_CHUNKS):
            cidx = grp * PASS1_CHUNKS + ci_
            rows = pl.ds(pl.multiple_of(cidx * CHUNK, CHUNK), CHUNK)
            gam_c = gam_ref[rows, :]
            gam_t = gam_c.T
            bg_c = bg_ref[rows, :]
            egam_c = egam_ref[rows, :]
            edec_c = edec_ref[rows, :]
            for hd in range(HEADS):
                lanes = slice(hd * HEAD_DIM, (hd + 1) * HEAD_DIM)
                gl = HEADS + hd
                beta = jnp.broadcast_to(bg_c[:, hd:hd + 1], wide)
                eg = jnp.broadcast_to(egam_c[:, gl:gl + 1], wide)
                ed = jnp.broadcast_to(edec_c[:, gl:gl + 1], wide)
                gcol = jnp.broadcast_to(gam_c[:, gl:gl + 1], wide)[:, :CHUNK]
                decay = jnp.exp(jnp.where(causal, gcol - gam_t[gl:gl + 1, :], -jnp.inf))
                probs.append(dict(cidx=cidx, rows=rows, hd=hd, lanes=lanes, beta=beta, eg=eg, ed=ed,
                                  decay=decay, q=q_ref[rows, lanes], k=k_ref[rows, lanes],
                                  v=v_ref[rows, lanes]))
        for p in probs:
            p["k16"] = p["k"].astype(BF16)
            p["kq16"] = jnp.concatenate([p["k16"], (p["q"] * scale).astype(BF16)], axis=0)
        kkqk = [_dot_nt(p["kq16"], p["k16"]) for p in probs]
        a_list = []
        for p, s in zip(probs, kkqk):
            a_list.append(jnp.where(strict, s[:CHUNK] * p["beta"][:, :CHUNK] * p["decay"], 0.0))
            a_ref[p["hd"], p["rows"], :] = (s[CHUNK:] * p["decay"]).astype(BF16)
        tinv = _inverse_minus_eye_many(a_list)
        for p in probs:
            p["vb"] = p["v"] * p["beta"]
            p["kbg"] = p["k"] * p["beta"] * p["eg"]
        uw = [_dot(t.astype(BF16), jnp.concatenate([p["vb"], p["kbg"]], axis=1).astype(BF16))
              for p, t in zip(probs, tinv)]
        for p, r in zip(probs, uw):
            rows, lanes = p["rows"], p["lanes"]
            u_ref[rows, lanes] = p["vb"] + r[:, :HEAD_DIM]
            w_ref[rows, lanes] = (p["kbg"] + r[:, HEAD_DIM:]).astype(BF16)
            qd_ref[rows, lanes] = (p["q"] * (p["eg"] * scale)).astype(BF16)
            kdt_ref[p["hd"], p["cidx"]] = (p["k"] * p["ed"]).T.astype(BF16)
        return carry

    lax.fori_loop(0, TM // CHUNK // PASS1_CHUNKS, independent_part, 0)

    nw = nw_ref[...]

    def recurrent_part(c, carry):
        rows = pl.ds(pl.multiple_of(c * CHUNK, CHUNK), CHUNK)
        egl_c = egl_ref[rows, :]
        heads = range(HEADS)
        lanes = [slice(hd * HEAD_DIM, (hd + 1) * HEAD_DIM) for hd in heads]
        state = [state_ref[hd] for hd in heads]
        wq = [jnp.concatenate([w_ref[rows, lanes[hd]], qd_ref[rows, lanes[hd]]], axis=0) for hd in heads]
        r = [_dot(wq[hd], state[hd].astype(BF16)) for hd in heads]
        vn16 = [(u_ref[rows, lanes[hd]] - r[hd][:CHUNK]).astype(BF16) for hd in heads]
        av = [_dot(a_ref[hd, rows, :], vn16[hd]) for hd in heads]
        kv = [_dot(kdt_ref[hd, c], vn16[hd]) for hd in heads]
        for hd in heads:
            state_ref[hd] = state[hd] * egl_c[0:1, HEADS + hd:HEADS + hd + 1] + kv[hd]
            o = r[hd][CHUNK:] + av[hd]
            on = o * lax.rsqrt(jnp.mean(o * o, axis=-1, keepdims=True) + RMS_EPS) * nw
            o_ref[rows, lanes[hd]] = (on * _silu(z_ref[rows, lanes[hd]])).astype(BF16)
        return carry

    lax.fori_loop(0, TM // CHUNK, recurrent_part, 0)


def _dn_core(q, k, v, z, bg, norm_w, batch, tiles_per_batch):
    rows = q.shape[0]
    row_spec = pl.BlockSpec((TM, D_MODEL), lambda b, s: (b * tiles_per_batch + s, 0))
    small = pltpu.VMEM((TM, LANES), F32)
    return pl.pallas_call(
        _dn_core_kernel,
        grid=(batch, tiles_per_batch),
        in_specs=[row_spec, row_spec, row_spec, row_spec,
                  pl.BlockSpec((TM, LANES), lambda b, s: (b * tiles_per_batch + s, 0)),
                  pl.BlockSpec((1, HEAD_DIM), lambda b, s: (0, 0))],
        out_specs=row_spec,
        out_shape=jax.ShapeDtypeStruct((rows, D_MODEL), BF16),
        scratch_shapes=[
            pltpu.VMEM((HEADS, HEAD_DIM, HEAD_DIM), F32),
            small, small, small, small,
            pltpu.VMEM((TM, D_MODEL), F32),
            pltpu.VMEM((TM, D_MODEL), BF16),
            pltpu.VMEM((TM, D_MODEL), BF16),
            pltpu.VMEM((HEADS, TM // CHUNK, HEAD_DIM, CHUNK), BF16),
            pltpu.VMEM((HEADS, TM, CHUNK), BF16),
        ],
        compiler_params=pltpu.CompilerParams(dimension_semantics=("arbitrary", "arbitrary"),
                                             vmem_limit_bytes=VMEM_LIMIT),
        name="dn_core",
    )(q, k, v, z, bg, norm_w)


def _cf_conv_piece(ext_ref, dw_ref, res, piece, not_before):
    n = CF_PIECE + CF_HALO + SUBLANES
    start = pl.multiple_of(piece * CF_PIECE + _zero_after(not_before), SUBLANES)
    window = ext_ref[pl.ds(start, n), :]
    shifted = window if res == 0 else pltpu.roll(window, n - res, axis=0)
    acc = None
    for j in range(CF_KERNEL):
        off = CF_HALO - (CF_KERNEL - 1) + j
        if off % SUBLANES != res:
            continue
        term = dw_ref[j:j + 1, :] * shifted[off - res:off - res + CF_PIECE, :]
        acc = term if acc is None else acc + term
    return acc


def _cf_layer_kernel(x_ref, xres_ref, sh1_ref, sc1_ref, gt1_ref, sh2_ref, sc2_ref, gt2_ref,
                     win_ref, dw_ref, dwb_ref, cg_ref, cb_ref, wo_ref, lng_ref, lnb_ref, w1_ref, w2_ref,
                     o_ref, ext_ref, ext_next_ref, u_ref, u_next_ref, *, n_tiles, tiles_per_batch):
    s = pl.program_id(0)
    t_in = jnp.minimum(s, n_tiles - 1)
    first_of_batch = t_in % tiles_per_batch == 0

    @pl.when(s == 0)
    def _():
        ext_ref[...] = jnp.zeros(ext_ref.shape, F32)
        u_ref[...] = jnp.zeros(u_ref.shape, BF16)
        ext_next_ref[CF_HALO + TM:CF_HALO + TM + SUBLANES, :] = jnp.zeros((SUBLANES, D_MODEL), F32)

    @pl.when(first_of_batch)
    def _():
        ext_next_ref[0:CF_HALO, :] = jnp.zeros((CF_HALO, D_MODEL), F32)

    @pl.when(jnp.logical_not(first_of_batch))
    def _():
        ext_next_ref[0:CF_HALO, :] = ext_ref[TM:TM + CF_HALO, :]

    n_chunks = FF_DIM // FF_TILE
    n_pieces = TM // CF_PIECE
    conv = [None] * n_pieces
    todo = [(res, piece) for res in range(SUBLANES) for piece in range(n_pieces)]
    slots_seen = [0]

    def conv_pieces_after(result):
        for blk in range(n_pieces):
            slots_seen[0] += 1
            if slots_seen[0] % 5 == 0 or not todo:
                continue
            res, piece = todo.pop(0)
            r0 = blk * CF_PIECE
            part = _cf_conv_piece(ext_ref, dw_ref, res, piece, result[r0:r0 + 1, 0:1])
            conv[piece] = part if conv[piece] is None else conv[piece] + part

    y = _dot(u_ref[...], wo_ref[...])
    conv_pieces_after(y)

    h = (x_ref[...] * (1.0 + sc1_ref[0, 0]) + sh1_ref[0, 0]).astype(BF16)
    val = _dot(h, win_ref[:, 0:D_MODEL])
    conv_pieces_after(val)
    gate = _dot(h, win_ref[:, D_MODEL:2 * D_MODEL])
    conv_pieces_after(gate)
    ext_next_ref[CF_HALO:CF_HALO + TM, :] = val * jax.nn.sigmoid(gate)

    x1 = _layer_norm(ALPHA * xres_ref[...] + (1.0 + gt1_ref[0, 0]) * y, lng_ref[0:1, :], lnb_ref[0:1, :])
    h2 = (x1 * (1.0 + sc2_ref[0, 0]) + sh2_ref[0, 0]).astype(BF16)
    chunk_cols = [slice(c * FF_TILE, (c + 1) * FF_TILE) for c in range(n_chunks)]
    d1 = _dot(h2, w1_ref[:, chunk_cols[0]])
    conv_pieces_after(d1)
    acc = None
    for c in range(n_chunks):
        d1_next = None
        if c + 1 < n_chunks:
            d1_next = _dot(h2, w1_ref[:, chunk_cols[c + 1]])
            conv_pieces_after(d1_next)
        a = jnp.maximum(d1, 0.0)
        p = _dot((a * a).astype(BF16), w2_ref[chunk_cols[c], :])
        conv_pieces_after(p)
        acc = p if acc is None else acc + p
        d1 = d1_next
    assert not todo
    o_ref[...] = _layer_norm(ALPHA * x1 + (1.0 + gt2_ref[0, 0]) * acc, lng_ref[1:2, :], lnb_ref[1:2, :])

    cu = _layer_norm(jnp.concatenate(conv, axis=0) + dwb_ref[...], cg_ref[...], cb_ref[...])
    u_next_ref[...] = _silu(cu).astype(BF16)

    u_ref[...] = u_next_ref[...]
    ext_ref[...] = ext_next_ref[...]


def _cf_layer(x2d, mods, layer, mixer, tiles_per_batch, w_in, dw_w, dw_b, cf_g, cf_b, w_out, ln_g, ln_b,
              w1, w2):
    rows = x2d.shape[0]
    n_tiles = rows // TM
    base = layer * N_MOD
    in_tile = lambda s: jnp.minimum(s, n_tiles - 1)
    mlp_tile = lambda s: jnp.maximum(s - 2, 0)

    def mod_spec(row, tile_of_step):
        return pl.BlockSpec((1, 1, 1, D_MODEL), lambda s: (row, tile_of_step(s) // tiles_per_batch, 0, 0))

    ext_rows = CF_HALO + TM + SUBLANES
    return pl.pallas_call(
        functools.partial(_cf_layer_kernel, n_tiles=n_tiles, tiles_per_batch=tiles_per_batch),
        grid=(n_tiles + 2,),
        in_specs=[
            pl.BlockSpec((TM, D_MODEL), lambda s: (in_tile(s), 0)),
            pl.BlockSpec((TM, D_MODEL), lambda s: (mlp_tile(s), 0)),
            mod_spec(base + 0, in_tile), mod_spec(base + 1, in_tile),
            mod_spec(base + 2, mlp_tile), mod_spec(base + 3, mlp_tile),
            mod_spec(base + 4, mlp_tile), mod_spec(base + 5, mlp_tile),
            _const_spec((D_MODEL, 2 * D_MODEL), mixer), _const_spec((CF_KERNEL, D_MODEL)),
            _const_spec((1, D_MODEL)), _const_spec((1, D_MODEL)), _const_spec((1, D_MODEL)),
            _const_spec((D_MODEL, D_MODEL), mixer), _const_spec((2, D_MODEL)), _const_spec((2, D_MODEL)),
            _const_spec((D_MODEL, FF_DIM), layer), _const_spec((FF_DIM, D_MODEL), layer),
        ],
        out_specs=pl.BlockSpec((TM, D_MODEL), lambda s: (mlp_tile(s), 0)),
        out_shape=jax.ShapeDtypeStruct((rows, D_MODEL), F32),
        scratch_shapes=[pltpu.VMEM((ext_rows, D_MODEL), F32),
                        pltpu.VMEM((ext_rows, D_MODEL), F32),
                        pltpu.VMEM((TM, D_MODEL), BF16),
                        pltpu.VMEM((TM, D_MODEL), BF16)],
        compiler_params=pltpu.CompilerParams(dimension_semantics=("arbitrary",),
                                             vmem_limit_bytes=VMEM_LIMIT),
        name="cf_layer",
    )(x2d, x2d, mods, mods, mods, mods, mods, mods, w_in, dw_w, dw_b, cf_g, cf_b, w_out, ln_g, ln_b, w1, w2)


def kernel(x, c, ada_w, ada_b, ln_g, ln_b, dn_w_in, dn_conv_w, dn_a_log, dn_dt_bias, dn_norm_w,
           dn_w_out, cf_w_in, cf_dw_w, cf_dw_b, cf_ln_g, cf_ln_b, cf_w_out, ff_w1, ff_w2):
    batch, seq, d = x.shape
    assert d == D_MODEL and seq % TM == 0 and TM % CHUNK == 0
    tiles_per_batch = seq // TM
    n_qkv = 3 * D_MODEL
    n_main = 4 * D_MODEL

    mods = _ada_mods(c, ada_w, ada_b)
    w1, w2 = _cast_stack(ff_w1), _cast_stack(ff_w2)
    dn_main, dn_out = _cast_stack(dn_w_in, cols=n_main), _cast_stack(dn_w_out)
    cf_in, cf_out = _cast_stack(cf_w_in), _cast_stack(cf_w_out)
    x2d = x.reshape(batch * seq, D_MODEL)
    for i in range(DEPTH):
        j = i // 2
        if i % 2 == 0:
            w_ba = jnp.pad(dn_w_in[j][:, n_main:], ((0, 0), (0, LANES - 2 * HEADS))).astype(BF16)
            gate_pad = (HEADS, LANES - 2 * HEADS)
            alog_vec = jnp.pad(dn_a_log[j], gate_pad).reshape(1, LANES)
            dtb_vec = jnp.pad(dn_dt_bias[j], gate_pad).reshape(1, LANES)
            q, k, v, z, bg = _dn_in(x2d, mods, i, j, tiles_per_batch, dn_main, w_ba,
                                    dn_conv_w[j][:, :n_qkv], alog_vec, dtb_vec)
            u = _dn_core(q, k, v, z, bg, dn_norm_w[j].reshape(1, HEAD_DIM), batch, tiles_per_batch)
            x2d = _post(x2d, u, mods, i, j, tiles_per_batch, dn_out, ln_g[i], ln_b[i], w1, w2)
        else:
            x2d = _cf_layer(x2d, mods, i, j, tiles_per_batch, cf_in, cf_dw_w[j],
                            cf_dw_b[j].reshape(1, D_MODEL), cf_ln_g[j].reshape(1, D_MODEL),
                            cf_ln_b[j].reshape(1, D_MODEL), cf_out, ln_g[i], ln_b[i], w1, w2)
    return x2d.reshape(batch, seq, D_MODEL)
```

```python
import functools

import jax
import jax.numpy as jnp
from jax import lax
from jax.experimental import pallas as pl
from jax.experimental.pallas import tpu as pltpu

F32 = jnp.float32
BF16 = jnp.bfloat16

D_MODEL = 1024
DEPTH = 4
N_MOD = 6
HEADS = 8
HEAD_DIM = 128
DN_CONV = 4
CHUNK = 64
BLOCK16 = 16
CF_KERNEL = 31
FF_DIM = 4 * D_MODEL
ALPHA = (2.0 * DEPTH) ** 0.25
LN_EPS = 1e-5
RMS_EPS = 1e-6
L2_EPS = 1e-6

LANES = 128
SUBLANES = 8
TM = 512
FF_TILE = 1024
CF_HALO = 32
CF_PIECE = 128
DN_HALO = 8
PASS1_CHUNKS = 4
VMEM_LIMIT = 56 * 1024 * 1024
CAST_BLOCK_ELEMS = 1024 * 1024


def _dot(a, b):
    return jnp.dot(a, b, preferred_element_type=F32)


def _dot_nt(a, b):
    return lax.dot_general(a, b, (((1,), (1,)), ((), ())), preferred_element_type=F32)


def _dot_tn(a, b):
    return lax.dot_general(a, b, (((0,), (0,)), ((), ())), preferred_element_type=F32)


def _bdot(a, b):
    return _dot(a.astype(BF16), b.astype(BF16))


def _split3(x):
    x1 = x.astype(BF16)
    r1 = x - x1.astype(F32)
    x2 = r1.astype(BF16)
    x3 = (r1 - x2.astype(F32)).astype(BF16)
    return x1, x2, x3


def _dot_exact_lhs(a_bf16, x):
    x1, x2, x3 = _split3(x)
    return _dot(a_bf16, x3) + _dot(a_bf16, x2) + _dot(a_bf16, x1)


def _layer_norm(r, g, b):
    mu = jnp.mean(r, axis=-1, keepdims=True)
    d = r - mu
    var = jnp.mean(d * d, axis=-1, keepdims=True)
    return d * lax.rsqrt(var + LN_EPS) * g + b


def _silu(x):
    return x * jax.nn.sigmoid(x)


def _softplus(x):
    return jnp.maximum(x, 0.0) + jnp.log1p(jnp.exp(-jnp.abs(x)))


def _zero_after(v):
    bits = lax.bitcast_convert_type(v, jnp.uint32)
    z = lax.shift_right_logical(lax.shift_right_logical(bits, jnp.uint32(16)), jnp.uint32(16))
    return z[0, 0].astype(jnp.int32)


def _const_spec(shape, layer=None):
    nd = len(shape)
    if layer is None:
        return pl.BlockSpec(shape, lambda *_: (0,) * nd, pipeline_mode=pl.Buffered(1))
    return pl.BlockSpec((None,) + tuple(shape), lambda *_: (layer,) + (0,) * nd,
                        pipeline_mode=pl.Buffered(1))


def _mod_spec(row, tiles_per_batch):
    return pl.BlockSpec((1, 1, 1, D_MODEL), lambda t: (row, t // tiles_per_batch, 0, 0))


def _ada_kernel(c_ref, w_ref, b_ref, o_ref):
    cond = _silu(c_ref[...])
    o_ref[0] = _dot(cond, w_ref[0]) + b_ref[0]


def _ada_mods(c, ada_w, ada_b):
    batch = c.shape[0]
    out = pl.pallas_call(
        _ada_kernel,
        grid=(DEPTH, N_MOD),
        in_specs=[
            pl.BlockSpec((batch, D_MODEL), lambda i, j: (0, 0)),
            pl.BlockSpec((1, D_MODEL, D_MODEL), lambda i, j: (i, 0, j)),
            pl.BlockSpec((1, 1, D_MODEL), lambda i, j: (i, 0, j)),
        ],
        out_specs=pl.BlockSpec((1, batch, D_MODEL), lambda i, j: (i * N_MOD + j, 0, 0)),
        out_shape=jax.ShapeDtypeStruct((DEPTH * N_MOD, batch, D_MODEL), F32),
        compiler_params=pltpu.CompilerParams(dimension_semantics=("arbitrary", "arbitrary")),
        name="ada_mods",
    )(c, ada_w, ada_b.reshape(DEPTH, 1, N_MOD * D_MODEL))
    return out.reshape(DEPTH * N_MOD, batch, 1, D_MODEL)


def _cast_kernel(w_ref, o_ref):
    o_ref[...] = w_ref[:, 0:o_ref.shape[-1]].astype(BF16)


def _cast_stack(w, cols=None):
    n_layers, rows, all_cols = w.shape
    cols = all_cols if cols is None else cols
    block_rows = rows
    while block_rows * all_cols > CAST_BLOCK_ELEMS:
        block_rows //= 2
    assert rows % block_rows == 0 and block_rows % SUBLANES == 0 and cols % LANES == 0
    return pl.pallas_call(
        _cast_kernel,
        grid=(n_layers, rows // block_rows),
        in_specs=[pl.BlockSpec((None, block_rows, all_cols), lambda l, i: (l, i, 0))],
        out_specs=pl.BlockSpec((None, block_rows, cols), lambda l, i: (l, i, 0)),
        out_shape=jax.ShapeDtypeStruct((n_layers, rows, cols), BF16),
        compiler_params=pltpu.CompilerParams(dimension_semantics=("arbitrary", "arbitrary")),
        name="cast_bf16",
    )(w)


def _post_kernel(x_ref, u_ref, gt1_ref, sh2_ref, sc2_ref, gt2_ref, wo_ref, lng_ref, lnb_ref,
                 w1_ref, w2_ref, o_ref):
    x = x_ref[...]
    y = _dot(u_ref[...], wo_ref[...])
    x1 = _layer_norm(ALPHA * x + (1.0 + gt1_ref[0, 0]) * y, lng_ref[0:1, :], lnb_ref[0:1, :])
    h = (x1 * (1.0 + sc2_ref[0, 0]) + sh2_ref[0, 0]).astype(BF16)
    n_chunks = FF_DIM // FF_TILE
    chunk_cols = [slice(c * FF_TILE, (c + 1) * FF_TILE) for c in range(n_chunks)]
    d1 = _dot(h, w1_ref[:, chunk_cols[0]])
    acc = None
    for c in range(n_chunks):
        d1_next = _dot(h, w1_ref[:, chunk_cols[c + 1]]) if c + 1 < n_chunks else None
        a = jnp.maximum(d1, 0.0)
        p = _dot((a * a).astype(BF16), w2_ref[chunk_cols[c], :])
        acc = p if acc is None else acc + p
        d1 = d1_next
    o_ref[...] = _layer_norm(ALPHA * x1 + (1.0 + gt2_ref[0, 0]) * acc, lng_ref[1:2, :], lnb_ref[1:2, :])


def _post(x2d, u2d, mods, layer, mixer, tiles_per_batch, w_out, ln_g, ln_b, w1, w2):
    rows = x2d.shape[0]
    row_spec = pl.BlockSpec((TM, D_MODEL), lambda t: (t, 0))
    base = layer * N_MOD
    return pl.pallas_call(
        _post_kernel,
        grid=(rows // TM,),
        in_specs=[
            row_spec, row_spec,
            _mod_spec(base + 2, tiles_per_batch), _mod_spec(base + 3, tiles_per_batch),
            _mod_spec(base + 4, tiles_per_batch), _mod_spec(base + 5, tiles_per_batch),
            _const_spec((D_MODEL, D_MODEL), mixer), _const_spec((2, D_MODEL)), _const_spec((2, D_MODEL)),
            _const_spec((D_MODEL, FF_DIM), layer), _const_spec((FF_DIM, D_MODEL), layer),
        ],
        out_specs=row_spec,
        out_shape=jax.ShapeDtypeStruct((rows, D_MODEL), F32),
        compiler_params=pltpu.CompilerParams(dimension_semantics=("arbitrary",),
                                             vmem_limit_bytes=VMEM_LIMIT),
        name="post_mlp",
    )(x2d, u2d, mods, mods, mods, mods, w_out, ln_g, ln_b, w1, w2)


def _dn_in_kernel(x_ref, sh_ref, sc_ref, w_ref, wba_ref, cw_ref, alog_ref, dtb_ref,
                  q_ref, k_ref, v_ref, z_ref, bg_ref, ext_ref, *, tiles_per_batch):
    t = pl.program_id(0)

    @pl.when(t % tiles_per_batch == 0)
    def _():
        ext_ref[0:DN_HALO, :] = jnp.zeros((DN_HALO, 3 * D_MODEL), F32)

    h = (x_ref[...] * (1.0 + sc_ref[0, 0]) + sh_ref[0, 0]).astype(BF16)

    for grp, out_ref in enumerate((q_ref, k_ref, v_ref)):
        cols = slice(grp * D_MODEL, (grp + 1) * D_MODEL)
        ext_ref[DN_HALO:DN_HALO + TM, cols] = _dot(h, w_ref[:, cols])
        ext = ext_ref[:, cols]
        acc = None
        for j in range(DN_CONV):
            off = DN_HALO - (DN_CONV - 1) + j
            res = off % SUBLANES
            shifted = ext if res == 0 else pltpu.roll(ext, DN_HALO + TM - res, axis=0)
            term = cw_ref[j:j + 1, cols] * shifted[off - res:off - res + TM, :]
            acc = term if acc is None else acc + term
        ext_ref[0:DN_HALO, cols] = ext_ref[TM:TM + DN_HALO, cols]
        y = _silu(acc)
        if out_ref is v_ref:
            out_ref[...] = y
        else:
            for hd in range(HEADS):
                lanes = slice(hd * HEAD_DIM, (hd + 1) * HEAD_DIM)
                ys = y[:, lanes]
                ss = jnp.sum(ys * ys, axis=-1, keepdims=True)
                out_ref[:, lanes] = ys * lax.rsqrt(ss + L2_EPS)

    z_ref[...] = _dot(h, w_ref[:, 3 * D_MODEL:4 * D_MODEL])

    ba = _dot(h, wba_ref[...].astype(BF16))
    beta = jax.nn.sigmoid(ba)
    g = -jnp.exp(alog_ref[...]) * _softplus(ba + dtb_ref[...])
    lane = lax.broadcasted_iota(jnp.int32, ba.shape, 1)
    bg_ref[...] = jnp.where(lane < HEADS, beta, g)


def _dn_in(x2d, mods, layer, mixer, tiles_per_batch, w_main, w_ba, conv_w, alog_vec, dtb_vec):
    rows = x2d.shape[0]
    row_spec = pl.BlockSpec((TM, D_MODEL), lambda t: (t, 0))
    base = layer * N_MOD
    out_sds = jax.ShapeDtypeStruct((rows, D_MODEL), F32)
    return pl.pallas_call(
        functools.partial(_dn_in_kernel, tiles_per_batch=tiles_per_batch),
        grid=(rows // TM,),
        in_specs=[
            row_spec, _mod_spec(base + 0, tiles_per_batch), _mod_spec(base + 1, tiles_per_batch),
            _const_spec((D_MODEL, 4 * D_MODEL), mixer), _const_spec((D_MODEL, LANES)),
            _const_spec((DN_CONV, 3 * D_MODEL)), _const_spec((1, LANES)), _const_spec((1, LANES)),
        ],
        out_specs=[row_spec, row_spec, row_spec, row_spec, pl.BlockSpec((TM, LANES), lambda t: (t, 0))],
        out_shape=[out_sds, out_sds, out_sds, out_sds, jax.ShapeDtypeStruct((rows, LANES), F32)],
        scratch_shapes=[pltpu.VMEM((DN_HALO + TM, 3 * D_MODEL), F32)],
        compiler_params=pltpu.CompilerParams(dimension_semantics=("arbitrary",),
                                             vmem_limit_bytes=VMEM_LIMIT),
        name="dn_in",
    )(x2d, mods, mods, w_main, w_ba, conv_w, alog_vec, dtb_vec)


def _inverse_minus_eye_many(a_list):
    ri = lax.broadcasted_iota(jnp.int32, (CHUNK, CHUNK), 0) // BLOCK16
    ci = lax.broadcasted_iota(jnp.int32, (CHUNK, CHUNK), 1) // BLOCK16
    on_diag = ri == ci
    b16 = lambda xs: [x.astype(BF16) for x in xs]
    mm = lambda xs, ys: [_dot(x, y) for x, y in zip(xs, ys)]

    d = [jnp.where(on_diag, a, 0.0) for a in a_list]
    low = [a - x for a, x in zip(a_list, d)]
    d_16 = b16(d)
    d2 = mm(d_16, d_16)
    d2_16 = b16(d2)
    d4 = mm(d2_16, d2_16)
    dd2 = mm(d_16, d2_16)
    p = [x2 - x - y for x, x2, y in zip(d, d2, dd2)]
    d4_16 = b16(d4)
    d8 = mm(d4_16, d4_16)
    pd4 = mm(b16(p), d4_16)
    p = [x + y + z for x, y, z in zip(p, d4, pd4)]
    pd8 = mm(b16(p), b16(d8))
    p = [x + y + z for x, y, z in zip(p, d8, pd8)]
    p_16 = b16(p)
    pl_ = mm(p_16, b16(low))
    m = [x + y for x, y in zip(low, pl_)]
    m_16 = b16(m)
    m2 = mm(m_16, m_16)
    mm2 = mm(m_16, b16(m2))
    q = [x2 - x - y for x, x2, y in zip(m, m2, mm2)]
    qp = mm(b16(q), p_16)
    return [x + y + z for x, y, z in zip(q, p, qp)]


def _dn_core_kernel(q_ref, k_ref, v_ref, z_ref, bg_ref, nw_ref, o_ref,
                    state_ref, gam_ref, egam_ref, edec_ref, egl_ref,
                    u_ref, w_ref, qd_ref, kdt_ref, a_ref):
    @pl.when(pl.program_id(1) == 0)
    def _():
        state_ref[...] = jnp.zeros(state_ref.shape, F32)

    ri = lax.broadcasted_iota(jnp.int32, (TM, TM), 0)
    ci = lax.broadcasted_iota(jnp.int32, (TM, TM), 1)
    same_chunk = (ri // CHUNK) == (ci // CHUNK)
    cum_mat = jnp.where(same_chunk & (ci <= ri), 1.0, 0.0).astype(BF16)
    gam = _dot_exact_lhs(cum_mat, bg_ref[...])
    gam_chunks = gam.reshape(TM // CHUNK, CHUNK, LANES)
    gtot = jnp.broadcast_to(gam_chunks[:, CHUNK - 1:CHUNK, :], gam_chunks.shape).reshape(TM, LANES)
    gam_ref[...] = gam
    egam_ref[...] = jnp.exp(gam)
    edec_ref[...] = jnp.exp(gtot - gam)
    egl_ref[...] = jnp.exp(gtot)

    r_idx = lax.broadcasted_iota(jnp.int32, (CHUNK, CHUNK), 0)
    c_idx = lax.broadcasted_iota(jnp.int32, (CHUNK, CHUNK), 1)
    causal = c_idx <= r_idx
    strict = c_idx < r_idx
    scale = HEAD_DIM ** -0.5
    wide = (CHUNK, HEAD_DIM)

    def independent_part(grp, carry):
        probs = []
        for ci_ in range(PASS1_CHUNKS):
            cidx = grp * PASS1_CHUNKS + ci_
            rows = pl.ds(pl.multiple_of(cidx * CHUNK, CHUNK), CHUNK)
            gam_c = gam_ref[rows, :]
            gam_t = gam_c.T
            bg_c = bg_ref[rows, :]
            egam_c = egam_ref[rows, :]
            edec_c = edec_ref[rows, :]
            for hd in range(HEADS):
                lanes = slice(hd * HEAD_DIM, (hd + 1) * HEAD_DIM)
                gl = HEADS + hd
                beta = jnp.broadcast_to(bg_c[:, hd:hd + 1], wide)
                eg = jnp.broadcast_to(egam_c[:, gl:gl + 1], wide)
                ed = jnp.broadcast_to(edec_c[:, gl:gl + 1], wide)
                gcol = jnp.broadcast_to(gam_c[:, gl:gl + 1], wide)[:, :CHUNK]
                decay = jnp.exp(jnp.where(causal, gcol - gam_t[gl:gl + 1, :], -jnp.inf))
                probs.append(dict(cidx=cidx, rows=rows, hd=hd, lanes=lanes, beta=beta, eg=eg, ed=ed,
                                  decay=decay, q=q_ref[rows, lanes], k=k_ref[rows, lanes],
                                  v=v_ref[rows, lanes]))
        for p in probs:
            p["k16"] = p["k"].astype(BF16)
            p["kq16"] = jnp.concatenate([p["k16"], (p["q"] * scale).astype(BF16)], axis=0)
        kkqk = [_dot_nt(p["kq16"], p["k16"]) for p in probs]
        a_list = []
        for p, s in zip(probs, kkqk):
            a_list.append(jnp.where(strict, s[:CHUNK] * p["beta"][:, :CHUNK] * p["decay"], 0.0))
            a_ref[p["hd"], p["rows"], :] = (s[CHUNK:] * p["decay"]).astype(BF16)
        tinv = _inverse_minus_eye_many(a_list)
        for p in probs:
            p["vb"] = p["v"] * p["beta"]
            p["kbg"] = p["k"] * p["beta"] * p["eg"]
        uw = [_dot(t.astype(BF16), jnp.concatenate([p["vb"], p["kbg"]], axis=1).astype(BF16))
              for p, t in zip(probs, tinv)]
        for p, r in zip(probs, uw):
            rows, lanes = p["rows"], p["lanes"]
            u_ref[rows, lanes] = p["vb"] + r[:, :HEAD_DIM]
            w_ref[rows, lanes] = (p["kbg"] + r[:, HEAD_DIM:]).astype(BF16)
            qd_ref[rows, lanes] = (p["q"] * (p["eg"] * scale)).astype(BF16)
            kdt_ref[p["hd"], p["cidx"]] = (p["k"] * p["ed"]).T.astype(BF16)
        return carry

    lax.fori_loop(0, TM // CHUNK // PASS1_CHUNKS, independent_part, 0)

    nw = nw_ref[...]

    def recurrent_part(c, carry):
        rows = pl.ds(pl.multiple_of(c * CHUNK, CHUNK), CHUNK)
        egl_c = egl_ref[rows, :]
        heads = range(HEADS)
        lanes = [slice(hd * HEAD_DIM, (hd + 1) * HEAD_DIM) for hd in heads]
        state = [state_ref[hd] for hd in heads]
        wq = [jnp.concatenate([w_ref[rows, lanes[hd]], qd_ref[rows, lanes[hd]]], axis=0) for hd in heads]
        r = [_dot(wq[hd], state[hd].astype(BF16)) for hd in heads]
        vn16 = [(u_ref[rows, lanes[hd]] - r[hd][:CHUNK]).astype(BF16) for hd in heads]
        av = [_dot(a_ref[hd, rows, :], vn16[hd]) for hd in heads]
        kv = [_dot(kdt_ref[hd, c], vn16[hd]) for hd in heads]
        for hd in heads:
            state_ref[hd] = state[hd] * egl_c[0:1, HEADS + hd:HEADS + hd + 1] + kv[hd]
            o = r[hd][CHUNK:] + av[hd]
            on = o * lax.rsqrt(jnp.mean(o * o, axis=-1, keepdims=True) + RMS_EPS) * nw
            o_ref[rows, lanes[hd]] = (on * _silu(z_ref[rows, lanes[hd]])).astype(BF16)
        return carry

    lax.fori_loop(0, TM // CHUNK, recurrent_part, 0)


def _dn_core(q, k, v, z, bg, norm_w, batch, tiles_per_batch):
    rows = q.shape[0]
    row_spec = pl.BlockSpec((TM, D_MODEL), lambda b, s: (b * tiles_per_batch + s, 0))
    small = pltpu.VMEM((TM, LANES), F32)
    return pl.pallas_call(
        _dn_core_kernel,
        grid=(batch, tiles_per_batch),
        in_specs=[row_spec, row_spec, row_spec, row_spec,
                  pl.BlockSpec((TM, LANES), lambda b, s: (b * tiles_per_batch + s, 0)),
                  pl.BlockSpec((1, HEAD_DIM), lambda b, s: (0, 0))],
        out_specs=row_spec,
        out_shape=jax.ShapeDtypeStruct((rows, D_MODEL), BF16),
        scratch_shapes=[
            pltpu.VMEM((HEADS, HEAD_DIM, HEAD_DIM), F32),
            small, small, small, small,
            pltpu.VMEM((TM, D_MODEL), F32),
            pltpu.VMEM((TM, D_MODEL), BF16),
            pltpu.VMEM((TM, D_MODEL), BF16),
            pltpu.VMEM((HEADS, TM // CHUNK, HEAD_DIM, CHUNK), BF16),
            pltpu.VMEM((HEADS, TM, CHUNK), BF16),
        ],
        compiler_params=pltpu.CompilerParams(dimension_semantics=("arbitrary", "arbitrary"),
                                             vmem_limit_bytes=VMEM_LIMIT),
        name="dn_core",
    )(q, k, v, z, bg, norm_w)


def _cf_conv_piece(ext_ref, dw_ref, res, piece, not_before):
    n = CF_PIECE + CF_HALO + SUBLANES
    start = pl.multiple_of(piece * CF_PIECE + _zero_after(not_before), SUBLANES)
    window = ext_ref[pl.ds(start, n), :]
    shifted = window if res == 0 else pltpu.roll(window, n - res, axis=0)
    acc = None
    for j in range(CF_KERNEL):
        off = CF_HALO - (CF_KERNEL - 1) + j
        if off % SUBLANES != res:
            continue
        term = dw_ref[j:j + 1, :] * shifted[off - res:off - res + CF_PIECE, :]
        acc = term if acc is None else acc + term
    return acc


def _cf_layer_kernel(x_ref, xres_ref, sh1_ref, sc1_ref, gt1_ref, sh2_ref, sc2_ref, gt2_ref,
                     win_ref, dw_ref, dwb_ref, cg_ref, cb_ref, wo_ref, lng_ref, lnb_ref, w1_ref, w2_ref,
                     o_ref, ext_ref, ext_next_ref, u_ref, u_next_ref, *, n_tiles, tiles_per_batch):
    s = pl.program_id(0)
    t_in = jnp.minimum(s, n_tiles - 1)
    first_of_batch = t_in % tiles_per_batch == 0

    @pl.when(s == 0)
    def _():
        ext_ref[...] = jnp.zeros(ext_ref.shape, F32)
        u_ref[...] = jnp.zeros(u_ref.shape, BF16)
        ext_next_ref[CF_HALO + TM:CF_HALO + TM + SUBLANES, :] = jnp.zeros((SUBLANES, D_MODEL), F32)

    @pl.when(first_of_batch)
    def _():
        ext_next_ref[0:CF_HALO, :] = jnp.zeros((CF_HALO, D_MODEL), F32)

    @pl.when(jnp.logical_not(first_of_batch))
    def _():
        ext_next_ref[0:CF_HALO, :] = ext_ref[TM:TM + CF_HALO, :]

    n_chunks = FF_DIM // FF_TILE
    n_pieces = TM // CF_PIECE
    conv = [None] * n_pieces
    todo = [(res, piece) for res in range(SUBLANES) for piece in range(n_pieces)]
    slots_seen = [0]

    def conv_pieces_after(result):
        for blk in range(n_pieces):
            slots_seen[0] += 1
            if slots_seen[0] % 5 == 0 or not todo:
                continue
            res, piece = todo.pop(0)
            r0 = blk * CF_PIECE
            part = _cf_conv_piece(ext_ref, dw_ref, res, piece, result[r0:r0 + 1, 0:1])
            conv[piece] = part if conv[piece] is None else conv[piece] + part

    y = _dot(u_ref[...], wo_ref[...])
    conv_pieces_after(y)

    h = (x_ref[...] * (1.0 + sc1_ref[0, 0]) + sh1_ref[0, 0]).astype(BF16)
    val = _dot(h, win_ref[:, 0:D_MODEL])
    conv_pieces_after(val)
    gate = _dot(h, win_ref[:, D_MODEL:2 * D_MODEL])
    conv_pieces_after(gate)
    ext_next_ref[CF_HALO:CF_HALO + TM, :] = val * jax.nn.sigmoid(gate)

    x1 = _layer_norm(ALPHA * xres_ref[...] + (1.0 + gt1_ref[0, 0]) * y, lng_ref[0:1, :], lnb_ref[0:1, :])
    h2 = (x1 * (1.0 + sc2_ref[0, 0]) + sh2_ref[0, 0]).astype(BF16)
    chunk_cols = [slice(c * FF_TILE, (c + 1) * FF_TILE) for c in range(n_chunks)]
    d1 = _dot(h2, w1_ref[:, chunk_cols[0]])
    conv_pieces_after(d1)
    acc = None
    for c in range(n_chunks):
        d1_next = None
        if c + 1 < n_chunks:
            d1_next = _dot(h2, w1_ref[:, chunk_cols[c + 1]])
            conv_pieces_after(d1_next)
        a = jnp.maximum(d1, 0.0)
        p = _dot((a * a).astype(BF16), w2_ref[chunk_cols[c], :])
        conv_pieces_after(p)
        acc = p if acc is None else acc + p
        d1 = d1_next
    assert not todo
    o_ref[...] = _layer_norm(ALPHA * x1 + (1.0 + gt2_ref[0, 0]) * acc, lng_ref[1:2, :], lnb_ref[1:2, :])

    cu = _layer_norm(jnp.concatenate(conv, axis=0) + dwb_ref[...], cg_ref[...], cb_ref[...])
    u_next_ref[...] = _silu(cu).astype(BF16)

    u_ref[...] = u_next_ref[...]
    ext_ref[...] = ext_next_ref[...]


def _cf_layer(x2d, mods, layer, mixer, tiles_per_batch, w_in, dw_w, dw_b, cf_g, cf_b, w_out, ln_g, ln_b,
              w1, w2):
    rows = x2d.shape[0]
    n_tiles = rows // TM
    base = layer * N_MOD
    in_tile = lambda s: jnp.minimum(s, n_tiles - 1)
    mlp_tile = lambda s: jnp.maximum(s - 2, 0)

    def mod_spec(row, tile_of_step):
        return pl.BlockSpec((1, 1, 1, D_MODEL), lambda s: (row, tile_of_step(s) // tiles_per_batch, 0, 0))

    ext_rows = CF_HALO + TM + SUBLANES
    return pl.pallas_call(
        functools.partial(_cf_layer_kernel, n_tiles=n_tiles, tiles_per_batch=tiles_per_batch),
        grid=(n_tiles + 2,),
        in_specs=[
            pl.BlockSpec((TM, D_MODEL), lambda s: (in_tile(s), 0)),
            pl.BlockSpec((TM, D_MODEL), lambda s: (mlp_tile(s), 0)),
            mod_spec(base + 0, in_tile), mod_spec(base + 1, in_tile),
            mod_spec(base + 2, mlp_tile), mod_spec(base + 3, mlp_tile),
            mod_spec(base + 4, mlp_tile), mod_spec(base + 5, mlp_tile),
            _const_spec((D_MODEL, 2 * D_MODEL), mixer), _const_spec((CF_KERNEL, D_MODEL)),
            _const_spec((1, D_MODEL)), _const_spec((1, D_MODEL)), _const_spec((1, D_MODEL)),
            _const_spec((D_MODEL, D_MODEL), mixer), _const_spec((2, D_MODEL)), _const_spec((2, D_MODEL)),
            _const_spec((D_MODEL, FF_DIM), layer), _const_spec((FF_DIM, D_MODEL), layer),
        ],
        out_specs=pl.BlockSpec((TM, D_MODEL), lambda s: (mlp_tile(s), 0)),
        out_shape=jax.ShapeDtypeStruct((rows, D_MODEL), F32),
        scratch_shapes=[pltpu.VMEM((ext_rows, D_MODEL), F32),
                        pltpu.VMEM((ext_rows, D_MODEL), F32),
                        pltpu.VMEM((TM, D_MODEL), BF16),
                        pltpu.VMEM((TM, D_MODEL), BF16)],
        compiler_params=pltpu.CompilerParams(dimension_semantics=("arbitrary",),
                                             vmem_limit_bytes=VMEM_LIMIT),
        name="cf_layer",
    )(x2d, x2d, mods, mods, mods, mods, mods, mods, w_in, dw_w, dw_b, cf_g, cf_b, w_out, ln_g, ln_b, w1, w2)


def kernel(x, c, ada_w, ada_b, ln_g, ln_b, dn_w_in, dn_conv_w, dn_a_log, dn_dt_bias, dn_norm_w,
           dn_w_out, cf_w_in, cf_dw_w, cf_dw_b, cf_ln_g, cf_ln_b, cf_w_out, ff_w1, ff_w2):
    batch, seq, d = x.shape
    assert d == D_MODEL and seq % TM == 0 and TM % CHUNK == 0
    tiles_per_batch = seq // TM
    n_qkv = 3 * D_MODEL
    n_main = 4 * D_MODEL

    mods = _ada_mods(c, ada_w, ada_b)
    w1, w2 = _cast_stack(ff_w1), _cast_stack(ff_w2)
    dn_main, dn_out = _cast_stack(dn_w_in, cols=n_main), _cast_stack(dn_w_out)
    cf_in, cf_out = _cast_stack(cf_w_in), _cast_stack(cf_w_out)
    x2d = x.reshape(batch * seq, D_MODEL)
    for i in range(DEPTH):
        j = i // 2
        if i % 2 == 0:
            w_ba = jnp.pad(dn_w_in[j, :, n_main:], ((0, 0), (0, LANES - 2 * HEADS)))
            gate_pad = (HEADS, LANES - 2 * HEADS)
            alog_vec = jnp.pad(dn_a_log[j], gate_pad).reshape(1, LANES)
            dtb_vec = jnp.pad(dn_dt_bias[j], gate_pad).reshape(1, LANES)
            q, k, v, z, bg = _dn_in(x2d, mods, i, j, tiles_per_batch, dn_main, w_ba,
                                    dn_conv_w[j][:, :n_qkv], alog_vec, dtb_vec)
            u = _dn_core(q, k, v, z, bg, dn_norm_w[j].reshape(1, HEAD_DIM), batch, tiles_per_batch)
            x2d = _post(x2d, u, mods, i, j, tiles_per_batch, dn_out, ln_g[i], ln_b[i], w1, w2)
        else:
            x2d = _cf_layer(x2d, mods, i, j, tiles_per_batch, cf_in, cf_dw_w[j],
                            cf_dw_b[j].reshape(1, D_MODEL), cf_ln_g[j].reshape(1, D_MODEL),
                            cf_ln_b[j].reshape(1, D_MODEL), cf_out, ln_g[i], ln_b[i], w1, w2)
    return x2d.reshape(batch, seq, D_MODEL)
```

```python
import functools

import jax
import jax.numpy as jnp
from jax import lax
from jax.experimental import pallas as pl
from jax.experimental.pallas import tpu as pltpu

F32 = jnp.float32
BF16 = jnp.bfloat16

D_MODEL = 1024
DEPTH = 4
N_MOD = 6
HEADS = 8
HEAD_DIM = 128
DN_CONV = 4
CHUNK = 64
BLOCK16 = 16
CF_KERNEL = 31
FF_DIM = 4 * D_MODEL
ALPHA = (2.0 * DEPTH) ** 0.25
LN_EPS = 1e-5
RMS_EPS = 1e-6
L2_EPS = 1e-6

LANES = 128
SUBLANES = 8
TM = 512
FF_TILE = 1024
CF_HALO = 32
CF_PIECE = 128
DN_HALO = 8
PASS1_CHUNKS = 4
VMEM_LIMIT = 56 * 1024 * 1024
CAST_BLOCK_ELEMS = 1024 * 1024


def _dot(a, b):
    return jnp.dot(a, b, preferred_element_type=F32)


def _dot_nt(a, b):
    return lax.dot_general(a, b, (((1,), (1,)), ((), ())), preferred_element_type=F32)


def _dot_tn(a, b):
    return lax.dot_general(a, b, (((0,), (0,)), ((), ())), preferred_element_type=F32)


def _bdot(a, b):
    return _dot(a.astype(BF16), b.astype(BF16))


def _split3(x):
    x1 = x.astype(BF16)
    r1 = x - x1.astype(F32)
    x2 = r1.astype(BF16)
    x3 = (r1 - x2.astype(F32)).astype(BF16)
    return x1, x2, x3


def _dot_exact_lhs(a_bf16, x):
    x1, x2, x3 = _split3(x)
    return _dot(a_bf16, x3) + _dot(a_bf16, x2) + _dot(a_bf16, x1)


def _layer_norm(r, g, b):
    mu = jnp.mean(r, axis=-1, keepdims=True)
    d = r - mu
    var = jnp.mean(d * d, axis=-1, keepdims=True)
    return d * lax.rsqrt(var + LN_EPS) * g + b


def _silu(x):
    return x * jax.nn.sigmoid(x)


def _softplus(x):
    return jnp.maximum(x, 0.0) + jnp.log1p(jnp.exp(-jnp.abs(x)))


def _zero_after(v):
    bits = lax.bitcast_convert_type(v, jnp.uint32)
    z = lax.shift_right_logical(lax.shift_right_logical(bits, jnp.uint32(16)), jnp.uint32(16))
    return z[0, 0].astype(jnp.int32)


def _const_spec(shape, layer=None):
    nd = len(shape)
    if layer is None:
        return pl.BlockSpec(shape, lambda *_: (0,) * nd, pipeline_mode=pl.Buffered(1))
    return pl.BlockSpec((None,) + tuple(shape), lambda *_: (layer,) + (0,) * nd,
                        pipeline_mode=pl.Buffered(1))


def _mod_spec(row, tiles_per_batch):
    return pl.BlockSpec((1, 1, 1, D_MODEL), lambda t: (row, t // tiles_per_batch, 0, 0))


def _ada_kernel(c_ref, w_ref, b_ref, o_ref):
    cond = _silu(c_ref[...])
    o_ref[0] = _dot(cond, w_ref[0]) + b_ref[0]


def _ada_mods(c, ada_w, ada_b):
    batch = c.shape[0]
    out = pl.pallas_call(
        _ada_kernel,
        grid=(DEPTH, N_MOD),
        in_specs=[
            pl.BlockSpec((batch, D_MODEL), lambda i, j: (0, 0)),
            pl.BlockSpec((1, D_MODEL, D_MODEL), lambda i, j: (i, 0, j)),
            pl.BlockSpec((1, 1, D_MODEL), lambda i, j: (i, 0, j)),
        ],
        out_specs=pl.BlockSpec((1, batch, D_MODEL), lambda i, j: (i * N_MOD + j, 0, 0)),
        out_shape=jax.ShapeDtypeStruct((DEPTH * N_MOD, batch, D_MODEL), F32),
        compiler_params=pltpu.CompilerParams(dimension_semantics=("arbitrary", "arbitrary")),
        name="ada_mods",
    )(c, ada_w, ada_b.reshape(DEPTH, 1, N_MOD * D_MODEL))
    return out.reshape(DEPTH * N_MOD, batch, 1, D_MODEL)


def _cast_kernel(w_ref, o_ref):
    o_ref[...] = w_ref[:, 0:o_ref.shape[-1]].astype(BF16)


def _cast_stack(w, cols=None):
    n_layers, rows, all_cols = w.shape
    cols = all_cols if cols is None else cols
    block_rows = rows
    while block_rows * all_cols > CAST_BLOCK_ELEMS:
        block_rows //= 2
    assert rows % block_rows == 0 and block_rows % SUBLANES == 0 and cols % LANES == 0
    return pl.pallas_call(
        _cast_kernel,
        grid=(n_layers, rows // block_rows),
        in_specs=[pl.BlockSpec((None, block_rows, all_cols), lambda l, i: (l, i, 0))],
        out_specs=pl.BlockSpec((None, block_rows, cols), lambda l, i: (l, i, 0)),
        out_shape=jax.ShapeDtypeStruct((n_layers, rows, cols), BF16),
        compiler_params=pltpu.CompilerParams(dimension_semantics=("arbitrary", "arbitrary")),
        name="cast_bf16",
    )(w)


def _post_kernel(x_ref, u_ref, gt1_ref, sh2_ref, sc2_ref, gt2_ref, wo_ref, lng_ref, lnb_ref,
                 w1_ref, w2_ref, o_ref):
    x = x_ref[...]
    y = _dot(u_ref[...], wo_ref[...])
    x1 = _layer_norm(ALPHA * x + (1.0 + gt1_ref[0, 0]) * y, lng_ref[0:1, :], lnb_ref[0:1, :])
    h = (x1 * (1.0 + sc2_ref[0, 0]) + sh2_ref[0, 0]).astype(BF16)
    n_chunks = FF_DIM // FF_TILE
    chunk_cols = [slice(c * FF_TILE, (c + 1) * FF_TILE) for c in range(n_chunks)]
    d1 = _dot(h, w1_ref[:, chunk_cols[0]])
    acc = None
    for c in range(n_chunks):
        d1_next = _dot(h, w1_ref[:, chunk_cols[c + 1]]) if c + 1 < n_chunks else None
        a = jnp.maximum(d1, 0.0)
        p = _dot((a * a).astype(BF16), w2_ref[chunk_cols[c], :])
        acc = p if acc is None else acc + p
        d1 = d1_next
    o_ref[...] = _layer_norm(ALPHA * x1 + (1.0 + gt2_ref[0, 0]) * acc, lng_ref[1:2, :], lnb_ref[1:2, :])


def _post(x2d, u2d, mods, layer, mixer, tiles_per_batch, w_out, ln_g, ln_b, w1, w2):
    rows = x2d.shape[0]
    row_spec = pl.BlockSpec((TM, D_MODEL), lambda t: (t, 0))
    base = layer * N_MOD
    return pl.pallas_call(
        _post_kernel,
        grid=(rows // TM,),
        in_specs=[
            row_spec, row_spec,
            _mod_spec(base + 2, tiles_per_batch), _mod_spec(base + 3, tiles_per_batch),
            _mod_spec(base + 4, tiles_per_batch), _mod_spec(base + 5, tiles_per_batch),
            _const_spec((D_MODEL, D_MODEL), mixer), _const_spec((2, D_MODEL)), _const_spec((2, D_MODEL)),
            _const_spec((D_MODEL, FF_DIM), layer), _const_spec((FF_DIM, D_MODEL), layer),
        ],
        out_specs=row_spec,
        out_shape=jax.ShapeDtypeStruct((rows, D_MODEL), F32),
        compiler_params=pltpu.CompilerParams(dimension_semantics=("arbitrary",),
                                             vmem_limit_bytes=VMEM_LIMIT),
        name="post_mlp",
    )(x2d, u2d, mods, mods, mods, mods, w_out, ln_g, ln_b, w1, w2)


def _dn_in_kernel(x_ref, sh_ref, sc_ref, w_ref, wba_ref, cw_ref, alog_ref, dtb_ref,
                  q_ref, k_ref, v_ref, z_ref, bg_ref, ext_ref, *, tiles_per_batch):
    t = pl.program_id(0)

    @pl.when(t % tiles_per_batch == 0)
    def _():
        ext_ref[0:DN_HALO, :] = jnp.zeros((DN_HALO, 3 * D_MODEL), F32)

    h = (x_ref[...] * (1.0 + sc_ref[0, 0]) + sh_ref[0, 0]).astype(BF16)

    for grp, out_ref in enumerate((q_ref, k_ref, v_ref)):
        cols = slice(grp * D_MODEL, (grp + 1) * D_MODEL)
        ext_ref[DN_HALO:DN_HALO + TM, cols] = _dot(h, w_ref[:, cols])
        ext = ext_ref[:, cols]
        acc = None
        for j in range(DN_CONV):
            off = DN_HALO - (DN_CONV - 1) + j
            res = off % SUBLANES
            shifted = ext if res == 0 else pltpu.roll(ext, DN_HALO + TM - res, axis=0)
            term = cw_ref[j:j + 1, cols] * shifted[off - res:off - res + TM, :]
            acc = term if acc is None else acc + term
        ext_ref[0:DN_HALO, cols] = ext_ref[TM:TM + DN_HALO, cols]
        y = _silu(acc)
        if out_ref is v_ref:
            out_ref[...] = y
        else:
            for hd in range(HEADS):
                lanes = slice(hd * HEAD_DIM, (hd + 1) * HEAD_DIM)
                ys = y[:, lanes]
                ss = jnp.sum(ys * ys, axis=-1, keepdims=True)
                out_ref[:, lanes] = ys * lax.rsqrt(ss + L2_EPS)

    z_ref[...] = _dot(h, w_ref[:, 3 * D_MODEL:4 * D_MODEL])

    ba = _dot(h, wba_ref[...].astype(BF16))
    beta = jax.nn.sigmoid(ba)
    g = -jnp.exp(alog_ref[...]) * _softplus(ba + dtb_ref[...])
    lane = lax.broadcasted_iota(jnp.int32, ba.shape, 1)
    bg_ref[...] = jnp.where(lane < HEADS, beta, g)


def _dn_in(x2d, mods, layer, mixer, tiles_per_batch, w_main, w_ba, conv_w, alog_vec, dtb_vec):
    rows = x2d.shape[0]
    row_spec = pl.BlockSpec((TM, D_MODEL), lambda t: (t, 0))
    base = layer * N_MOD
    out_sds = jax.ShapeDtypeStruct((rows, D_MODEL), F32)
    return pl.pallas_call(
        functools.partial(_dn_in_kernel, tiles_per_batch=tiles_per_batch),
        grid=(rows // TM,),
        in_specs=[
            row_spec, _mod_spec(base + 0, tiles_per_batch), _mod_spec(base + 1, tiles_per_batch),
            _const_spec((D_MODEL, 4 * D_MODEL), mixer), _const_spec((D_MODEL, LANES)),
            _const_spec((DN_CONV, 3 * D_MODEL)), _const_spec((1, LANES)), _const_spec((1, LANES)),
        ],
        out_specs=[row_spec, row_spec, row_spec, row_spec, pl.BlockSpec((TM, LANES), lambda t: (t, 0))],
        out_shape=[out_sds, out_sds, out_sds, out_sds, jax.ShapeDtypeStruct((rows, LANES), F32)],
        scratch_shapes=[pltpu.VMEM((DN_HALO + TM, 3 * D_MODEL), F32)],
        compiler_params=pltpu.CompilerParams(dimension_semantics=("arbitrary",),
                                             vmem_limit_bytes=VMEM_LIMIT),
        name="dn_in",
    )(x2d, mods, mods, w_main, w_ba, conv_w, alog_vec, dtb_vec)


def _inverse_minus_eye_levels(a_list, out):
    ri = lax.broadcasted_iota(jnp.int32, (CHUNK, CHUNK), 0) // BLOCK16
    ci = lax.broadcasted_iota(jnp.int32, (CHUNK, CHUNK), 1) // BLOCK16
    on_diag = ri == ci
    b16 = lambda xs: [x.astype(BF16) for x in xs]
    mm = lambda xs, ys: [_dot(x, y) for x, y in zip(xs, ys)]

    d = [jnp.where(on_diag, a, 0.0) for a in a_list]
    low = [a - x for a, x in zip(a_list, d)]
    d_16 = b16(d)
    d2 = mm(d_16, d_16)
    yield
    d2_16 = b16(d2)
    d4 = mm(d2_16, d2_16)
    dd2 = mm(d_16, d2_16)
    yield
    p = [x2 - x - y for x, x2, y in zip(d, d2, dd2)]
    d4_16 = b16(d4)
    d8 = mm(d4_16, d4_16)
    pd4 = mm(b16(p), d4_16)
    yield
    p = [x + y + z for x, y, z in zip(p, d4, pd4)]
    pd8 = mm(b16(p), b16(d8))
    yield
    p = [x + y + z for x, y, z in zip(p, d8, pd8)]
    p_16 = b16(p)
    pl_ = mm(p_16, b16(low))
    yield
    m = [x + y for x, y in zip(low, pl_)]
    m_16 = b16(m)
    m2 = mm(m_16, m_16)
    yield
    mm2 = mm(m_16, b16(m2))
    yield
    q = [x2 - x - y for x, x2, y in zip(m, m2, mm2)]
    qp = mm(b16(q), p_16)
    yield
    out.extend(x + y + z for x, y, z in zip(q, p, qp))


def _dn_core_kernel(q_ref, k_ref, v_ref, z_ref, bg_ref, nw_ref, o_ref,
                    state_ref, gam_ref, egam_ref, edec_ref, egl_ref,
                    u_ref, w_ref, qd_ref, kdt_ref, a_ref, *, tiles_per_batch):
    s = pl.program_id(0)
    cur = s % 2
    prev = 1 - cur

    @pl.when(s == 0)
    def _():
        for ref in (egl_ref, u_ref, w_ref, qd_ref, kdt_ref, a_ref):
            ref[1] = jnp.zeros(ref.shape[1:], ref.dtype)

    @pl.when(jnp.maximum(s - 1, 0) % tiles_per_batch == 0)
    def _():
        state_ref[...] = jnp.zeros(state_ref.shape, F32)

    ri = lax.broadcasted_iota(jnp.int32, (TM, TM), 0)
    ci = lax.broadcasted_iota(jnp.int32, (TM, TM), 1)
    same_chunk = (ri // CHUNK) == (ci // CHUNK)
    cum_mat = jnp.where(same_chunk & (ci <= ri), 1.0, 0.0).astype(BF16)
    gam = _dot_exact_lhs(cum_mat, bg_ref[...])
    gam_chunks = gam.reshape(TM // CHUNK, CHUNK, LANES)
    gtot = jnp.broadcast_to(gam_chunks[:, CHUNK - 1:CHUNK, :], gam_chunks.shape).reshape(TM, LANES)
    gam_ref[...] = gam
    egam_ref[...] = jnp.exp(gam)
    edec_ref[...] = jnp.exp(gtot - gam)
    egl_ref[cur] = jnp.exp(gtot)

    r_idx = lax.broadcasted_iota(jnp.int32, (CHUNK, CHUNK), 0)
    c_idx = lax.broadcasted_iota(jnp.int32, (CHUNK, CHUNK), 1)
    causal = c_idx <= r_idx
    strict = c_idx < r_idx
    scale = HEAD_DIM ** -0.5
    wide = (CHUNK, HEAD_DIM)
    nw = nw_ref[...]

    def independent_levels(grp):
        probs = []
        for ci_ in range(PASS1_CHUNKS):
            cidx = grp * PASS1_CHUNKS + ci_
            rows = pl.ds(pl.multiple_of(cidx * CHUNK, CHUNK), CHUNK)
            gam_c = gam_ref[rows, :]
            gam_t = gam_c.T
            bg_c = bg_ref[rows, :]
            egam_c = egam_ref[rows, :]
            edec_c = edec_ref[rows, :]
            for hd in range(HEADS):
                lanes = slice(hd * HEAD_DIM, (hd + 1) * HEAD_DIM)
                gl = HEADS + hd
                beta = jnp.broadcast_to(bg_c[:, hd:hd + 1], wide)
                eg = jnp.broadcast_to(egam_c[:, gl:gl + 1], wide)
                ed = jnp.broadcast_to(edec_c[:, gl:gl + 1], wide)
                gcol = jnp.broadcast_to(gam_c[:, gl:gl + 1], wide)[:, :CHUNK]
                decay = jnp.exp(jnp.where(causal, gcol - gam_t[gl:gl + 1, :], -jnp.inf))
                probs.append(dict(cidx=cidx, rows=rows, hd=hd, lanes=lanes, beta=beta, eg=eg, ed=ed,
                                  decay=decay, q=q_ref[rows, lanes], k=k_ref[rows, lanes],
                                  v=v_ref[rows, lanes]))
        for p in probs:
            p["k16"] = p["k"].astype(BF16)
            p["kq16"] = jnp.concatenate([p["k16"], (p["q"] * scale).astype(BF16)], axis=0)
        kkqk = [_dot_nt(p["kq16"], p["k16"]) for p in probs]
        yield
        a_list = []
        for p, kk in zip(probs, kkqk):
            a_list.append(jnp.where(strict, kk[:CHUNK] * p["beta"][:, :CHUNK] * p["decay"], 0.0))
            a_ref[cur, p["hd"], p["rows"], :] = (kk[CHUNK:] * p["decay"]).astype(BF16)
        tinv = []
        yield from _inverse_minus_eye_levels(a_list, tinv)
        for p in probs:
            p["vb"] = p["v"] * p["beta"]
            p["kbg"] = p["k"] * p["beta"] * p["eg"]
        uw = [_dot(t.astype(BF16), jnp.concatenate([p["vb"], p["kbg"]], axis=1).astype(BF16))
              for p, t in zip(probs, tinv)]
        yield
        for p, r in zip(probs, uw):
            rows, lanes = p["rows"], p["lanes"]
            u_ref[cur, rows, lanes] = p["vb"] + r[:, :HEAD_DIM]
            w_ref[cur, rows, lanes] = (p["kbg"] + r[:, HEAD_DIM:]).astype(BF16)
            qd_ref[cur, rows, lanes] = (p["q"] * (p["eg"] * scale)).astype(BF16)
            kdt_ref[cur, p["hd"], p["cidx"]] = (p["k"] * p["ed"]).T.astype(BF16)

    def recurrent_levels(grp):
        heads = range(HEADS)
        lanes = [slice(hd * HEAD_DIM, (hd + 1) * HEAD_DIM) for hd in heads]
        for ci_ in range(PASS1_CHUNKS):
            c = grp * PASS1_CHUNKS + ci_
            rows = pl.ds(pl.multiple_of(c * CHUNK, CHUNK), CHUNK)
            egl_c = egl_ref[prev, rows, :]
            state = [state_ref[hd] for hd in heads]
            wq = [jnp.concatenate([w_ref[prev, rows, lanes[hd]], qd_ref[prev, rows, lanes[hd]]], axis=0)
                  for hd in heads]
            r = [_dot(wq[hd], state[hd].astype(BF16)) for hd in heads]
            yield
            vn16 = [(u_ref[prev, rows, lanes[hd]] - r[hd][:CHUNK]).astype(BF16) for hd in heads]
            av = [_dot(a_ref[prev, hd, rows, :], vn16[hd]) for hd in heads]
            kv = [_dot(kdt_ref[prev, hd, c], vn16[hd]) for hd in heads]
            yield
            for hd in heads:
                state_ref[hd] = state[hd] * egl_c[0:1, HEADS + hd:HEADS + hd + 1] + kv[hd]
                o = r[hd][CHUNK:] + av[hd]
                on = o * lax.rsqrt(jnp.mean(o * o, axis=-1, keepdims=True) + RMS_EPS) * nw
                o_ref[rows, lanes[hd]] = (on * _silu(z_ref[rows, lanes[hd]])).astype(BF16)

    def group_body(grp, carry):
        pending = [recurrent_levels(grp), independent_levels(grp)]
        while pending:
            for gen in list(pending):
                try:
                    next(gen)
                except StopIteration:
                    pending.remove(gen)
        return carry

    lax.fori_loop(0, TM // CHUNK // PASS1_CHUNKS, group_body, 0)


def _dn_core(q, k, v, z, bg, norm_w, tiles_per_batch):
    rows = q.shape[0]
    n_tiles = rows // TM
    in_tile = lambda s: jnp.minimum(s, n_tiles - 1)
    out_tile = lambda s: jnp.maximum(s - 1, 0)
    in_rows = pl.BlockSpec((TM, D_MODEL), lambda s: (in_tile(s), 0))
    out_rows = pl.BlockSpec((TM, D_MODEL), lambda s: (out_tile(s), 0))
    small = pltpu.VMEM((TM, LANES), F32)
    return pl.pallas_call(
        functools.partial(_dn_core_kernel, tiles_per_batch=tiles_per_batch),
        grid=(n_tiles + 1,),
        in_specs=[in_rows, in_rows, in_rows, out_rows,
                  pl.BlockSpec((TM, LANES), lambda s: (in_tile(s), 0)),
                  pl.BlockSpec((1, HEAD_DIM), lambda s: (0, 0))],
        out_specs=out_rows,
        out_shape=jax.ShapeDtypeStruct((rows, D_MODEL), BF16),
        scratch_shapes=[
            pltpu.VMEM((HEADS, HEAD_DIM, HEAD_DIM), F32),
            small, small, small,
            pltpu.VMEM((2, TM, LANES), F32),
            pltpu.VMEM((2, TM, D_MODEL), F32),
            pltpu.VMEM((2, TM, D_MODEL), BF16),
            pltpu.VMEM((2, TM, D_MODEL), BF16),
            pltpu.VMEM((2, HEADS, TM // CHUNK, HEAD_DIM, CHUNK), BF16),
            pltpu.VMEM((2, HEADS, TM, CHUNK), BF16),
        ],
        compiler_params=pltpu.CompilerParams(dimension_semantics=("arbitrary",),
                                             vmem_limit_bytes=VMEM_LIMIT),
        name="dn_core",
    )(q, k, v, z, bg, norm_w)


def _cf_conv_piece(ext_ref, dw_ref, res, piece, not_before):
    n = CF_PIECE + CF_HALO + SUBLANES
    start = pl.multiple_of(piece * CF_PIECE + _zero_after(not_before), SUBLANES)
    window = ext_ref[pl.ds(start, n), :]
    shifted = window if res == 0 else pltpu.roll(window, n - res, axis=0)
    acc = None
    for j in range(CF_KERNEL):
        off = CF_HALO - (CF_KERNEL - 1) + j
        if off % SUBLANES != res:
            continue
        term = dw_ref[j:j + 1, :] * shifted[off - res:off - res + CF_PIECE, :]
        acc = term if acc is None else acc + term
    return acc


def _cf_layer_kernel(x_ref, xres_ref, sh1_ref, sc1_ref, gt1_ref, sh2_ref, sc2_ref, gt2_ref,
                     win_ref, dw_ref, dwb_ref, cg_ref, cb_ref, wo_ref, lng_ref, lnb_ref, w1_ref, w2_ref,
                     o_ref, ext_ref, ext_next_ref, u_ref, u_next_ref, *, n_tiles, tiles_per_batch):
    s = pl.program_id(0)
    t_in = jnp.minimum(s, n_tiles - 1)
    first_of_batch = t_in % tiles_per_batch == 0

    @pl.when(s == 0)
    def _():
        ext_ref[...] = jnp.zeros(ext_ref.shape, F32)
        u_ref[...] = jnp.zeros(u_ref.shape, BF16)
        ext_next_ref[CF_HALO + TM:CF_HALO + TM + SUBLANES, :] = jnp.zeros((SUBLANES, D_MODEL), F32)

    @pl.when(first_of_batch)
    def _():
        ext_next_ref[0:CF_HALO, :] = jnp.zeros((CF_HALO, D_MODEL), F32)

    @pl.when(jnp.logical_not(first_of_batch))
    def _():
        ext_next_ref[0:CF_HALO, :] = ext_ref[TM:TM + CF_HALO, :]

    n_chunks = FF_DIM // FF_TILE
    n_pieces = TM // CF_PIECE
    conv = [None] * n_pieces
    todo = [(res, piece) for res in range(SUBLANES) for piece in range(n_pieces)]
    slots_seen = [0]

    def conv_pieces_after(result):
        for blk in range(n_pieces):
            slots_seen[0] += 1
            if slots_seen[0] % 5 == 0 or not todo:
                continue
            res, piece = todo.pop(0)
            r0 = blk * CF_PIECE
            part = _cf_conv_piece(ext_ref, dw_ref, res, piece, result[r0:r0 + 1, 0:1])
            conv[piece] = part if conv[piece] is None else conv[piece] + part

    y = _dot(u_ref[...], wo_ref[...])
    conv_pieces_after(y)

    h = (x_ref[...] * (1.0 + sc1_ref[0, 0]) + sh1_ref[0, 0]).astype(BF16)
    val = _dot(h, win_ref[:, 0:D_MODEL])
    conv_pieces_after(val)
    gate = _dot(h, win_ref[:, D_MODEL:2 * D_MODEL])
    conv_pieces_after(gate)
    ext_next_ref[CF_HALO:CF_HALO + TM, :] = val * jax.nn.sigmoid(gate)

    x1 = _layer_norm(ALPHA * xres_ref[...] + (1.0 + gt1_ref[0, 0]) * y, lng_ref[0:1, :], lnb_ref[0:1, :])
    h2 = (x1 * (1.0 + sc2_ref[0, 0]) + sh2_ref[0, 0]).astype(BF16)
    chunk_cols = [slice(c * FF_TILE, (c + 1) * FF_TILE) for c in range(n_chunks)]
    d1 = _dot(h2, w1_ref[:, chunk_cols[0]])
    conv_pieces_after(d1)
    acc = None
    for c in range(n_chunks):
        d1_next = None
        if c + 1 < n_chunks:
            d1_next = _dot(h2, w1_ref[:, chunk_cols[c + 1]])
            conv_pieces_after(d1_next)
        a = jnp.maximum(d1, 0.0)
        p = _dot((a * a).astype(BF16), w2_ref[chunk_cols[c], :])
        conv_pieces_after(p)
        acc = p if acc is None else acc + p
        d1 = d1_next
    assert not todo
    o_ref[...] = _layer_norm(ALPHA * x1 + (1.0 + gt2_ref[0, 0]) * acc, lng_ref[1:2, :], lnb_ref[1:2, :])

    cu = _layer_norm(jnp.concatenate(conv, axis=0) + dwb_ref[...], cg_ref[...], cb_ref[...])
    u_next_ref[...] = _silu(cu).astype(BF16)

    u_ref[...] = u_next_ref[...]
    ext_ref[...] = ext_next_ref[...]


def _cf_layer(x2d, mods, layer, mixer, tiles_per_batch, w_in, dw_w, dw_b, cf_g, cf_b, w_out, ln_g, ln_b,
              w1, w2):
    rows = x2d.shape[0]
    n_tiles = rows // TM
    base = layer * N_MOD
    in_tile = lambda s: jnp.minimum(s, n_tiles - 1)
    mlp_tile = lambda s: jnp.maximum(s - 2, 0)

    def mod_spec(row, tile_of_step):
        return pl.BlockSpec((1, 1, 1, D_MODEL), lambda s: (row, tile_of_step(s) // tiles_per_batch, 0, 0))

    ext_rows = CF_HALO + TM + SUBLANES
    return pl.pallas_call(
        functools.partial(_cf_layer_kernel, n_tiles=n_tiles, tiles_per_batch=tiles_per_batch),
        grid=(n_tiles + 2,),
        in_specs=[
            pl.BlockSpec((TM, D_MODEL), lambda s: (in_tile(s), 0)),
            pl.BlockSpec((TM, D_MODEL), lambda s: (mlp_tile(s), 0)),
            mod_spec(base + 0, in_tile), mod_spec(base + 1, in_tile),
            mod_spec(base + 2, mlp_tile), mod_spec(base + 3, mlp_tile),
            mod_spec(base + 4, mlp_tile), mod_spec(base + 5, mlp_tile),
            _const_spec((D_MODEL, 2 * D_MODEL), mixer), _const_spec((CF_KERNEL, D_MODEL)),
            _const_spec((1, D_MODEL)), _const_spec((1, D_MODEL)), _const_spec((1, D_MODEL)),
            _const_spec((D_MODEL, D_MODEL), mixer), _const_spec((2, D_MODEL)), _const_spec((2, D_MODEL)),
            _const_spec((D_MODEL, FF_DIM), layer), _const_spec((FF_DIM, D_MODEL), layer),
        ],
        out_specs=pl.BlockSpec((TM, D_MODEL), lambda s: (mlp_tile(s), 0)),
        out_shape=jax.ShapeDtypeStruct((rows, D_MODEL), F32),
        scratch_shapes=[pltpu.VMEM((ext_rows, D_MODEL), F32),
                        pltpu.VMEM((ext_rows, D_MODEL), F32),
                        pltpu.VMEM((TM, D_MODEL), BF16),
                        pltpu.VMEM((TM, D_MODEL), BF16)],
        compiler_params=pltpu.CompilerParams(dimension_semantics=("arbitrary",),
                                             vmem_limit_bytes=VMEM_LIMIT),
        name="cf_layer",
    )(x2d, x2d, mods, mods, mods, mods, mods, mods, w_in, dw_w, dw_b, cf_g, cf_b, w_out, ln_g, ln_b, w1, w2)


def kernel(x, c, ada_w, ada_b, ln_g, ln_b, dn_w_in, dn_conv_w, dn_a_log, dn_dt_bias, dn_norm_w,
           dn_w_out, cf_w_in, cf_dw_w, cf_dw_b, cf_ln_g, cf_ln_b, cf_w_out, ff_w1, ff_w2):
    batch, seq, d = x.shape
    assert d == D_MODEL and seq % TM == 0 and TM % CHUNK == 0
    tiles_per_batch = seq // TM
    n_qkv = 3 * D_MODEL
    n_main = 4 * D_MODEL

    mods = _ada_mods(c, ada_w, ada_b)
    w1, w2 = _cast_stack(ff_w1), _cast_stack(ff_w2)
    dn_main, dn_out = _cast_stack(dn_w_in, cols=n_main), _cast_stack(dn_w_out)
    cf_in, cf_out = _cast_stack(cf_w_in), _cast_stack(cf_w_out)
    x2d = x.reshape(batch * seq, D_MODEL)
    for i in range(DEPTH):
        j = i // 2
        if i % 2 == 0:
            w_ba = jnp.pad(dn_w_in[j, :, n_main:], ((0, 0), (0, LANES - 2 * HEADS)))
            gate_pad = (HEADS, LANES - 2 * HEADS)
            alog_vec = jnp.pad(dn_a_log[j], gate_pad).reshape(1, LANES)
            dtb_vec = jnp.pad(dn_dt_bias[j], gate_pad).reshape(1, LANES)
            q, k, v, z, bg = _dn_in(x2d, mods, i, j, tiles_per_batch, dn_main, w_ba,
                                    dn_conv_w[j][:, :n_qkv], alog_vec, dtb_vec)
            u = _dn_core(q, k, v, z, bg, dn_norm_w[j].reshape(1, HEAD_DIM), tiles_per_batch)
            x2d = _post(x2d, u, mods, i, j, tiles_per_batch, dn_out, ln_g[i], ln_b[i], w1, w2)
        else:
            x2d = _cf_layer(x2d, mods, i, j, tiles_per_batch, cf_in, cf_dw_w[j],
                            cf_dw_b[j].reshape(1, D_MODEL), cf_ln_g[j].reshape(1, D_MODEL),
                            cf_ln_b[j].reshape(1, D_MODEL), cf_out, ln_g[i], ln_b[i], w1, w2)
    return x2d.reshape(batch, seq, D_MODEL)
```

```python
import functools

import jax
import jax.numpy as jnp
from jax import lax
from jax.experimental import pallas as pl
from jax.experimental.pallas import tpu as pltpu

F32 = jnp.float32
BF16 = jnp.bfloat16

D_MODEL = 1024
DEPTH = 4
N_MOD = 6
HEADS = 8
HEAD_DIM = 128
DN_CONV = 4
CHUNK = 64
BLOCK16 = 16
CF_KERNEL = 31
FF_DIM = 4 * D_MODEL
ALPHA = (2.0 * DEPTH) ** 0.25
LN_EPS = 1e-5
RMS_EPS = 1e-6
L2_EPS = 1e-6

LANES = 128
SUBLANES = 8
TM = 512
FF_TILE = 1024
CF_HALO = 32
CF_PIECE = 128
DN_HALO = 8
PASS1_CHUNKS = 4
VMEM_LIMIT = 56 * 1024 * 1024
CAST_BLOCK_ELEMS = 1024 * 1024


def _dot(a, b):
    return jnp.dot(a, b, preferred_element_type=F32)


def _dot_nt(a, b):
    return lax.dot_general(a, b, (((1,), (1,)), ((), ())), preferred_element_type=F32)


def _dot_tn(a, b):
    return lax.dot_general(a, b, (((0,), (0,)), ((), ())), preferred_element_type=F32)


def _bdot(a, b):
    return _dot(a.astype(BF16), b.astype(BF16))


def _split3(x):
    x1 = x.astype(BF16)
    r1 = x - x1.astype(F32)
    x2 = r1.astype(BF16)
    x3 = (r1 - x2.astype(F32)).astype(BF16)
    return x1, x2, x3


def _dot_exact_lhs(a_bf16, x):
    x1, x2, x3 = _split3(x)
    return _dot(a_bf16, x3) + _dot(a_bf16, x2) + _dot(a_bf16, x1)


def _layer_norm(r, g, b):
    mu = jnp.mean(r, axis=-1, keepdims=True)
    d = r - mu
    var = jnp.mean(d * d, axis=-1, keepdims=True)
    return d * lax.rsqrt(var + LN_EPS) * g + b


def _silu(x):
    return x * jax.nn.sigmoid(x)


def _softplus(x):
    return jnp.maximum(x, 0.0) + jnp.log1p(jnp.exp(-jnp.abs(x)))


def _zero_after(v):
    bits = lax.bitcast_convert_type(v, jnp.uint32)
    z = lax.shift_right_logical(lax.shift_right_logical(bits, jnp.uint32(16)), jnp.uint32(16))
    return z[0, 0].astype(jnp.int32)


def _const_spec(shape, layer=None):
    nd = len(shape)
    if layer is None:
        return pl.BlockSpec(shape, lambda *_: (0,) * nd, pipeline_mode=pl.Buffered(1))
    return pl.BlockSpec((None,) + tuple(shape), lambda *_: (layer,) + (0,) * nd,
                        pipeline_mode=pl.Buffered(1))


def _mod_spec(row, tiles_per_batch):
    return pl.BlockSpec((1, 1, 1, D_MODEL), lambda t: (row, t // tiles_per_batch, 0, 0))


def _ada_kernel(c_ref, w_ref, b_ref, o_ref):
    cond = _silu(c_ref[...])
    o_ref[0] = _dot(cond, w_ref[0]) + b_ref[0]


def _ada_mods(c, ada_w, ada_b):
    batch = c.shape[0]
    out = pl.pallas_call(
        _ada_kernel,
        grid=(DEPTH, N_MOD),
        in_specs=[
            pl.BlockSpec((batch, D_MODEL), lambda i, j: (0, 0)),
            pl.BlockSpec((1, D_MODEL, D_MODEL), lambda i, j: (i, 0, j)),
            pl.BlockSpec((1, 1, D_MODEL), lambda i, j: (i, 0, j)),
        ],
        out_specs=pl.BlockSpec((1, batch, D_MODEL), lambda i, j: (i * N_MOD + j, 0, 0)),
        out_shape=jax.ShapeDtypeStruct((DEPTH * N_MOD, batch, D_MODEL), F32),
        compiler_params=pltpu.CompilerParams(dimension_semantics=("arbitrary", "arbitrary")),
        name="ada_mods",
    )(c, ada_w, ada_b.reshape(DEPTH, 1, N_MOD * D_MODEL))
    return out.reshape(DEPTH * N_MOD, batch, 1, D_MODEL)


def _cast_kernel(w_ref, o_ref):
    o_ref[...] = w_ref[:, 0:o_ref.shape[-1]].astype(BF16)


def _cast_stack(w, cols=None):
    n_layers, rows, all_cols = w.shape
    cols = all_cols if cols is None else cols
    block_rows = rows
    while block_rows * all_cols > CAST_BLOCK_ELEMS:
        block_rows //= 2
    assert rows % block_rows == 0 and block_rows % SUBLANES == 0 and cols % LANES == 0
    return pl.pallas_call(
        _cast_kernel,
        grid=(n_layers, rows // block_rows),
        in_specs=[pl.BlockSpec((None, block_rows, all_cols), lambda l, i: (l, i, 0))],
        out_specs=pl.BlockSpec((None, block_rows, cols), lambda l, i: (l, i, 0)),
        out_shape=jax.ShapeDtypeStruct((n_layers, rows, cols), BF16),
        compiler_params=pltpu.CompilerParams(dimension_semantics=("arbitrary", "arbitrary")),
        name="cast_bf16",
    )(w)


def _post_kernel(x_ref, u_ref, gt1_ref, sh2_ref, sc2_ref, gt2_ref, wo_ref, lng_ref, lnb_ref,
                 w1_ref, w2_ref, o_ref):
    x = x_ref[...]
    y = _dot(u_ref[...], wo_ref[...])
    x1 = _layer_norm(ALPHA * x + (1.0 + gt1_ref[0, 0]) * y, lng_ref[0:1, :], lnb_ref[0:1, :])
    h = (x1 * (1.0 + sc2_ref[0, 0]) + sh2_ref[0, 0]).astype(BF16)
    n_chunks = FF_DIM // FF_TILE
    chunk_cols = [slice(c * FF_TILE, (c + 1) * FF_TILE) for c in range(n_chunks)]
    d1 = _dot(h, w1_ref[:, chunk_cols[0]])
    acc = None
    for c in range(n_chunks):
        d1_next = _dot(h, w1_ref[:, chunk_cols[c + 1]]) if c + 1 < n_chunks else None
        a = jnp.maximum(d1, 0.0)
        p = _dot((a * a).astype(BF16), w2_ref[chunk_cols[c], :])
        acc = p if acc is None else acc + p
        d1 = d1_next
    o_ref[...] = _layer_norm(ALPHA * x1 + (1.0 + gt2_ref[0, 0]) * acc, lng_ref[1:2, :], lnb_ref[1:2, :])


def _post(x2d, u2d, mods, layer, mixer, tiles_per_batch, w_out, ln_g, ln_b, w1, w2):
    rows = x2d.shape[0]
    row_spec = pl.BlockSpec((TM, D_MODEL), lambda t: (t, 0))
    base = layer * N_MOD
    return pl.pallas_call(
        _post_kernel,
        grid=(rows // TM,),
        in_specs=[
            row_spec, row_spec,
            _mod_spec(base + 2, tiles_per_batch), _mod_spec(base + 3, tiles_per_batch),
            _mod_spec(base + 4, tiles_per_batch), _mod_spec(base + 5, tiles_per_batch),
            _const_spec((D_MODEL, D_MODEL), mixer), _const_spec((2, D_MODEL)), _const_spec((2, D_MODEL)),
            _const_spec((D_MODEL, FF_DIM), layer), _const_spec((FF_DIM, D_MODEL), layer),
        ],
        out_specs=row_spec,
        out_shape=jax.ShapeDtypeStruct((rows, D_MODEL), F32),
        compiler_params=pltpu.CompilerParams(dimension_semantics=("arbitrary",),
                                             vmem_limit_bytes=VMEM_LIMIT),
        name="post_mlp",
    )(x2d, u2d, mods, mods, mods, mods, w_out, ln_g, ln_b, w1, w2)


def _dn_in_kernel(x_ref, sh_ref, sc_ref, w_ref, wba_ref, cw_ref, alog_ref, dtb_ref,
                  q_ref, k_ref, v_ref, z_ref, bg_ref, ext_ref, *, tiles_per_batch):
    t = pl.program_id(0)

    @pl.when(t % tiles_per_batch == 0)
    def _():
        ext_ref[0:DN_HALO, :] = jnp.zeros((DN_HALO, 3 * D_MODEL), F32)

    h = (x_ref[...] * (1.0 + sc_ref[0, 0]) + sh_ref[0, 0]).astype(BF16)

    for grp, out_ref in enumerate((q_ref, k_ref, v_ref)):
        cols = slice(grp * D_MODEL, (grp + 1) * D_MODEL)
        ext_ref[DN_HALO:DN_HALO + TM, cols] = _dot(h, w_ref[:, cols])
        ext = ext_ref[:, cols]
        acc = None
        for j in range(DN_CONV):
            off = DN_HALO - (DN_CONV - 1) + j
            res = off % SUBLANES
            shifted = ext if res == 0 else pltpu.roll(ext, DN_HALO + TM - res, axis=0)
            term = cw_ref[j:j + 1, cols] * shifted[off - res:off - res + TM, :]
            acc = term if acc is None else acc + term
        ext_ref[0:DN_HALO, cols] = ext_ref[TM:TM + DN_HALO, cols]
        y = _silu(acc)
        if out_ref is v_ref:
            out_ref[...] = y
        else:
            for hd in range(HEADS):
                lanes = slice(hd * HEAD_DIM, (hd + 1) * HEAD_DIM)
                ys = y[:, lanes]
                ss = jnp.sum(ys * ys, axis=-1, keepdims=True)
                out_ref[:, lanes] = ys * lax.rsqrt(ss + L2_EPS)

    z_ref[...] = _dot(h, w_ref[:, 3 * D_MODEL:4 * D_MODEL])

    ba = _dot(h, wba_ref[...].astype(BF16))
    beta = jax.nn.sigmoid(ba)
    g = -jnp.exp(alog_ref[...]) * _softplus(ba + dtb_ref[...])
    lane = lax.broadcasted_iota(jnp.int32, ba.shape, 1)
    bg_ref[...] = jnp.where(lane < HEADS, beta, g)


def _dn_in(x2d, mods, layer, mixer, tiles_per_batch, w_main, w_ba, conv_w, alog_vec, dtb_vec):
    rows = x2d.shape[0]
    row_spec = pl.BlockSpec((TM, D_MODEL), lambda t: (t, 0))
    base = layer * N_MOD
    out_sds = jax.ShapeDtypeStruct((rows, D_MODEL), F32)
    return pl.pallas_call(
        functools.partial(_dn_in_kernel, tiles_per_batch=tiles_per_batch),
        grid=(rows // TM,),
        in_specs=[
            row_spec, _mod_spec(base + 0, tiles_per_batch), _mod_spec(base + 1, tiles_per_batch),
            _const_spec((D_MODEL, 4 * D_MODEL), mixer), _const_spec((D_MODEL, LANES)),
            _const_spec((DN_CONV, 3 * D_MODEL)), _const_spec((1, LANES)), _const_spec((1, LANES)),
        ],
        out_specs=[row_spec, row_spec, row_spec, row_spec, pl.BlockSpec((TM, LANES), lambda t: (t, 0))],
        out_shape=[out_sds, out_sds, out_sds, out_sds, jax.ShapeDtypeStruct((rows, LANES), F32)],
        scratch_shapes=[pltpu.VMEM((DN_HALO + TM, 3 * D_MODEL), F32)],
        compiler_params=pltpu.CompilerParams(dimension_semantics=("arbitrary",),
                                             vmem_limit_bytes=VMEM_LIMIT),
        name="dn_in",
    )(x2d, mods, mods, w_main, w_ba, conv_w, alog_vec, dtb_vec)


def _inverse_minus_eye_levels(a_list, out):
    ri = lax.broadcasted_iota(jnp.int32, (CHUNK, CHUNK), 0) // BLOCK16
    ci = lax.broadcasted_iota(jnp.int32, (CHUNK, CHUNK), 1) // BLOCK16
    on_diag = ri == ci
    b16 = lambda xs: [x.astype(BF16) for x in xs]
    mm = lambda xs, ys: [_dot(x, y) for x, y in zip(xs, ys)]

    d = [jnp.where(on_diag, a, 0.0) for a in a_list]
    low = [a - x for a, x in zip(a_list, d)]
    d_16 = b16(d)
    d2 = mm(d_16, d_16)
    yield
    d2_16 = b16(d2)
    d4 = mm(d2_16, d2_16)
    dd2 = mm(d_16, d2_16)
    yield
    p = [x2 - x - y for x, x2, y in zip(d, d2, dd2)]
    d4_16 = b16(d4)
    d8 = mm(d4_16, d4_16)
    pd4 = mm(b16(p), d4_16)
    yield
    p = [x + y + z for x, y, z in zip(p, d4, pd4)]
    pd8 = mm(b16(p), b16(d8))
    yield
    p = [x + y + z for x, y, z in zip(p, d8, pd8)]
    p_16 = b16(p)
    pl_ = mm(p_16, b16(low))
    yield
    m = [x + y for x, y in zip(low, pl_)]
    m_16 = b16(m)
    m2 = mm(m_16, m_16)
    yield
    mm2 = mm(m_16, b16(m2))
    yield
    q = [x2 - x - y for x, x2, y in zip(m, m2, mm2)]
    qp = mm(b16(q), p_16)
    yield
    out.extend(x + y + z for x, y, z in zip(q, p, qp))


def _dn_core_kernel(q_ref, k_ref, v_ref, z_ref, bg_ref, nw_ref, o_ref,
                    state_ref, gam_ref, egam_ref, edec_ref, egl_ref,
                    u_ref, w_ref, qd_ref, kdt_ref, a_ref, *, tiles_per_batch):
    s = pl.program_id(0)
    cur = s % 2
    prev = 1 - cur

    @pl.when(s == 0)
    def _():
        for ref in (egl_ref, u_ref, w_ref, qd_ref, kdt_ref, a_ref):
            ref[1] = jnp.zeros(ref.shape[1:], ref.dtype)

    @pl.when(jnp.maximum(s - 1, 0) % tiles_per_batch == 0)
    def _():
        state_ref[...] = jnp.zeros(state_ref.shape, F32)

    r_idx = lax.broadcasted_iota(jnp.int32, (CHUNK, CHUNK), 0)
    c_idx = lax.broadcasted_iota(jnp.int32, (CHUNK, CHUNK), 1)
    causal = c_idx <= r_idx
    strict = c_idx < r_idx

    cum_mat = jnp.where(causal, 1.0, 0.0).astype(BF16)
    for c in range(TM // CHUNK):
        rows = slice(c * CHUNK, (c + 1) * CHUNK)
        gam = _dot_exact_lhs(cum_mat, bg_ref[rows, :])
        gtot = jnp.broadcast_to(gam[CHUNK - 1:CHUNK, :], gam.shape)
        gam_ref[rows, :] = gam
        egam_ref[rows, :] = jnp.exp(gam)
        edec_ref[rows, :] = jnp.exp(gtot - gam)
        egl_ref[cur, rows, :] = jnp.exp(gtot)
    scale = HEAD_DIM ** -0.5
    wide = (CHUNK, HEAD_DIM)
    nw = nw_ref[...]

    def independent_levels(grp):
        probs = []
        for ci_ in range(PASS1_CHUNKS):
            cidx = grp * PASS1_CHUNKS + ci_
            rows = pl.ds(pl.multiple_of(cidx * CHUNK, CHUNK), CHUNK)
            gam_c = gam_ref[rows, :]
            gam_t = gam_c.T
            bg_c = bg_ref[rows, :]
            egam_c = egam_ref[rows, :]
            edec_c = edec_ref[rows, :]
            for hd in range(HEADS):
                lanes = slice(hd * HEAD_DIM, (hd + 1) * HEAD_DIM)
                gl = HEADS + hd
                beta = jnp.broadcast_to(bg_c[:, hd:hd + 1], wide)
                eg = jnp.broadcast_to(egam_c[:, gl:gl + 1], wide)
                ed = jnp.broadcast_to(edec_c[:, gl:gl + 1], wide)
                gcol = jnp.broadcast_to(gam_c[:, gl:gl + 1], wide)[:, :CHUNK]
                decay = jnp.exp(jnp.where(causal, gcol - gam_t[gl:gl + 1, :], -jnp.inf))
                probs.append(dict(cidx=cidx, rows=rows, hd=hd, lanes=lanes, beta=beta, eg=eg, ed=ed,
                                  decay=decay, q=q_ref[rows, lanes], k=k_ref[rows, lanes],
                                  v=v_ref[rows, lanes]))
        for p in probs:
            p["k16"] = p["k"].astype(BF16)
            p["kq16"] = jnp.concatenate([p["k16"], (p["q"] * scale).astype(BF16)], axis=0)
        kkqk = [_dot_nt(p["kq16"], p["k16"]) for p in probs]
        yield
        a_list = []
        for p, kk in zip(probs, kkqk):
            a_list.append(jnp.where(strict, kk[:CHUNK] * p["beta"][:, :CHUNK] * p["decay"], 0.0))
            a_ref[cur, p["hd"], p["rows"], :] = (kk[CHUNK:] * p["decay"]).astype(BF16)
        tinv = []
        yield from _inverse_minus_eye_levels(a_list, tinv)
        for p in probs:
            p["vb"] = p["v"] * p["beta"]
            p["kbg"] = p["k"] * p["beta"] * p["eg"]
        uw = [_dot(t.astype(BF16), jnp.concatenate([p["vb"], p["kbg"]], axis=1).astype(BF16))
              for p, t in zip(probs, tinv)]
        yield
        for p, r in zip(probs, uw):
            rows, lanes = p["rows"], p["lanes"]
            u_ref[cur, rows, lanes] = p["vb"] + r[:, :HEAD_DIM]
            w_ref[cur, rows, lanes] = (p["kbg"] + r[:, HEAD_DIM:]).astype(BF16)
            qd_ref[cur, rows, lanes] = (p["q"] * (p["eg"] * scale)).astype(BF16)
            kdt_ref[cur, p["hd"], p["cidx"]] = (p["k"] * p["ed"]).T.astype(BF16)

    def recurrent_levels(grp):
        heads = range(HEADS)
        lanes = [slice(hd * HEAD_DIM, (hd + 1) * HEAD_DIM) for hd in heads]
        for ci_ in range(PASS1_CHUNKS):
            c = grp * PASS1_CHUNKS + ci_
            rows = pl.ds(pl.multiple_of(c * CHUNK, CHUNK), CHUNK)
            egl_c = egl_ref[prev, rows, :]
            state = [state_ref[hd] for hd in heads]
            wq = [jnp.concatenate([w_ref[prev, rows, lanes[hd]], qd_ref[prev, rows, lanes[hd]]], axis=0)
                  for hd in heads]
            r = [_dot(wq[hd], state[hd].astype(BF16)) for hd in heads]
            yield
            vn16 = [(u_ref[prev, rows, lanes[hd]] - r[hd][:CHUNK]).astype(BF16) for hd in heads]
            av = [_dot(a_ref[prev, hd, rows, :], vn16[hd]) for hd in heads]
            kv = [_dot(kdt_ref[prev, hd, c], vn16[hd]) for hd in heads]
            yield
            for hd in heads:
                state_ref[hd] = state[hd] * egl_c[0:1, HEADS + hd:HEADS + hd + 1] + kv[hd]
                o = r[hd][CHUNK:] + av[hd]
                on = o * lax.rsqrt(jnp.mean(o * o, axis=-1, keepdims=True) + RMS_EPS) * nw
                o_ref[rows, lanes[hd]] = (on * _silu(z_ref[rows, lanes[hd]])).astype(BF16)

    def group_body(grp, carry):
        pending = [recurrent_levels(grp), independent_levels(grp)]
        while pending:
            for gen in list(pending):
                try:
                    next(gen)
                except StopIteration:
                    pending.remove(gen)
        return carry

    lax.fori_loop(0, TM // CHUNK // PASS1_CHUNKS, group_body, 0)


def _dn_core(q, k, v, z, bg, norm_w, tiles_per_batch):
    rows = q.shape[0]
    n_tiles = rows // TM
    in_tile = lambda s: jnp.minimum(s, n_tiles - 1)
    out_tile = lambda s: jnp.maximum(s - 1, 0)
    in_rows = pl.BlockSpec((TM, D_MODEL), lambda s: (in_tile(s), 0))
    out_rows = pl.BlockSpec((TM, D_MODEL), lambda s: (out_tile(s), 0))
    small = pltpu.VMEM((TM, LANES), F32)
    return pl.pallas_call(
        functools.partial(_dn_core_kernel, tiles_per_batch=tiles_per_batch),
        grid=(n_tiles + 1,),
        in_specs=[in_rows, in_rows, in_rows, out_rows,
                  pl.BlockSpec((TM, LANES), lambda s: (in_tile(s), 0)),
                  pl.BlockSpec((1, HEAD_DIM), lambda s: (0, 0))],
        out_specs=out_rows,
        out_shape=jax.ShapeDtypeStruct((rows, D_MODEL), BF16),
        scratch_shapes=[
            pltpu.VMEM((HEADS, HEAD_DIM, HEAD_DIM), F32),
            small, small, small,
            pltpu.VMEM((2, TM, LANES), F32),
            pltpu.VMEM((2, TM, D_MODEL), F32),
            pltpu.VMEM((2, TM, D_MODEL), BF16),
            pltpu.VMEM((2, TM, D_MODEL), BF16),
            pltpu.VMEM((2, HEADS, TM // CHUNK, HEAD_DIM, CHUNK), BF16),
            pltpu.VMEM((2, HEADS, TM, CHUNK), BF16),
        ],
        compiler_params=pltpu.CompilerParams(dimension_semantics=("arbitrary",),
                                             vmem_limit_bytes=VMEM_LIMIT),
        name="dn_core",
    )(q, k, v, z, bg, norm_w)


def _cf_conv_piece(ext_ref, dw_ref, res, piece, not_before):
    n = CF_PIECE + CF_HALO + SUBLANES
    start = pl.multiple_of(piece * CF_PIECE + _zero_after(not_before), SUBLANES)
    window = ext_ref[pl.ds(start, n), :]
    shifted = window if res == 0 else pltpu.roll(window, n - res, axis=0)
    acc = None
    for j in range(CF_KERNEL):
        off = CF_HALO - (CF_KERNEL - 1) + j
        if off % SUBLANES != res:
            continue
        term = dw_ref[j:j + 1, :] * shifted[off - res:off - res + CF_PIECE, :]
        acc = term if acc is None else acc + term
    return acc


def _cf_layer_kernel(x_ref, xres_ref, sh1_ref, sc1_ref, gt1_ref, sh2_ref, sc2_ref, gt2_ref,
                     win_ref, dw_ref, dwb_ref, cg_ref, cb_ref, wo_ref, lng_ref, lnb_ref, w1_ref, w2_ref,
                     o_ref, ext_ref, ext_next_ref, u_ref, u_next_ref, *, n_tiles, tiles_per_batch):
    s = pl.program_id(0)
    t_in = jnp.minimum(s, n_tiles - 1)
    first_of_batch = t_in % tiles_per_batch == 0

    @pl.when(s == 0)
    def _():
        ext_ref[...] = jnp.zeros(ext_ref.shape, F32)
        u_ref[...] = jnp.zeros(u_ref.shape, BF16)
        ext_next_ref[CF_HALO + TM:CF_HALO + TM + SUBLANES, :] = jnp.zeros((SUBLANES, D_MODEL), F32)

    @pl.when(first_of_batch)
    def _():
        ext_next_ref[0:CF_HALO, :] = jnp.zeros((CF_HALO, D_MODEL), F32)

    @pl.when(jnp.logical_not(first_of_batch))
    def _():
        ext_next_ref[0:CF_HALO, :] = ext_ref[TM:TM + CF_HALO, :]

    n_chunks = FF_DIM // FF_TILE
    n_pieces = TM // CF_PIECE
    conv = [None] * n_pieces
    todo = [(res, piece) for res in range(SUBLANES) for piece in range(n_pieces)]
    slots_seen = [0]

    def conv_pieces_after(result):
        for blk in range(n_pieces):
            slots_seen[0] += 1
            if slots_seen[0] % 5 == 0 or not todo:
                continue
            res, piece = todo.pop(0)
            r0 = blk * CF_PIECE
            part = _cf_conv_piece(ext_ref, dw_ref, res, piece, result[r0:r0 + 1, 0:1])
            conv[piece] = part if conv[piece] is None else conv[piece] + part

    y = _dot(u_ref[...], wo_ref[...])
    conv_pieces_after(y)

    h = (x_ref[...] * (1.0 + sc1_ref[0, 0]) + sh1_ref[0, 0]).astype(BF16)
    val = _dot(h, win_ref[:, 0:D_MODEL])
    conv_pieces_after(val)
    gate = _dot(h, win_ref[:, D_MODEL:2 * D_MODEL])
    conv_pieces_after(gate)
    ext_next_ref[CF_HALO:CF_HALO + TM, :] = val * jax.nn.sigmoid(gate)

    x1 = _layer_norm(ALPHA * xres_ref[...] + (1.0 + gt1_ref[0, 0]) * y, lng_ref[0:1, :], lnb_ref[0:1, :])
    h2 = (x1 * (1.0 + sc2_ref[0, 0]) + sh2_ref[0, 0]).astype(BF16)
    chunk_cols = [slice(c * FF_TILE, (c + 1) * FF_TILE) for c in range(n_chunks)]
    d1 = _dot(h2, w1_ref[:, chunk_cols[0]])
    conv_pieces_after(d1)
    acc = None
    for c in range(n_chunks):
        d1_next = None
        if c + 1 < n_chunks:
            d1_next = _dot(h2, w1_ref[:, chunk_cols[c + 1]])
            conv_pieces_after(d1_next)
        a = jnp.maximum(d1, 0.0)
        p = _dot((a * a).astype(BF16), w2_ref[chunk_cols[c], :])
        conv_pieces_after(p)
        acc = p if acc is None else acc + p
        d1 = d1_next
    assert not todo
    o_ref[...] = _layer_norm(ALPHA * x1 + (1.0 + gt2_ref[0, 0]) * acc, lng_ref[1:2, :], lnb_ref[1:2, :])

    cu = _layer_norm(jnp.concatenate(conv, axis=0) + dwb_ref[...], cg_ref[...], cb_ref[...])
    u_next_ref[...] = _silu(cu).astype(BF16)

    u_ref[...] = u_next_ref[...]
    ext_ref[...] = ext_next_ref[...]


def _cf_layer(x2d, mods, layer, mixer, tiles_per_batch, w_in, dw_w, dw_b, cf_g, cf_b, w_out, ln_g, ln_b,
              w1, w2):
    rows = x2d.shape[0]
    n_tiles = rows // TM
    base = layer * N_MOD
    in_tile = lambda s: jnp.minimum(s, n_tiles - 1)
    mlp_tile = lambda s: jnp.maximum(s - 2, 0)

    def mod_spec(row, tile_of_step):
        return pl.BlockSpec((1, 1, 1, D_MODEL), lambda s: (row, tile_of_step(s) // tiles_per_batch, 0, 0))

    ext_rows = CF_HALO + TM + SUBLANES
    return pl.pallas_call(
        functools.partial(_cf_layer_kernel, n_tiles=n_tiles, tiles_per_batch=tiles_per_batch),
        grid=(n_tiles + 2,),
        in_specs=[
            pl.BlockSpec((TM, D_MODEL), lambda s: (in_tile(s), 0)),
            pl.BlockSpec((TM, D_MODEL), lambda s: (mlp_tile(s), 0)),
            mod_spec(base + 0, in_tile), mod_spec(base + 1, in_tile),
            mod_spec(base + 2, mlp_tile), mod_spec(base + 3, mlp_tile),
            mod_spec(base + 4, mlp_tile), mod_spec(base + 5, mlp_tile),
            _const_spec((D_MODEL, 2 * D_MODEL), mixer), _const_spec((CF_KERNEL, D_MODEL)),
            _const_spec((1, D_MODEL)), _const_spec((1, D_MODEL)), _const_spec((1, D_MODEL)),
            _const_spec((D_MODEL, D_MODEL), mixer), _const_spec((2, D_MODEL)), _const_spec((2, D_MODEL)),
            _const_spec((D_MODEL, FF_DIM), layer), _const_spec((FF_DIM, D_MODEL), layer),
        ],
        out_specs=pl.BlockSpec((TM, D_MODEL), lambda s: (mlp_tile(s), 0)),
        out_shape=jax.ShapeDtypeStruct((rows, D_MODEL), F32),
        scratch_shapes=[pltpu.VMEM((ext_rows, D_MODEL), F32),
                        pltpu.VMEM((ext_rows, D_MODEL), F32),
                        pltpu.VMEM((TM, D_MODEL), BF16),
                        pltpu.VMEM((TM, D_MODEL), BF16)],
        compiler_params=pltpu.CompilerParams(dimension_semantics=("arbitrary",),
                                             vmem_limit_bytes=VMEM_LIMIT),
        name="cf_layer",
    )(x2d, x2d, mods, mods, mods, mods, mods, mods, w_in, dw_w, dw_b, cf_g, cf_b, w_out, ln_g, ln_b, w1, w2)


def kernel(x, c, ada_w, ada_b, ln_g, ln_b, dn_w_in, dn_conv_w, dn_a_log, dn_dt_bias, dn_norm_w,
           dn_w_out, cf_w_in, cf_dw_w, cf_dw_b, cf_ln_g, cf_ln_b, cf_w_out, ff_w1, ff_w2):
    batch, seq, d = x.shape
    assert d == D_MODEL and seq % TM == 0 and TM % CHUNK == 0
    tiles_per_batch = seq // TM
    n_qkv = 3 * D_MODEL
    n_main = 4 * D_MODEL

    mods = _ada_mods(c, ada_w, ada_b)
    w1, w2 = _cast_stack(ff_w1), _cast_stack(ff_w2)
    dn_main, dn_out = _cast_stack(dn_w_in, cols=n_main), _cast_stack(dn_w_out)
    cf_in, cf_out = _cast_stack(cf_w_in), _cast_stack(cf_w_out)
    x2d = x.reshape(batch * seq, D_MODEL)
    for i in range(DEPTH):
        j = i // 2
        if i % 2 == 0:
            w_ba = jnp.pad(dn_w_in[j, :, n_main:], ((0, 0), (0, LANES - 2 * HEADS)))
            gate_pad = (HEADS, LANES - 2 * HEADS)
            alog_vec = jnp.pad(dn_a_log[j], gate_pad).reshape(1, LANES)
            dtb_vec = jnp.pad(dn_dt_bias[j], gate_pad).reshape(1, LANES)
            q, k, v, z, bg = _dn_in(x2d, mods, i, j, tiles_per_batch, dn_main, w_ba,
                                    dn_conv_w[j][:, :n_qkv], alog_vec, dtb_vec)
            u = _dn_core(q, k, v, z, bg, dn_norm_w[j].reshape(1, HEAD_DIM), tiles_per_batch)
            x2d = _post(x2d, u, mods, i, j, tiles_per_batch, dn_out, ln_g[i], ln_b[i], w1, w2)
        else:
            x2d = _cf_layer(x2d, mods, i, j, tiles_per_batch, cf_in, cf_dw_w[j],
                            cf_dw_b[j].reshape(1, D_MODEL), cf_ln_g[j].reshape(1, D_MODEL),
                            cf_ln_b[j].reshape(1, D_MODEL), cf_out, ln_g[i], ln_b[i], w1, w2)
    return x2d.reshape(batch, seq, D_MODEL)
```

```python
import functools

import jax
import jax.numpy as jnp
from jax import lax
from jax.experimental import pallas as pl
from jax.experimental.pallas import tpu as pltpu

F32 = jnp.float32
BF16 = jnp.bfloat16

D_MODEL = 1024
DEPTH = 4
N_MOD = 6
HEADS = 8
HEAD_DIM = 128
DN_CONV = 4
CHUNK = 64
BLOCK16 = 16
CF_KERNEL = 31
FF_DIM = 4 * D_MODEL
ALPHA = (2.0 * DEPTH) ** 0.25
LN_EPS = 1e-5
RMS_EPS = 1e-6
L2_EPS = 1e-6

LANES = 128
SUBLANES = 8
TM = 512
FF_TILE = 1024
CF_HALO = 32
CF_PIECE = 128
DN_HALO = 8
PASS1_CHUNKS = 2
VMEM_LIMIT = 56 * 1024 * 1024
CAST_BLOCK_ELEMS = 1024 * 1024


def _dot(a, b):
    return jnp.dot(a, b, preferred_element_type=F32)


def _dot_nt(a, b):
    return lax.dot_general(a, b, (((1,), (1,)), ((), ())), preferred_element_type=F32)


def _dot_tn(a, b):
    return lax.dot_general(a, b, (((0,), (0,)), ((), ())), preferred_element_type=F32)


def _bdot(a, b):
    return _dot(a.astype(BF16), b.astype(BF16))


def _split3(x):
    x1 = x.astype(BF16)
    r1 = x - x1.astype(F32)
    x2 = r1.astype(BF16)
    x3 = (r1 - x2.astype(F32)).astype(BF16)
    return x1, x2, x3


def _dot_exact_lhs(a_bf16, x):
    x1, x2, x3 = _split3(x)
    return _dot(a_bf16, x3) + _dot(a_bf16, x2) + _dot(a_bf16, x1)


def _layer_norm(r, g, b):
    mu = jnp.mean(r, axis=-1, keepdims=True)
    d = r - mu
    var = jnp.mean(d * d, axis=-1, keepdims=True)
    return d * lax.rsqrt(var + LN_EPS) * g + b


def _silu(x):
    return x * jax.nn.sigmoid(x)


def _softplus(x):
    return jnp.maximum(x, 0.0) + jnp.log1p(jnp.exp(-jnp.abs(x)))


def _zero_after(v):
    bits = lax.bitcast_convert_type(v, jnp.uint32)
    z = lax.shift_right_logical(lax.shift_right_logical(bits, jnp.uint32(16)), jnp.uint32(16))
    return z[0, 0].astype(jnp.int32)


def _const_spec(shape, layer=None):
    nd = len(shape)
    if layer is None:
        return pl.BlockSpec(shape, lambda *_: (0,) * nd, pipeline_mode=pl.Buffered(1))
    return pl.BlockSpec((None,) + tuple(shape), lambda *_: (layer,) + (0,) * nd,
                        pipeline_mode=pl.Buffered(1))


def _mod_spec(row, tiles_per_batch):
    return pl.BlockSpec((1, 1, 1, D_MODEL), lambda t: (row, t // tiles_per_batch, 0, 0))


def _ada_kernel(c_ref, w_ref, b_ref, o_ref):
    cond = _silu(c_ref[...])
    o_ref[0] = _dot(cond, w_ref[0]) + b_ref[0]


def _ada_mods(c, ada_w, ada_b):
    batch = c.shape[0]
    out = pl.pallas_call(
        _ada_kernel,
        grid=(DEPTH, N_MOD),
        in_specs=[
            pl.BlockSpec((batch, D_MODEL), lambda i, j: (0, 0)),
            pl.BlockSpec((1, D_MODEL, D_MODEL), lambda i, j: (i, 0, j)),
            pl.BlockSpec((1, 1, D_MODEL), lambda i, j: (i, 0, j)),
        ],
        out_specs=pl.BlockSpec((1, batch, D_MODEL), lambda i, j: (i * N_MOD + j, 0, 0)),
        out_shape=jax.ShapeDtypeStruct((DEPTH * N_MOD, batch, D_MODEL), F32),
        compiler_params=pltpu.CompilerParams(dimension_semantics=("arbitrary", "arbitrary")),
        name="ada_mods",
    )(c, ada_w, ada_b.reshape(DEPTH, 1, N_MOD * D_MODEL))
    return out.reshape(DEPTH * N_MOD, batch, 1, D_MODEL)


def _cast_kernel(w_ref, o_ref):
    o_ref[...] = w_ref[:, 0:o_ref.shape[-1]].astype(BF16)


def _cast_stack(w, cols=None):
    n_layers, rows, all_cols = w.shape
    cols = all_cols if cols is None else cols
    block_rows = rows
    while block_rows * all_cols > CAST_BLOCK_ELEMS:
        block_rows //= 2
    assert rows % block_rows == 0 and block_rows % SUBLANES == 0 and cols % LANES == 0
    return pl.pallas_call(
        _cast_kernel,
        grid=(n_layers, rows // block_rows),
        in_specs=[pl.BlockSpec((None, block_rows, all_cols), lambda l, i: (l, i, 0))],
        out_specs=pl.BlockSpec((None, block_rows, cols), lambda l, i: (l, i, 0)),
        out_shape=jax.ShapeDtypeStruct((n_layers, rows, cols), BF16),
        compiler_params=pltpu.CompilerParams(dimension_semantics=("arbitrary", "arbitrary")),
        name="cast_bf16",
    )(w)


def _post_kernel(x_ref, u_ref, gt1_ref, sh2_ref, sc2_ref, gt2_ref, wo_ref, lng_ref, lnb_ref,
                 w1_ref, w2_ref, o_ref):
    x = x_ref[...]
    y = _dot(u_ref[...], wo_ref[...])
    x1 = _layer_norm(ALPHA * x + (1.0 + gt1_ref[0, 0]) * y, lng_ref[0:1, :], lnb_ref[0:1, :])
    h = (x1 * (1.0 + sc2_ref[0, 0]) + sh2_ref[0, 0]).astype(BF16)
    n_chunks = FF_DIM // FF_TILE
    chunk_cols = [slice(c * FF_TILE, (c + 1) * FF_TILE) for c in range(n_chunks)]
    d1 = _dot(h, w1_ref[:, chunk_cols[0]])
    acc = None
    for c in range(n_chunks):
        d1_next = _dot(h, w1_ref[:, chunk_cols[c + 1]]) if c + 1 < n_chunks else None
        a = jnp.maximum(d1, 0.0)
        p = _dot((a * a).astype(BF16), w2_ref[chunk_cols[c], :])
        acc = p if acc is None else acc + p
        d1 = d1_next
    o_ref[...] = _layer_norm(ALPHA * x1 + (1.0 + gt2_ref[0, 0]) * acc, lng_ref[1:2, :], lnb_ref[1:2, :])


def _post(x2d, u2d, mods, layer, mixer, tiles_per_batch, w_out, ln_g, ln_b, w1, w2):
    rows = x2d.shape[0]
    row_spec = pl.BlockSpec((TM, D_MODEL), lambda t: (t, 0))
    base = layer * N_MOD
    return pl.pallas_call(
        _post_kernel,
        grid=(rows // TM,),
        in_specs=[
            row_spec, row_spec,
            _mod_spec(base + 2, tiles_per_batch), _mod_spec(base + 3, tiles_per_batch),
            _mod_spec(base + 4, tiles_per_batch), _mod_spec(base + 5, tiles_per_batch),
            _const_spec((D_MODEL, D_MODEL), mixer), _const_spec((2, D_MODEL)), _const_spec((2, D_MODEL)),
            _const_spec((D_MODEL, FF_DIM), layer), _const_spec((FF_DIM, D_MODEL), layer),
        ],
        out_specs=row_spec,
        out_shape=jax.ShapeDtypeStruct((rows, D_MODEL), F32),
        compiler_params=pltpu.CompilerParams(dimension_semantics=("arbitrary",),
                                             vmem_limit_bytes=VMEM_LIMIT),
        name="post_mlp",
    )(x2d, u2d, mods, mods, mods, mods, w_out, ln_g, ln_b, w1, w2)


def _dn_in_kernel(x_ref, sh_ref, sc_ref, w_ref, wba_ref, cw_ref, alog_ref, dtb_ref,
                  q_ref, k_ref, v_ref, z_ref, bg_ref, ext_ref, *, tiles_per_batch):
    t = pl.program_id(0)

    @pl.when(t % tiles_per_batch == 0)
    def _():
        ext_ref[0:DN_HALO, :] = jnp.zeros((DN_HALO, 3 * D_MODEL), F32)

    h = (x_ref[...] * (1.0 + sc_ref[0, 0]) + sh_ref[0, 0]).astype(BF16)

    for grp, out_ref in enumerate((q_ref, k_ref, v_ref)):
        cols = slice(grp * D_MODEL, (grp + 1) * D_MODEL)
        ext_ref[DN_HALO:DN_HALO + TM, cols] = _dot(h, w_ref[:, cols])
        ext = ext_ref[:, cols]
        acc = None
        for j in range(DN_CONV):
            off = DN_HALO - (DN_CONV - 1) + j
            res = off % SUBLANES
            shifted = ext if res == 0 else pltpu.roll(ext, DN_HALO + TM - res, axis=0)
            term = cw_ref[j:j + 1, cols] * shifted[off - res:off - res + TM, :]
            acc = term if acc is None else acc + term
        ext_ref[0:DN_HALO, cols] = ext_ref[TM:TM + DN_HALO, cols]
        y = _silu(acc)
        if out_ref is v_ref:
            out_ref[...] = y
        else:
            for hd in range(HEADS):
                lanes = slice(hd * HEAD_DIM, (hd + 1) * HEAD_DIM)
                ys = y[:, lanes]
                ss = jnp.sum(ys * ys, axis=-1, keepdims=True)
                out_ref[:, lanes] = ys * lax.rsqrt(ss + L2_EPS)

    z_ref[...] = _dot(h, w_ref[:, 3 * D_MODEL:4 * D_MODEL])

    ba = _dot(h, wba_ref[...].astype(BF16))
    beta = jax.nn.sigmoid(ba)
    g = -jnp.exp(alog_ref[...]) * _softplus(ba + dtb_ref[...])
    lane = lax.broadcasted_iota(jnp.int32, ba.shape, 1)
    bg_ref[...] = jnp.where(lane < HEADS, beta, g)


def _dn_in(x2d, mods, layer, mixer, tiles_per_batch, w_main, w_ba, conv_w, alog_vec, dtb_vec):
    rows = x2d.shape[0]
    row_spec = pl.BlockSpec((TM, D_MODEL), lambda t: (t, 0))
    base = layer * N_MOD
    out_sds = jax.ShapeDtypeStruct((rows, D_MODEL), F32)
    return pl.pallas_call(
        functools.partial(_dn_in_kernel, tiles_per_batch=tiles_per_batch),
        grid=(rows // TM,),
        in_specs=[
            row_spec, _mod_spec(base + 0, tiles_per_batch), _mod_spec(base + 1, tiles_per_batch),
            _const_spec((D_MODEL, 4 * D_MODEL), mixer), _const_spec((D_MODEL, LANES)),
            _const_spec((DN_CONV, 3 * D_MODEL)), _const_spec((1, LANES)), _const_spec((1, LANES)),
        ],
        out_specs=[row_spec, row_spec, row_spec, row_spec, pl.BlockSpec((TM, LANES), lambda t: (t, 0))],
        out_shape=[out_sds, out_sds, out_sds, out_sds, jax.ShapeDtypeStruct((rows, LANES), F32)],
        scratch_shapes=[pltpu.VMEM((DN_HALO + TM, 3 * D_MODEL), F32)],
        compiler_params=pltpu.CompilerParams(dimension_semantics=("arbitrary",),
                                             vmem_limit_bytes=VMEM_LIMIT),
        name="dn_in",
    )(x2d, mods, mods, w_main, w_ba, conv_w, alog_vec, dtb_vec)


def _inverse_minus_eye_levels(a_list, out):
    ri = lax.broadcasted_iota(jnp.int32, (CHUNK, CHUNK), 0) // BLOCK16
    ci = lax.broadcasted_iota(jnp.int32, (CHUNK, CHUNK), 1) // BLOCK16
    on_diag = ri == ci
    b16 = lambda xs: [x.astype(BF16) for x in xs]
    mm = lambda xs, ys: [_dot(x, y) for x, y in zip(xs, ys)]

    d = [jnp.where(on_diag, a, 0.0) for a in a_list]
    low = [a - x for a, x in zip(a_list, d)]
    d_16 = b16(d)
    d2 = mm(d_16, d_16)
    yield
    d2_16 = b16(d2)
    d4 = mm(d2_16, d2_16)
    dd2 = mm(d_16, d2_16)
    yield
    p = [x2 - x - y for x, x2, y in zip(d, d2, dd2)]
    d4_16 = b16(d4)
    d8 = mm(d4_16, d4_16)
    pd4 = mm(b16(p), d4_16)
    yield
    p = [x + y + z for x, y, z in zip(p, d4, pd4)]
    pd8 = mm(b16(p), b16(d8))
    yield
    p = [x + y + z for x, y, z in zip(p, d8, pd8)]
    p_16 = b16(p)
    pl_ = mm(p_16, b16(low))
    yield
    m = [x + y for x, y in zip(low, pl_)]
    m_16 = b16(m)
    m2 = mm(m_16, m_16)
    yield
    mm2 = mm(m_16, b16(m2))
    yield
    q = [x2 - x - y for x, x2, y in zip(m, m2, mm2)]
    qp = mm(b16(q), p_16)
    yield
    out.extend(x + y + z for x, y, z in zip(q, p, qp))


def _dn_core_kernel(q_ref, k_ref, v_ref, z_ref, bg_ref, nw_ref, o_ref,
                    state_ref, gam_ref, egam_ref, edec_ref, egl_ref,
                    u_ref, w_ref, qd_ref, kdt_ref, a_ref, *, tiles_per_batch):
    s = pl.program_id(0)
    cur = s % 2
    prev = 1 - cur

    @pl.when(s == 0)
    def _():
        for ref in (egl_ref, u_ref, w_ref, qd_ref, kdt_ref, a_ref):
            ref[1] = jnp.zeros(ref.shape[1:], ref.dtype)

    @pl.when(jnp.maximum(s - 1, 0) % tiles_per_batch == 0)
    def _():
        state_ref[...] = jnp.zeros(state_ref.shape, F32)

    r_idx = lax.broadcasted_iota(jnp.int32, (CHUNK, CHUNK), 0)
    c_idx = lax.broadcasted_iota(jnp.int32, (CHUNK, CHUNK), 1)
    causal = c_idx <= r_idx
    strict = c_idx < r_idx

    cum_mat = jnp.where(causal, 1.0, 0.0).astype(BF16)
    for c in range(TM // CHUNK):
        rows = slice(c * CHUNK, (c + 1) * CHUNK)
        gam = _dot_exact_lhs(cum_mat, bg_ref[rows, :])
        gtot = jnp.broadcast_to(gam[CHUNK - 1:CHUNK, :], gam.shape)
        gam_ref[rows, :] = gam
        egam_ref[rows, :] = jnp.exp(gam)
        edec_ref[rows, :] = jnp.exp(gtot - gam)
        egl_ref[cur, rows, :] = jnp.exp(gtot)
    scale = HEAD_DIM ** -0.5
    wide = (CHUNK, HEAD_DIM)
    nw = nw_ref[...]

    def independent_levels(grp):
        probs = []
        for ci_ in range(PASS1_CHUNKS):
            cidx = grp * PASS1_CHUNKS + ci_
            rows = pl.ds(pl.multiple_of(cidx * CHUNK, CHUNK), CHUNK)
            gam_c = gam_ref[rows, :]
            gam_t = gam_c.T
            bg_c = bg_ref[rows, :]
            egam_c = egam_ref[rows, :]
            edec_c = edec_ref[rows, :]
            for hd in range(HEADS):
                lanes = slice(hd * HEAD_DIM, (hd + 1) * HEAD_DIM)
                gl = HEADS + hd
                beta = jnp.broadcast_to(bg_c[:, hd:hd + 1], wide)
                eg = jnp.broadcast_to(egam_c[:, gl:gl + 1], wide)
                ed = jnp.broadcast_to(edec_c[:, gl:gl + 1], wide)
                gcol = jnp.broadcast_to(gam_c[:, gl:gl + 1], wide)[:, :CHUNK]
                decay = jnp.exp(jnp.where(causal, gcol - gam_t[gl:gl + 1, :], -jnp.inf))
                probs.append(dict(cidx=cidx, rows=rows, hd=hd, lanes=lanes, beta=beta, eg=eg, ed=ed,
                                  decay=decay, q=q_ref[rows, lanes], k=k_ref[rows, lanes],
                                  v=v_ref[rows, lanes]))
        for p in probs:
            p["k16"] = p["k"].astype(BF16)
            p["kq16"] = jnp.concatenate([p["k16"], (p["q"] * scale).astype(BF16)], axis=0)
        kkqk = [_dot_nt(p["kq16"], p["k16"]) for p in probs]
        yield
        a_list = []
        for p, kk in zip(probs, kkqk):
            a_list.append(jnp.where(strict, kk[:CHUNK] * p["beta"][:, :CHUNK] * p["decay"], 0.0))
            a_ref[cur, p["hd"], p["rows"], :] = (kk[CHUNK:] * p["decay"]).astype(BF16)
        tinv = []
        yield from _inverse_minus_eye_levels(a_list, tinv)
        for p in probs:
            p["vb"] = p["v"] * p["beta"]
            p["kbg"] = p["k"] * p["beta"] * p["eg"]
        uw = [_dot(t.astype(BF16), jnp.concatenate([p["vb"], p["kbg"]], axis=1).astype(BF16))
              for p, t in zip(probs, tinv)]
        yield
        for p, r in zip(probs, uw):
            rows, lanes = p["rows"], p["lanes"]
            u_ref[cur, rows, lanes] = p["vb"] + r[:, :HEAD_DIM]
            w_ref[cur, rows, lanes] = (p["kbg"] + r[:, HEAD_DIM:]).astype(BF16)
            qd_ref[cur, rows, lanes] = (p["q"] * (p["eg"] * scale)).astype(BF16)
            kdt_ref[cur, p["hd"], p["cidx"]] = (p["k"] * p["ed"]).T.astype(BF16)

    def recurrent_levels(grp):
        heads = range(HEADS)
        lanes = [slice(hd * HEAD_DIM, (hd + 1) * HEAD_DIM) for hd in heads]
        for ci_ in range(PASS1_CHUNKS):
            c = grp * PASS1_CHUNKS + ci_
            rows = pl.ds(pl.multiple_of(c * CHUNK, CHUNK), CHUNK)
            egl_c = egl_ref[prev, rows, :]
            state = [state_ref[hd] for hd in heads]
            wq = [jnp.concatenate([w_ref[prev, rows, lanes[hd]], qd_ref[prev, rows, lanes[hd]]], axis=0)
                  for hd in heads]
            r = [_dot(wq[hd], state[hd].astype(BF16)) for hd in heads]
            yield
            vn16 = [(u_ref[prev, rows, lanes[hd]] - r[hd][:CHUNK]).astype(BF16) for hd in heads]
            av = [_dot(a_ref[prev, hd, rows, :], vn16[hd]) for hd in heads]
            kv = [_dot(kdt_ref[prev, hd, c], vn16[hd]) for hd in heads]
            yield
            for hd in heads:
                state_ref[hd] = state[hd] * egl_c[0:1, HEADS + hd:HEADS + hd + 1] + kv[hd]
                o = r[hd][CHUNK:] + av[hd]
                on = o * lax.rsqrt(jnp.mean(o * o, axis=-1, keepdims=True) + RMS_EPS) * nw
                o_ref[rows, lanes[hd]] = (on * _silu(z_ref[rows, lanes[hd]])).astype(BF16)

    def group_body(grp, carry):
        pending = [recurrent_levels(grp), independent_levels(grp)]
        while pending:
            for gen in list(pending):
                try:
                    next(gen)
                except StopIteration:
                    pending.remove(gen)
        return carry

    lax.fori_loop(0, TM // CHUNK // PASS1_CHUNKS, group_body, 0)


def _dn_core(q, k, v, z, bg, norm_w, tiles_per_batch):
    rows = q.shape[0]
    n_tiles = rows // TM
    in_tile = lambda s: jnp.minimum(s, n_tiles - 1)
    out_tile = lambda s: jnp.maximum(s - 1, 0)
    in_rows = pl.BlockSpec((TM, D_MODEL), lambda s: (in_tile(s), 0))
    out_rows = pl.BlockSpec((TM, D_MODEL), lambda s: (out_tile(s), 0))
    small = pltpu.VMEM((TM, LANES), F32)
    return pl.pallas_call(
        functools.partial(_dn_core_kernel, tiles_per_batch=tiles_per_batch),
        grid=(n_tiles + 1,),
        in_specs=[in_rows, in_rows, in_rows, out_rows,
                  pl.BlockSpec((TM, LANES), lambda s: (in_tile(s), 0)),
                  pl.BlockSpec((1, HEAD_DIM), lambda s: (0, 0))],
        out_specs=out_rows,
        out_shape=jax.ShapeDtypeStruct((rows, D_MODEL), BF16),
        scratch_shapes=[
            pltpu.VMEM((HEADS, HEAD_DIM, HEAD_DIM), F32),
            small, small, small,
            pltpu.VMEM((2, TM, LANES), F32),
            pltpu.VMEM((2, TM, D_MODEL), F32),
            pltpu.VMEM((2, TM, D_MODEL), BF16),
            pltpu.VMEM((2, TM, D_MODEL), BF16),
            pltpu.VMEM((2, HEADS, TM // CHUNK, HEAD_DIM, CHUNK), BF16),
            pltpu.VMEM((2, HEADS, TM, CHUNK), BF16),
        ],
        compiler_params=pltpu.CompilerParams(dimension_semantics=("arbitrary",),
                                             vmem_limit_bytes=VMEM_LIMIT),
        name="dn_core",
    )(q, k, v, z, bg, norm_w)


def _cf_conv_piece(ext_ref, dw_ref, res, piece, not_before):
    n = CF_PIECE + CF_HALO + SUBLANES
    start = pl.multiple_of(piece * CF_PIECE + _zero_after(not_before), SUBLANES)
    window = ext_ref[pl.ds(start, n), :]
    shifted = window if res == 0 else pltpu.roll(window, n - res, axis=0)
    acc = None
    for j in range(CF_KERNEL):
        off = CF_HALO - (CF_KERNEL - 1) + j
        if off % SUBLANES != res:
            continue
        term = dw_ref[j:j + 1, :] * shifted[off - res:off - res + CF_PIECE, :]
        acc = term if acc is None else acc + term
    return acc


def _cf_layer_kernel(x_ref, xres_ref, sh1_ref, sc1_ref, gt1_ref, sh2_ref, sc2_ref, gt2_ref,
                     win_ref, dw_ref, dwb_ref, cg_ref, cb_ref, wo_ref, lng_ref, lnb_ref, w1_ref, w2_ref,
                     o_ref, ext_ref, ext_next_ref, u_ref, u_next_ref, *, n_tiles, tiles_per_batch):
    s = pl.program_id(0)
    t_in = jnp.minimum(s, n_tiles - 1)
    first_of_batch = t_in % tiles_per_batch == 0

    @pl.when(s == 0)
    def _():
        ext_ref[...] = jnp.zeros(ext_ref.shape, F32)
        u_ref[...] = jnp.zeros(u_ref.shape, BF16)
        ext_next_ref[CF_HALO + TM:CF_HALO + TM + SUBLANES, :] = jnp.zeros((SUBLANES, D_MODEL), F32)

    @pl.when(first_of_batch)
    def _():
        ext_next_ref[0:CF_HALO, :] = jnp.zeros((CF_HALO, D_MODEL), F32)

    @pl.when(jnp.logical_not(first_of_batch))
    def _():
        ext_next_ref[0:CF_HALO, :] = ext_ref[TM:TM + CF_HALO, :]

    n_chunks = FF_DIM // FF_TILE
    n_pieces = TM // CF_PIECE
    conv = [None] * n_pieces
    todo = [(res, piece) for res in range(SUBLANES) for piece in range(n_pieces)]
    slots_seen = [0]

    def conv_pieces_after(result):
        for blk in range(n_pieces):
            slots_seen[0] += 1
            if slots_seen[0] % 5 == 0 or not todo:
                continue
            res, piece = todo.pop(0)
            r0 = blk * CF_PIECE
            part = _cf_conv_piece(ext_ref, dw_ref, res, piece, result[r0:r0 + 1, 0:1])
            conv[piece] = part if conv[piece] is None else conv[piece] + part

    y = _dot(u_ref[...], wo_ref[...])
    conv_pieces_after(y)

    h = (x_ref[...] * (1.0 + sc1_ref[0, 0]) + sh1_ref[0, 0]).astype(BF16)
    val = _dot(h, win_ref[:, 0:D_MODEL])
    conv_pieces_after(val)
    gate = _dot(h, win_ref[:, D_MODEL:2 * D_MODEL])
    conv_pieces_after(gate)
    ext_next_ref[CF_HALO:CF_HALO + TM, :] = val * jax.nn.sigmoid(gate)

    x1 = _layer_norm(ALPHA * xres_ref[...] + (1.0 + gt1_ref[0, 0]) * y, lng_ref[0:1, :], lnb_ref[0:1, :])
    h2 = (x1 * (1.0 + sc2_ref[0, 0]) + sh2_ref[0, 0]).astype(BF16)
    chunk_cols = [slice(c * FF_TILE, (c + 1) * FF_TILE) for c in range(n_chunks)]
    d1 = _dot(h2, w1_ref[:, chunk_cols[0]])
    conv_pieces_after(d1)
    acc = None
    for c in range(n_chunks):
        d1_next = None
        if c + 1 < n_chunks:
            d1_next = _dot(h2, w1_ref[:, chunk_cols[c + 1]])
            conv_pieces_after(d1_next)
        a = jnp.maximum(d1, 0.0)
        p = _dot((a * a).astype(BF16), w2_ref[chunk_cols[c], :])
        conv_pieces_after(p)
        acc = p if acc is None else acc + p
        d1 = d1_next
    assert not todo
    o_ref[...] = _layer_norm(ALPHA * x1 + (1.0 + gt2_ref[0, 0]) * acc, lng_ref[1:2, :], lnb_ref[1:2, :])

    cu = _layer_norm(jnp.concatenate(conv, axis=0) + dwb_ref[...], cg_ref[...], cb_ref[...])
    u_next_ref[...] = _silu(cu).astype(BF16)

    u_ref[...] = u_next_ref[...]
    ext_ref[...] = ext_next_ref[...]


def _cf_layer(x2d, mods, layer, mixer, tiles_per_batch, w_in, dw_w, dw_b, cf_g, cf_b, w_out, ln_g, ln_b,
              w1, w2):
    rows = x2d.shape[0]
    n_tiles = rows // TM
    base = layer * N_MOD
    in_tile = lambda s: jnp.minimum(s, n_tiles - 1)
    mlp_tile = lambda s: jnp.maximum(s - 2, 0)

    def mod_spec(row, tile_of_step):
        return pl.BlockSpec((1, 1, 1, D_MODEL), lambda s: (row, tile_of_step(s) // tiles_per_batch, 0, 0))

    ext_rows = CF_HALO + TM + SUBLANES
    return pl.pallas_call(
        functools.partial(_cf_layer_kernel, n_tiles=n_tiles, tiles_per_batch=tiles_per_batch),
        grid=(n_tiles + 2,),
        in_specs=[
            pl.BlockSpec((TM, D_MODEL), lambda s: (in_tile(s), 0)),
            pl.BlockSpec((TM, D_MODEL), lambda s: (mlp_tile(s), 0)),
            mod_spec(base + 0, in_tile), mod_spec(base + 1, in_tile),
            mod_spec(base + 2, mlp_tile), mod_spec(base + 3, mlp_tile),
            mod_spec(base + 4, mlp_tile), mod_spec(base + 5, mlp_tile),
            _const_spec((D_MODEL, 2 * D_MODEL), mixer), _const_spec((CF_KERNEL, D_MODEL)),
            _const_spec((1, D_MODEL)), _const_spec((1, D_MODEL)), _const_spec((1, D_MODEL)),
            _const_spec((D_MODEL, D_MODEL), mixer), _const_spec((2, D_MODEL)), _const_spec((2, D_MODEL)),
            _const_spec((D_MODEL, FF_DIM), layer), _const_spec((FF_DIM, D_MODEL), layer),
        ],
        out_specs=pl.BlockSpec((TM, D_MODEL), lambda s: (mlp_tile(s), 0)),
        out_shape=jax.ShapeDtypeStruct((rows, D_MODEL), F32),
        scratch_shapes=[pltpu.VMEM((ext_rows, D_MODEL), F32),
                        pltpu.VMEM((ext_rows, D_MODEL), F32),
                        pltpu.VMEM((TM, D_MODEL), BF16),
                        pltpu.VMEM((TM, D_MODEL), BF16)],
        compiler_params=pltpu.CompilerParams(dimension_semantics=("arbitrary",),
                                             vmem_limit_bytes=VMEM_LIMIT),
        name="cf_layer",
    )(x2d, x2d, mods, mods, mods, mods, mods, mods, w_in, dw_w, dw_b, cf_g, cf_b, w_out, ln_g, ln_b, w1, w2)


def kernel(x, c, ada_w, ada_b, ln_g, ln_b, dn_w_in, dn_conv_w, dn_a_log, dn_dt_bias, dn_norm_w,
           dn_w_out, cf_w_in, cf_dw_w, cf_dw_b, cf_ln_g, cf_ln_b, cf_w_out, ff_w1, ff_w2):
    batch, seq, d = x.shape
    assert d == D_MODEL and seq % TM == 0 and TM % CHUNK == 0
    tiles_per_batch = seq // TM
    n_qkv = 3 * D_MODEL
    n_main = 4 * D_MODEL

    mods = _ada_mods(c, ada_w, ada_b)
    w1, w2 = _cast_stack(ff_w1), _cast_stack(ff_w2)
    dn_main, dn_out = _cast_stack(dn_w_in, cols=n_main), _cast_stack(dn_w_out)
    cf_in, cf_out = _cast_stack(cf_w_in), _cast_stack(cf_w_out)
    x2d = x.reshape(batch * seq, D_MODEL)
    for i in range(DEPTH):
        j = i // 2
        if i % 2 == 0:
            w_ba = jnp.pad(dn_w_in[j, :, n_main:], ((0, 0), (0, LANES - 2 * HEADS)))
            gate_pad = (HEADS, LANES - 2 * HEADS)
            alog_vec = jnp.pad(dn_a_log[j], gate_pad).reshape(1, LANES)
            dtb_vec = jnp.pad(dn_dt_bias[j], gate_pad).reshape(1, LANES)
            q, k, v, z, bg = _dn_in(x2d, mods, i, j, tiles_per_batch, dn_main, w_ba,
                                    dn_conv_w[j][:, :n_qkv], alog_vec, dtb_vec)
            u = _dn_core(q, k, v, z, bg, dn_norm_w[j].reshape(1, HEAD_DIM), tiles_per_batch)
            x2d = _post(x2d, u, mods, i, j, tiles_per_batch, dn_out, ln_g[i], ln_b[i], w1, w2)
        else:
            x2d = _cf_layer(x2d, mods, i, j, tiles_per_batch, cf_in, cf_dw_w[j],
                            cf_dw_b[j].reshape(1, D_MODEL), cf_ln_g[j].reshape(1, D_MODEL),
                            cf_ln_b[j].reshape(1, D_MODEL), cf_out, ln_g[i], ln_b[i], w1, w2)
    return x2d.reshape(batch, seq, D_MODEL)
```

```python
import functools

import jax
import jax.numpy as jnp
from jax import lax
from jax.experimental import pallas as pl
from jax.experimental.pallas import tpu as pltpu

F32 = jnp.float32
BF16 = jnp.bfloat16

D_MODEL = 1024
DEPTH = 4
N_MOD = 6
HEADS = 8
HEAD_DIM = 128
DN_CONV = 4
CHUNK = 64
BLOCK16 = 16
CF_KERNEL = 31
FF_DIM = 4 * D_MODEL
ALPHA = (2.0 * DEPTH) ** 0.25
LN_EPS = 1e-5
RMS_EPS = 1e-6
L2_EPS = 1e-6

LANES = 128
SUBLANES = 8
TM = 512
FF_TILE = 1024
CF_HALO = 32
CF_PIECE = 128
DN_HALO = 8
PASS1_CHUNKS = 4
VMEM_LIMIT = 56 * 1024 * 1024
CAST_BLOCK_ELEMS = 1024 * 1024


def _dot(a, b):
    return jnp.dot(a, b, preferred_element_type=F32)


def _dot_nt(a, b):
    return lax.dot_general(a, b, (((1,), (1,)), ((), ())), preferred_element_type=F32)


def _split3(x):
    x1 = x.astype(BF16)
    r1 = x - x1.astype(F32)
    x2 = r1.astype(BF16)
    x3 = (r1 - x2.astype(F32)).astype(BF16)
    return x1, x2, x3


def _dot_exact_lhs(a_bf16, x):
    x1, x2, x3 = _split3(x)
    return _dot(a_bf16, x3) + _dot(a_bf16, x2) + _dot(a_bf16, x1)


def _layer_norm(r, g, b):
    mu = jnp.mean(r, axis=-1, keepdims=True)
    d = r - mu
    var = jnp.mean(d * d, axis=-1, keepdims=True)
    return d * lax.rsqrt(var + LN_EPS) * g + b


def _sigmoid(x):
    return 0.5 * jnp.tanh(0.5 * x) + 0.5


def _silu(x):
    return x * _sigmoid(x)


def _softplus(x):
    return jnp.maximum(x, 0.0) + jnp.log1p(jnp.exp(-jnp.abs(x)))


def _zero_after(v):
    bits = lax.bitcast_convert_type(v, jnp.uint32)
    z = lax.shift_right_logical(lax.shift_right_logical(bits, jnp.uint32(16)), jnp.uint32(16))
    return z[0, 0].astype(jnp.int32)


def _const_spec(shape, layer=None):
    nd = len(shape)
    if layer is None:
        return pl.BlockSpec(shape, lambda *_: (0,) * nd, pipeline_mode=pl.Buffered(1))
    return pl.BlockSpec((None,) + tuple(shape), lambda *_: (layer,) + (0,) * nd,
                        pipeline_mode=pl.Buffered(1))


def _mod_spec(row, tiles_per_batch):
    return pl.BlockSpec((1, 1, 1, D_MODEL), lambda t: (row, t // tiles_per_batch, 0, 0))


def _ada_kernel(c_ref, w_ref, b_ref, o_ref):
    cond = _silu(c_ref[...])
    o_ref[0] = _dot(cond, w_ref[0]) + b_ref[0]


def _ada_mods(c, ada_w, ada_b):
    batch = c.shape[0]
    out = pl.pallas_call(
        _ada_kernel,
        grid=(DEPTH, N_MOD),
        in_specs=[
            pl.BlockSpec((batch, D_MODEL), lambda i, j: (0, 0)),
            pl.BlockSpec((1, D_MODEL, D_MODEL), lambda i, j: (i, 0, j)),
            pl.BlockSpec((1, 1, D_MODEL), lambda i, j: (i, 0, j)),
        ],
        out_specs=pl.BlockSpec((1, batch, D_MODEL), lambda i, j: (i * N_MOD + j, 0, 0)),
        out_shape=jax.ShapeDtypeStruct((DEPTH * N_MOD, batch, D_MODEL), F32),
        compiler_params=pltpu.CompilerParams(dimension_semantics=("arbitrary", "arbitrary")),
        name="ada_mods",
    )(c, ada_w, ada_b.reshape(DEPTH, 1, N_MOD * D_MODEL))
    return out.reshape(DEPTH * N_MOD, batch, 1, D_MODEL)


def _cast_kernel(w_ref, o_ref):
    o_ref[...] = w_ref[:, 0:o_ref.shape[-1]].astype(BF16)


def _cast_stack(w, cols=None):
    n_layers, rows, all_cols = w.shape
    cols = all_cols if cols is None else cols
    block_rows = rows
    while block_rows * all_cols > CAST_BLOCK_ELEMS:
        block_rows //= 2
    assert rows % block_rows == 0 and block_rows % SUBLANES == 0 and cols % LANES == 0
    return pl.pallas_call(
        _cast_kernel,
        grid=(n_layers, rows // block_rows),
        in_specs=[pl.BlockSpec((None, block_rows, all_cols), lambda l, i: (l, i, 0))],
        out_specs=pl.BlockSpec((None, block_rows, cols), lambda l, i: (l, i, 0)),
        out_shape=jax.ShapeDtypeStruct((n_layers, rows, cols), BF16),
        compiler_params=pltpu.CompilerParams(dimension_semantics=("arbitrary", "arbitrary")),
        name="cast_bf16",
    )(w)


def _post_kernel(x_ref, u_ref, gt1_ref, sh2_ref, sc2_ref, gt2_ref, wo_ref, lng_ref, lnb_ref,
                 w1_ref, w2_ref, o_ref):
    x = x_ref[...]
    y = _dot(u_ref[...], wo_ref[...])
    x1 = _layer_norm(ALPHA * x + (1.0 + gt1_ref[0, 0]) * y, lng_ref[0:1, :], lnb_ref[0:1, :])
    h = (x1 * (1.0 + sc2_ref[0, 0]) + sh2_ref[0, 0]).astype(BF16)
    n_chunks = FF_DIM // FF_TILE
    chunk_cols = [slice(c * FF_TILE, (c + 1) * FF_TILE) for c in range(n_chunks)]
    d1 = _dot(h, w1_ref[:, chunk_cols[0]])
    acc = None
    for c in range(n_chunks):
        d1_next = _dot(h, w1_ref[:, chunk_cols[c + 1]]) if c + 1 < n_chunks else None
        a = jnp.maximum(d1, 0.0)
        p = _dot((a * a).astype(BF16), w2_ref[chunk_cols[c], :])
        acc = p if acc is None else acc + p
        d1 = d1_next
    o_ref[...] = _layer_norm(ALPHA * x1 + (1.0 + gt2_ref[0, 0]) * acc, lng_ref[1:2, :], lnb_ref[1:2, :])


def _post(x2d, u2d, mods, layer, mixer, tiles_per_batch, w_out, ln_g, ln_b, w1, w2):
    rows = x2d.shape[0]
    row_spec = pl.BlockSpec((TM, D_MODEL), lambda t: (t, 0))
    base = layer * N_MOD
    return pl.pallas_call(
        _post_kernel,
        grid=(rows // TM,),
        in_specs=[
            row_spec, row_spec,
            _mod_spec(base + 2, tiles_per_batch), _mod_spec(base + 3, tiles_per_batch),
            _mod_spec(base + 4, tiles_per_batch), _mod_spec(base + 5, tiles_per_batch),
            _const_spec((D_MODEL, D_MODEL), mixer), _const_spec((2, D_MODEL)), _const_spec((2, D_MODEL)),
            _const_spec((D_MODEL, FF_DIM), layer), _const_spec((FF_DIM, D_MODEL), layer),
        ],
        out_specs=row_spec,
        out_shape=jax.ShapeDtypeStruct((rows, D_MODEL), F32),
        compiler_params=pltpu.CompilerParams(dimension_semantics=("arbitrary",),
                                             vmem_limit_bytes=VMEM_LIMIT),
        name="post_mlp",
    )(x2d, u2d, mods, mods, mods, mods, w_out, ln_g, ln_b, w1, w2)


def _dn_in_kernel(x_ref, sh_ref, sc_ref, w_ref, wba_ref, cw_ref, alog_ref, dtb_ref,
                  q_ref, k_ref, v_ref, z_ref, bg_ref, ext_ref, *, tiles_per_batch):
    t = pl.program_id(0)

    @pl.when(t % tiles_per_batch == 0)
    def _():
        ext_ref[0:DN_HALO, :] = jnp.zeros((DN_HALO, 3 * D_MODEL), F32)

    h = (x_ref[...] * (1.0 + sc_ref[0, 0]) + sh_ref[0, 0]).astype(BF16)

    for grp, out_ref in enumerate((q_ref, k_ref, v_ref)):
        cols = slice(grp * D_MODEL, (grp + 1) * D_MODEL)
        ext_ref[DN_HALO:DN_HALO + TM, cols] = _dot(h, w_ref[:, cols])
        ext = ext_ref[:, cols]
        acc = None
        for j in range(DN_CONV):
            off = DN_HALO - (DN_CONV - 1) + j
            res = off % SUBLANES
            shifted = ext if res == 0 else pltpu.roll(ext, DN_HALO + TM - res, axis=0)
            term = cw_ref[j:j + 1, cols] * shifted[off - res:off - res + TM, :]
            acc = term if acc is None else acc + term
        ext_ref[0:DN_HALO, cols] = ext_ref[TM:TM + DN_HALO, cols]
        y = _silu(acc)
        if out_ref is v_ref:
            out_ref[...] = y
        else:
            for hd in range(HEADS):
                lanes = slice(hd * HEAD_DIM, (hd + 1) * HEAD_DIM)
                ys = y[:, lanes]
                ss = jnp.sum(ys * ys, axis=-1, keepdims=True)
                out_ref[:, lanes] = ys * lax.rsqrt(ss + L2_EPS)

    z_ref[...] = _dot(h, w_ref[:, 3 * D_MODEL:4 * D_MODEL])

    ba = _dot(h, wba_ref[...].astype(BF16))
    beta = jax.nn.sigmoid(ba)
    g = -jnp.exp(alog_ref[...]) * _softplus(ba + dtb_ref[...])
    lane = lax.broadcasted_iota(jnp.int32, ba.shape, 1)
    bg_ref[...] = jnp.where(lane < HEADS, beta, g)


def _dn_in(x2d, mods, layer, mixer, tiles_per_batch, w_main, w_ba, conv_w, alog_vec, dtb_vec):
    rows = x2d.shape[0]
    row_spec = pl.BlockSpec((TM, D_MODEL), lambda t: (t, 0))
    base = layer * N_MOD
    out_sds = jax.ShapeDtypeStruct((rows, D_MODEL), F32)
    return pl.pallas_call(
        functools.partial(_dn_in_kernel, tiles_per_batch=tiles_per_batch),
        grid=(rows // TM,),
        in_specs=[
            row_spec, _mod_spec(base + 0, tiles_per_batch), _mod_spec(base + 1, tiles_per_batch),
            _const_spec((D_MODEL, 4 * D_MODEL), mixer), _const_spec((D_MODEL, LANES)),
            _const_spec((DN_CONV, 3 * D_MODEL)), _const_spec((1, LANES)), _const_spec((1, LANES)),
        ],
        out_specs=[row_spec, row_spec, row_spec, row_spec, pl.BlockSpec((TM, LANES), lambda t: (t, 0))],
        out_shape=[out_sds, out_sds, out_sds, out_sds, jax.ShapeDtypeStruct((rows, LANES), F32)],
        scratch_shapes=[pltpu.VMEM((DN_HALO + TM, 3 * D_MODEL), F32)],
        compiler_params=pltpu.CompilerParams(dimension_semantics=("arbitrary",),
                                             vmem_limit_bytes=VMEM_LIMIT),
        name="dn_in",
    )(x2d, mods, mods, w_main, w_ba, conv_w, alog_vec, dtb_vec)


def _inverse_minus_eye_levels(a_list, out):
    ri = lax.broadcasted_iota(jnp.int32, (CHUNK, CHUNK), 0) // BLOCK16
    ci = lax.broadcasted_iota(jnp.int32, (CHUNK, CHUNK), 1) // BLOCK16
    on_diag = ri == ci
    b16 = lambda xs: [x.astype(BF16) for x in xs]
    mm = lambda xs, ys: [_dot(x, y) for x, y in zip(xs, ys)]

    d = [jnp.where(on_diag, a, 0.0) for a in a_list]
    low = [a - x for a, x in zip(a_list, d)]
    d_16 = b16(d)
    d2 = mm(d_16, d_16)
    yield
    d2_16 = b16(d2)
    d4 = mm(d2_16, d2_16)
    dd2 = mm(d_16, d2_16)
    yield
    p = [x2 - x - y for x, x2, y in zip(d, d2, dd2)]
    d4_16 = b16(d4)
    d8 = mm(d4_16, d4_16)
    pd4 = mm(b16(p), d4_16)
    yield
    p = [x + y + z for x, y, z in zip(p, d4, pd4)]
    pd8 = mm(b16(p), b16(d8))
    yield
    p = [x + y + z for x, y, z in zip(p, d8, pd8)]
    p_16 = b16(p)
    pl_ = mm(p_16, b16(low))
    yield
    m = [x + y for x, y in zip(low, pl_)]
    m_16 = b16(m)
    m2 = mm(m_16, m_16)
    yield
    mm2 = mm(m_16, b16(m2))
    yield
    q = [x2 - x - y for x, x2, y in zip(m, m2, mm2)]
    qp = mm(b16(q), p_16)
    yield
    out.extend(x + y + z for x, y, z in zip(q, p, qp))


def _dn_core_kernel(q_ref, k_ref, v_ref, z_ref, bg_ref, nw_ref, o_ref,
                    state_ref, gam_ref, egam_ref, edec_ref, egl_ref,
                    u_ref, w_ref, qd_ref, kdt_ref, a_ref, *, tiles_per_batch):
    s = pl.program_id(0)
    cur = s % 2
    prev = 1 - cur

    @pl.when(s == 0)
    def _():
        for ref in (egl_ref, u_ref, w_ref, qd_ref, kdt_ref, a_ref):
            ref[1] = jnp.zeros(ref.shape[1:], ref.dtype)

    @pl.when(jnp.maximum(s - 1, 0) % tiles_per_batch == 0)
    def _():
        state_ref[...] = jnp.zeros(state_ref.shape, F32)

    r_idx = lax.broadcasted_iota(jnp.int32, (CHUNK, CHUNK), 0)
    c_idx = lax.broadcasted_iota(jnp.int32, (CHUNK, CHUNK), 1)
    causal = c_idx <= r_idx
    strict = c_idx < r_idx

    cum_mat = jnp.where(causal, 1.0, 0.0).astype(BF16)
    for c in range(TM // CHUNK):
        rows = slice(c * CHUNK, (c + 1) * CHUNK)
        gam = _dot_exact_lhs(cum_mat, bg_ref[rows, :])
        gtot = jnp.broadcast_to(gam[CHUNK - 1:CHUNK, :], gam.shape)
        gam_ref[rows, :] = gam
        egam_ref[rows, :] = jnp.exp(gam)
        edec_ref[rows, :] = jnp.exp(gtot - gam)
        egl_ref[cur, rows, :] = jnp.exp(gtot)
    scale = HEAD_DIM ** -0.5
    wide = (CHUNK, HEAD_DIM)
    nw = nw_ref[...]

    def independent_levels(grp):
        probs = []
        for ci_ in range(PASS1_CHUNKS):
            cidx = grp * PASS1_CHUNKS + ci_
            rows = pl.ds(pl.multiple_of(cidx * CHUNK, CHUNK), CHUNK)
            gam_c = gam_ref[rows, :]
            gam_t = gam_c.T
            bg_c = bg_ref[rows, :]
            egam_c = egam_ref[rows, :]
            edec_c = edec_ref[rows, :]
            for hd in range(HEADS):
                lanes = slice(hd * HEAD_DIM, (hd + 1) * HEAD_DIM)
                gl = HEADS + hd
                beta = jnp.broadcast_to(bg_c[:, hd:hd + 1], wide)
                eg = jnp.broadcast_to(egam_c[:, gl:gl + 1], wide)
                ed = jnp.broadcast_to(edec_c[:, gl:gl + 1], wide)
                gcol = jnp.broadcast_to(gam_c[:, gl:gl + 1], wide)[:, :CHUNK]
                decay = jnp.exp(jnp.where(causal, gcol - gam_t[gl:gl + 1, :], -jnp.inf))
                probs.append(dict(cidx=cidx, rows=rows, hd=hd, lanes=lanes, beta=beta, eg=eg, ed=ed,
                                  decay=decay, q=q_ref[rows, lanes], k=k_ref[rows, lanes],
                                  v=v_ref[rows, lanes]))
        for p in probs:
            p["k16"] = p["k"].astype(BF16)
            p["kq16"] = jnp.concatenate([p["k16"], (p["q"] * scale).astype(BF16)], axis=0)
        kkqk = [_dot_nt(p["kq16"], p["k16"]) for p in probs]
        yield
        a_list = []
        for p, kk in zip(probs, kkqk):
            a_list.append(jnp.where(strict, kk[:CHUNK] * p["beta"][:, :CHUNK] * p["decay"], 0.0))
            a_ref[cur, p["hd"], p["rows"], :] = (kk[CHUNK:] * p["decay"]).astype(BF16)
        tinv = []
        yield from _inverse_minus_eye_levels(a_list, tinv)
        for p in probs:
            p["vb"] = p["v"] * p["beta"]
            p["kbg"] = p["k"] * p["beta"] * p["eg"]
        uw = [_dot(t.astype(BF16), jnp.concatenate([p["vb"], p["kbg"]], axis=1).astype(BF16))
              for p, t in zip(probs, tinv)]
        yield
        for p, r in zip(probs, uw):
            rows, lanes = p["rows"], p["lanes"]
            u_ref[cur, rows, lanes] = p["vb"] + r[:, :HEAD_DIM]
            w_ref[cur, rows, lanes] = (p["kbg"] + r[:, HEAD_DIM:]).astype(BF16)
            qd_ref[cur, rows, lanes] = (p["q"] * (p["eg"] * scale)).astype(BF16)
            kdt_ref[cur, p["hd"], p["cidx"]] = (p["k"] * p["ed"]).T.astype(BF16)

    def recurrent_levels(grp):
        heads = range(HEADS)
        lanes = [slice(hd * HEAD_DIM, (hd + 1) * HEAD_DIM) for hd in heads]
        for ci_ in range(PASS1_CHUNKS):
            c = grp * PASS1_CHUNKS + ci_
            rows = pl.ds(pl.multiple_of(c * CHUNK, CHUNK), CHUNK)
            egl_c = egl_ref[prev, rows, :]
            state = [state_ref[hd] for hd in heads]
            wq = [jnp.concatenate([w_ref[prev, rows, lanes[hd]], qd_ref[prev, rows, lanes[hd]]], axis=0)
                  for hd in heads]
            r = [_dot(wq[hd], state[hd].astype(BF16)) for hd in heads]
            yield
            vn16 = [(u_ref[prev, rows, lanes[hd]] - r[hd][:CHUNK]).astype(BF16) for hd in heads]
            av = [_dot(a_ref[prev, hd, rows, :], vn16[hd]) for hd in heads]
            kv = [_dot(kdt_ref[prev, hd, c], vn16[hd]) for hd in heads]
            yield
            for hd in heads:
                state_ref[hd] = state[hd] * egl_c[0:1, HEADS + hd:HEADS + hd + 1] + kv[hd]
                o = r[hd][CHUNK:] + av[hd]
                on = o * lax.rsqrt(jnp.mean(o * o, axis=-1, keepdims=True) + RMS_EPS) * nw
                o_ref[rows, lanes[hd]] = (on * _silu(z_ref[rows, lanes[hd]])).astype(BF16)

    def group_body(grp, carry):
        pending = [recurrent_levels(grp), independent_levels(grp)]
        while pending:
            for gen in list(pending):
                try:
                    next(gen)
                except StopIteration:
                    pending.remove(gen)
        return carry

    lax.fori_loop(0, TM // CHUNK // PASS1_CHUNKS, group_body, 0)


def _dn_core(q, k, v, z, bg, norm_w, tiles_per_batch):
    rows = q.shape[0]
    n_tiles = rows // TM
    in_tile = lambda s: jnp.minimum(s, n_tiles - 1)
    out_tile = lambda s: jnp.maximum(s - 1, 0)
    in_rows = pl.BlockSpec((TM, D_MODEL), lambda s: (in_tile(s), 0))
    out_rows = pl.BlockSpec((TM, D_MODEL), lambda s: (out_tile(s), 0))
    small = pltpu.VMEM((TM, LANES), F32)
    return pl.pallas_call(
        functools.partial(_dn_core_kernel, tiles_per_batch=tiles_per_batch),
        grid=(n_tiles + 1,),
        in_specs=[in_rows, in_rows, in_rows, out_rows,
                  pl.BlockSpec((TM, LANES), lambda s: (in_tile(s), 0)),
                  pl.BlockSpec((1, HEAD_DIM), lambda s: (0, 0))],
        out_specs=out_rows,
        out_shape=jax.ShapeDtypeStruct((rows, D_MODEL), BF16),
        scratch_shapes=[
            pltpu.VMEM((HEADS, HEAD_DIM, HEAD_DIM), F32),
            small, small, small,
            pltpu.VMEM((2, TM, LANES), F32),
            pltpu.VMEM((2, TM, D_MODEL), F32),
            pltpu.VMEM((2, TM, D_MODEL), BF16),
            pltpu.VMEM((2, TM, D_MODEL), BF16),
            pltpu.VMEM((2, HEADS, TM // CHUNK, HEAD_DIM, CHUNK), BF16),
            pltpu.VMEM((2, HEADS, TM, CHUNK), BF16),
        ],
        compiler_params=pltpu.CompilerParams(dimension_semantics=("arbitrary",),
                                             vmem_limit_bytes=VMEM_LIMIT),
        name="dn_core",
    )(q, k, v, z, bg, norm_w)


def _cf_conv_piece(ext_ref, dw_ref, res, piece, not_before):
    n = CF_PIECE + CF_HALO + SUBLANES
    start = pl.multiple_of(piece * CF_PIECE + _zero_after(not_before), SUBLANES)
    window = ext_ref[pl.ds(start, n), :]
    shifted = window if res == 0 else pltpu.roll(window, n - res, axis=0)
    acc = None
    for j in range(CF_KERNEL):
        off = CF_HALO - (CF_KERNEL - 1) + j
        if off % SUBLANES != res:
            continue
        term = dw_ref[j:j + 1, :] * shifted[off - res:off - res + CF_PIECE, :]
        acc = term if acc is None else acc + term
    return acc


def _cf_layer_kernel(x_ref, xres_ref, sh1_ref, sc1_ref, gt1_ref, sh2_ref, sc2_ref, gt2_ref,
                     win_ref, dw_ref, dwb_ref, cg_ref, cb_ref, wo_ref, lng_ref, lnb_ref, w1_ref, w2_ref,
                     o_ref, ext_ref, ext_next_ref, u_ref, u_next_ref, *, n_tiles, tiles_per_batch):
    s = pl.program_id(0)
    t_in = jnp.minimum(s, n_tiles - 1)
    first_of_batch = t_in % tiles_per_batch == 0

    @pl.when(s == 0)
    def _():
        ext_ref[...] = jnp.zeros(ext_ref.shape, F32)
        u_ref[...] = jnp.zeros(u_ref.shape, BF16)
        ext_next_ref[CF_HALO + TM:CF_HALO + TM + SUBLANES, :] = jnp.zeros((SUBLANES, D_MODEL), F32)

    @pl.when(first_of_batch)
    def _():
        ext_next_ref[0:CF_HALO, :] = jnp.zeros((CF_HALO, D_MODEL), F32)

    @pl.when(jnp.logical_not(first_of_batch))
    def _():
        ext_next_ref[0:CF_HALO, :] = ext_ref[TM:TM + CF_HALO, :]

    n_chunks = FF_DIM // FF_TILE
    n_pieces = TM // CF_PIECE
    conv = [None] * n_pieces
    todo = [(res, piece) for res in range(SUBLANES) for piece in range(n_pieces)]
    slots_seen = [0]

    def conv_pieces_after(result):
        for blk in range(n_pieces):
            slots_seen[0] += 1
            if slots_seen[0] % 5 == 0 or not todo:
                continue
            res, piece = todo.pop(0)
            r0 = blk * CF_PIECE
            part = _cf_conv_piece(ext_ref, dw_ref, res, piece, result[r0:r0 + 1, 0:1])
            conv[piece] = part if conv[piece] is None else conv[piece] + part

    y = _dot(u_ref[...], wo_ref[...])
    conv_pieces_after(y)

    h = (x_ref[...] * (1.0 + sc1_ref[0, 0]) + sh1_ref[0, 0]).astype(BF16)
    val = _dot(h, win_ref[:, 0:D_MODEL])
    conv_pieces_after(val)
    gate = _dot(h, win_ref[:, D_MODEL:2 * D_MODEL])
    conv_pieces_after(gate)
    ext_next_ref[CF_HALO:CF_HALO + TM, :] = val * _sigmoid(gate)

    x1 = _layer_norm(ALPHA * xres_ref[...] + (1.0 + gt1_ref[0, 0]) * y, lng_ref[0:1, :], lnb_ref[0:1, :])
    h2 = (x1 * (1.0 + sc2_ref[0, 0]) + sh2_ref[0, 0]).astype(BF16)
    chunk_cols = [slice(c * FF_TILE, (c + 1) * FF_TILE) for c in range(n_chunks)]
    d1 = _dot(h2, w1_ref[:, chunk_cols[0]])
    conv_pieces_after(d1)
    acc = None
    for c in range(n_chunks):
        d1_next = None
        if c + 1 < n_chunks:
            d1_next = _dot(h2, w1_ref[:, chunk_cols[c + 1]])
            conv_pieces_after(d1_next)
        a = jnp.maximum(d1, 0.0)
        p = _dot((a * a).astype(BF16), w2_ref[chunk_cols[c], :])
        conv_pieces_after(p)
        acc = p if acc is None else acc + p
        d1 = d1_next
    assert not todo
    o_ref[...] = _layer_norm(ALPHA * x1 + (1.0 + gt2_ref[0, 0]) * acc, lng_ref[1:2, :], lnb_ref[1:2, :])

    cu = _layer_norm(jnp.concatenate(conv, axis=0) + dwb_ref[...], cg_ref[...], cb_ref[...])
    u_next_ref[...] = _silu(cu).astype(BF16)

    u_ref[...] = u_next_ref[...]
    ext_ref[...] = ext_next_ref[...]


def _cf_layer(x2d, mods, layer, mixer, tiles_per_batch, w_in, dw_w, dw_b, cf_g, cf_b, w_out, ln_g, ln_b,
              w1, w2):
    rows = x2d.shape[0]
    n_tiles = rows // TM
    base = layer * N_MOD
    in_tile = lambda s: jnp.minimum(s, n_tiles - 1)
    mlp_tile = lambda s: jnp.maximum(s - 2, 0)

    def mod_spec(row, tile_of_step):
        return pl.BlockSpec((1, 1, 1, D_MODEL), lambda s: (row, tile_of_step(s) // tiles_per_batch, 0, 0))

    ext_rows = CF_HALO + TM + SUBLANES
    return pl.pallas_call(
        functools.partial(_cf_layer_kernel, n_tiles=n_tiles, tiles_per_batch=tiles_per_batch),
        grid=(n_tiles + 2,),
        in_specs=[
            pl.BlockSpec((TM, D_MODEL), lambda s: (in_tile(s), 0)),
            pl.BlockSpec((TM, D_MODEL), lambda s: (mlp_tile(s), 0)),
            mod_spec(base + 0, in_tile), mod_spec(base + 1, in_tile),
            mod_spec(base + 2, mlp_tile), mod_spec(base + 3, mlp_tile),
            mod_spec(base + 4, mlp_tile), mod_spec(base + 5, mlp_tile),
            _const_spec((D_MODEL, 2 * D_MODEL), mixer), _const_spec((CF_KERNEL, D_MODEL)),
            _const_spec((1, D_MODEL)), _const_spec((1, D_MODEL)), _const_spec((1, D_MODEL)),
            _const_spec((D_MODEL, D_MODEL), mixer), _const_spec((2, D_MODEL)), _const_spec((2, D_MODEL)),
            _const_spec((D_MODEL, FF_DIM), layer), _const_spec((FF_DIM, D_MODEL), layer),
        ],
        out_specs=pl.BlockSpec((TM, D_MODEL), lambda s: (mlp_tile(s), 0)),
        out_shape=jax.ShapeDtypeStruct((rows, D_MODEL), F32),
        scratch_shapes=[pltpu.VMEM((ext_rows, D_MODEL), F32),
                        pltpu.VMEM((ext_rows, D_MODEL), F32),
                        pltpu.VMEM((TM, D_MODEL), BF16),
                        pltpu.VMEM((TM, D_MODEL), BF16)],
        compiler_params=pltpu.CompilerParams(dimension_semantics=("arbitrary",),
                                             vmem_limit_bytes=VMEM_LIMIT),
        name="cf_layer",
    )(x2d, x2d, mods, mods, mods, mods, mods, mods, w_in, dw_w, dw_b, cf_g, cf_b, w_out, ln_g, ln_b, w1, w2)


def kernel(x, c, ada_w, ada_b, ln_g, ln_b, dn_w_in, dn_conv_w, dn_a_log, dn_dt_bias, dn_norm_w,
           dn_w_out, cf_w_in, cf_dw_w, cf_dw_b, cf_ln_g, cf_ln_b, cf_w_out, ff_w1, ff_w2):
    batch, seq, d = x.shape
    assert d == D_MODEL and seq % TM == 0 and TM % CHUNK == 0
    tiles_per_batch = seq // TM
    n_qkv = 3 * D_MODEL
    n_main = 4 * D_MODEL

    mods = _ada_mods(c, ada_w, ada_b)
    w1, w2 = _cast_stack(ff_w1), _cast_stack(ff_w2)
    dn_main, dn_out = _cast_stack(dn_w_in, cols=n_main), _cast_stack(dn_w_out)
    cf_in, cf_out = _cast_stack(cf_w_in), _cast_stack(cf_w_out)
    x2d = x.reshape(batch * seq, D_MODEL)
    for i in range(DEPTH):
        j = i // 2
        if i % 2 == 0:
            w_ba = jnp.pad(dn_w_in[j, :, n_main:], ((0, 0), (0, LANES - 2 * HEADS)))
            gate_pad = (HEADS, LANES - 2 * HEADS)
            alog_vec = jnp.pad(dn_a_log[j], gate_pad).reshape(1, LANES)
            dtb_vec = jnp.pad(dn_dt_bias[j], gate_pad).reshape(1, LANES)
            q, k, v, z, bg = _dn_in(x2d, mods, i, j, tiles_per_batch, dn_main, w_ba,
                                    dn_conv_w[j][:, :n_qkv], alog_vec, dtb_vec)
            u = _dn_core(q, k, v, z, bg, dn_norm_w[j].reshape(1, HEAD_DIM), tiles_per_batch)
            x2d = _post(x2d, u, mods, i, j, tiles_per_batch, dn_out, ln_g[i], ln_b[i], w1, w2)
        else:
            x2d = _cf_layer(x2d, mods, i, j, tiles_per_batch, cf_in, cf_dw_w[j],
                            cf_dw_b[j].reshape(1, D_MODEL), cf_ln_g[j].reshape(1, D_MODEL),
                            cf_ln_b[j].reshape(1, D_MODEL), cf_out, ln_g[i], ln_b[i], w1, w2)
    return x2d.reshape(batch, seq, D_MODEL)
```

```python
import functools

import jax
import jax.numpy as jnp
from jax import lax
from jax.experimental import pallas as pl
from jax.experimental.pallas import tpu as pltpu

F32 = jnp.float32
BF16 = jnp.bfloat16

D_MODEL = 1024
DEPTH = 4
N_MOD = 6
HEADS = 8
HEAD_DIM = 128
DN_CONV = 4
CHUNK = 64
BLOCK16 = 16
CF_KERNEL = 31
FF_DIM = 4 * D_MODEL
ALPHA = (2.0 * DEPTH) ** 0.25
LN_EPS = 1e-5
RMS_EPS = 1e-6
L2_EPS = 1e-6

LANES = 128
SUBLANES = 8
TM = 512
FF_TILE = 1024
CF_HALO = 32
CF_PIECE = 128
DN_HALO = 8
PASS1_CHUNKS = 4
VMEM_LIMIT = 56 * 1024 * 1024
CAST_BLOCK_ELEMS = 1024 * 1024


def _dot(a, b):
    return jnp.dot(a, b, preferred_element_type=F32)


def _dot_nt(a, b):
    return lax.dot_general(a, b, (((1,), (1,)), ((), ())), preferred_element_type=F32)


def _split3(x):
    x1 = x.astype(BF16)
    r1 = x - x1.astype(F32)
    x2 = r1.astype(BF16)
    x3 = (r1 - x2.astype(F32)).astype(BF16)
    return x1, x2, x3


def _dot_exact_lhs(a_bf16, x):
    x1, x2, x3 = _split3(x)
    return _dot(a_bf16, x3) + _dot(a_bf16, x2) + _dot(a_bf16, x1)


def _layer_norm(r, g, b):
    mu = jnp.mean(r, axis=-1, keepdims=True)
    d = r - mu
    var = jnp.mean(d * d, axis=-1, keepdims=True)
    return d * lax.rsqrt(var + LN_EPS) * g + b


def _sigmoid(x):
    return 0.5 * jnp.tanh(0.5 * x) + 0.5


def _silu(x):
    return x * _sigmoid(x)


def _softplus(x):
    return jnp.maximum(x, 0.0) + jnp.log1p(jnp.exp(-jnp.abs(x)))


def _zero_after(v):
    bits = lax.bitcast_convert_type(v, jnp.uint32)
    z = lax.shift_right_logical(lax.shift_right_logical(bits, jnp.uint32(16)), jnp.uint32(16))
    return z[0, 0].astype(jnp.int32)


def _const_spec(shape, layer=None):
    nd = len(shape)
    if layer is None:
        return pl.BlockSpec(shape, lambda *_: (0,) * nd, pipeline_mode=pl.Buffered(1))
    return pl.BlockSpec((None,) + tuple(shape), lambda *_: (layer,) + (0,) * nd,
                        pipeline_mode=pl.Buffered(1))


def _mod_spec(row, tiles_per_batch):
    return pl.BlockSpec((1, 1, 1, D_MODEL), lambda t: (row, t // tiles_per_batch, 0, 0))


def _ada_kernel(c_ref, w_ref, b_ref, o_ref):
    cond = _silu(c_ref[...])
    o_ref[0] = _dot(cond.astype(BF16), w_ref[0].astype(BF16)) + b_ref[0]


def _ada_mods(c, ada_w, ada_b):
    batch = c.shape[0]
    out = pl.pallas_call(
        _ada_kernel,
        grid=(DEPTH, N_MOD),
        in_specs=[
            pl.BlockSpec((batch, D_MODEL), lambda i, j: (0, 0)),
            pl.BlockSpec((1, D_MODEL, D_MODEL), lambda i, j: (i, 0, j)),
            pl.BlockSpec((1, 1, D_MODEL), lambda i, j: (i, 0, j)),
        ],
        out_specs=pl.BlockSpec((1, batch, D_MODEL), lambda i, j: (i * N_MOD + j, 0, 0)),
        out_shape=jax.ShapeDtypeStruct((DEPTH * N_MOD, batch, D_MODEL), F32),
        compiler_params=pltpu.CompilerParams(dimension_semantics=("arbitrary", "arbitrary")),
        name="ada_mods",
    )(c, ada_w, ada_b.reshape(DEPTH, 1, N_MOD * D_MODEL))
    return out.reshape(DEPTH * N_MOD, batch, 1, D_MODEL)


def _cast_kernel(w_ref, o_ref):
    o_ref[...] = w_ref[:, 0:o_ref.shape[-1]].astype(BF16)


def _cast_stack(w, cols=None):
    n_layers, rows, all_cols = w.shape
    cols = all_cols if cols is None else cols
    block_rows = rows
    while block_rows * all_cols > CAST_BLOCK_ELEMS:
        block_rows //= 2
    assert rows % block_rows == 0 and block_rows % SUBLANES == 0 and cols % LANES == 0
    return pl.pallas_call(
        _cast_kernel,
        grid=(n_layers, rows // block_rows),
        in_specs=[pl.BlockSpec((None, block_rows, all_cols), lambda l, i: (l, i, 0))],
        out_specs=pl.BlockSpec((None, block_rows, cols), lambda l, i: (l, i, 0)),
        out_shape=jax.ShapeDtypeStruct((n_layers, rows, cols), BF16),
        compiler_params=pltpu.CompilerParams(dimension_semantics=("arbitrary", "arbitrary")),
        name="cast_bf16",
    )(w)


def _post_kernel(x_ref, u_ref, gt1_ref, sh2_ref, sc2_ref, gt2_ref, wo_ref, lng_ref, lnb_ref,
                 w1_ref, w2_ref, o_ref):
    x = x_ref[...]
    y = _dot(u_ref[...], wo_ref[...])
    x1 = _layer_norm(ALPHA * x + (1.0 + gt1_ref[0, 0]) * y, lng_ref[0:1, :], lnb_ref[0:1, :])
    h = (x1 * (1.0 + sc2_ref[0, 0]) + sh2_ref[0, 0]).astype(BF16)
    n_chunks = FF_DIM // FF_TILE
    chunk_cols = [slice(c * FF_TILE, (c + 1) * FF_TILE) for c in range(n_chunks)]
    d1 = _dot(h, w1_ref[:, chunk_cols[0]])
    acc = None
    for c in range(n_chunks):
        d1_next = _dot(h, w1_ref[:, chunk_cols[c + 1]]) if c + 1 < n_chunks else None
        a = jnp.maximum(d1, 0.0)
        p = _dot((a * a).astype(BF16), w2_ref[chunk_cols[c], :])
        acc = p if acc is None else acc + p
        d1 = d1_next
    o_ref[...] = _layer_norm(ALPHA * x1 + (1.0 + gt2_ref[0, 0]) * acc, lng_ref[1:2, :], lnb_ref[1:2, :])


def _post(x2d, u2d, mods, layer, mixer, tiles_per_batch, w_out, ln_g, ln_b, w1, w2):
    rows = x2d.shape[0]
    row_spec = pl.BlockSpec((TM, D_MODEL), lambda t: (t, 0))
    base = layer * N_MOD
    return pl.pallas_call(
        _post_kernel,
        grid=(rows // TM,),
        in_specs=[
            row_spec, row_spec,
            _mod_spec(base + 2, tiles_per_batch), _mod_spec(base + 3, tiles_per_batch),
            _mod_spec(base + 4, tiles_per_batch), _mod_spec(base + 5, tiles_per_batch),
            _const_spec((D_MODEL, D_MODEL), mixer), _const_spec((2, D_MODEL)), _const_spec((2, D_MODEL)),
            _const_spec((D_MODEL, FF_DIM), layer), _const_spec((FF_DIM, D_MODEL), layer),
        ],
        out_specs=row_spec,
        out_shape=jax.ShapeDtypeStruct((rows, D_MODEL), F32),
        compiler_params=pltpu.CompilerParams(dimension_semantics=("arbitrary",),
                                             vmem_limit_bytes=VMEM_LIMIT),
        name="post_mlp",
    )(x2d, u2d, mods, mods, mods, mods, w_out, ln_g, ln_b, w1, w2)


def _dn_in_kernel(x_ref, sh_ref, sc_ref, w_ref, wba_ref, cw_ref, alog_ref, dtb_ref,
                  q_ref, k_ref, v_ref, z_ref, bg_ref, ext_ref, *, tiles_per_batch):
    t = pl.program_id(0)

    @pl.when(t % tiles_per_batch == 0)
    def _():
        ext_ref[0:DN_HALO, :] = jnp.zeros((DN_HALO, 3 * D_MODEL), F32)

    h = (x_ref[...] * (1.0 + sc_ref[0, 0]) + sh_ref[0, 0]).astype(BF16)

    for grp, out_ref in enumerate((q_ref, k_ref, v_ref)):
        cols = slice(grp * D_MODEL, (grp + 1) * D_MODEL)
        ext_ref[DN_HALO:DN_HALO + TM, cols] = _dot(h, w_ref[:, cols])
        ext = ext_ref[:, cols]
        acc = None
        for j in range(DN_CONV):
            off = DN_HALO - (DN_CONV - 1) + j
            res = off % SUBLANES
            shifted = ext if res == 0 else pltpu.roll(ext, DN_HALO + TM - res, axis=0)
            term = cw_ref[j:j + 1, cols] * shifted[off - res:off - res + TM, :]
            acc = term if acc is None else acc + term
        ext_ref[0:DN_HALO, cols] = ext_ref[TM:TM + DN_HALO, cols]
        y = _silu(acc)
        if out_ref is v_ref:
            out_ref[...] = y
        else:
            for hd in range(HEADS):
                lanes = slice(hd * HEAD_DIM, (hd + 1) * HEAD_DIM)
                ys = y[:, lanes]
                ss = jnp.sum(ys * ys, axis=-1, keepdims=True)
                out_ref[:, lanes] = ys * lax.rsqrt(ss + L2_EPS)

    z_ref[...] = _dot(h, w_ref[:, 3 * D_MODEL:4 * D_MODEL])

    ba = _dot(h, wba_ref[...].astype(BF16))
    beta = jax.nn.sigmoid(ba)
    g = -jnp.exp(alog_ref[...]) * _softplus(ba + dtb_ref[...])
    lane = lax.broadcasted_iota(jnp.int32, ba.shape, 1)
    bg_ref[...] = jnp.where(lane < HEADS, beta, g)


def _dn_in(x2d, mods, layer, mixer, tiles_per_batch, w_main, w_ba, conv_w, alog_vec, dtb_vec):
    rows = x2d.shape[0]
    row_spec = pl.BlockSpec((TM, D_MODEL), lambda t: (t, 0))
    base = layer * N_MOD
    out_sds = jax.ShapeDtypeStruct((rows, D_MODEL), F32)
    return pl.pallas_call(
        functools.partial(_dn_in_kernel, tiles_per_batch=tiles_per_batch),
        grid=(rows // TM,),
        in_specs=[
            row_spec, _mod_spec(base + 0, tiles_per_batch), _mod_spec(base + 1, tiles_per_batch),
            _const_spec((D_MODEL, 4 * D_MODEL), mixer), _const_spec((D_MODEL, LANES)),
            _const_spec((DN_CONV, 3 * D_MODEL)), _const_spec((1, LANES)), _const_spec((1, LANES)),
        ],
        out_specs=[row_spec, row_spec, row_spec, row_spec, pl.BlockSpec((TM, LANES), lambda t: (t, 0))],
        out_shape=[out_sds, out_sds, out_sds, out_sds, jax.ShapeDtypeStruct((rows, LANES), F32)],
        scratch_shapes=[pltpu.VMEM((DN_HALO + TM, 3 * D_MODEL), F32)],
        compiler_params=pltpu.CompilerParams(dimension_semantics=("arbitrary",),
                                             vmem_limit_bytes=VMEM_LIMIT),
        name="dn_in",
    )(x2d, mods, mods, w_main, w_ba, conv_w, alog_vec, dtb_vec)


def _inverse_minus_eye_levels(a_list, out):
    ri = lax.broadcasted_iota(jnp.int32, (CHUNK, CHUNK), 0) // BLOCK16
    ci = lax.broadcasted_iota(jnp.int32, (CHUNK, CHUNK), 1) // BLOCK16
    on_diag = ri == ci
    b16 = lambda xs: [x.astype(BF16) for x in xs]
    mm = lambda xs, ys: [_dot(x, y) for x, y in zip(xs, ys)]

    d = [jnp.where(on_diag, a, 0.0) for a in a_list]
    low = [a - x for a, x in zip(a_list, d)]
    d_16 = b16(d)
    d2 = mm(d_16, d_16)
    yield
    d2_16 = b16(d2)
    d4 = mm(d2_16, d2_16)
    dd2 = mm(d_16, d2_16)
    yield
    p = [x2 - x - y for x, x2, y in zip(d, d2, dd2)]
    d4_16 = b16(d4)
    d8 = mm(d4_16, d4_16)
    pd4 = mm(b16(p), d4_16)
    yield
    p = [x + y + z for x, y, z in zip(p, d4, pd4)]
    pd8 = mm(b16(p), b16(d8))
    yield
    p = [x + y + z for x, y, z in zip(p, d8, pd8)]
    p_16 = b16(p)
    pl_ = mm(p_16, b16(low))
    yield
    m = [x + y for x, y in zip(low, pl_)]
    m_16 = b16(m)
    m2 = mm(m_16, m_16)
    yield
    mm2 = mm(m_16, b16(m2))
    yield
    q = [x2 - x - y for x, x2, y in zip(m, m2, mm2)]
    qp = mm(b16(q), p_16)
    yield
    out.extend(x + y + z for x, y, z in zip(q, p, qp))


def _dn_core_kernel(q_ref, k_ref, v_ref, z_ref, bg_ref, nw_ref, o_ref,
                    state_ref, gam_ref, egam_ref, edec_ref, egl_ref,
                    u_ref, w_ref, qd_ref, kdt_ref, a_ref, *, tiles_per_batch):
    s = pl.program_id(0)
    cur = s % 2
    prev = 1 - cur

    @pl.when(s == 0)
    def _():
        for ref in (egl_ref, u_ref, w_ref, qd_ref, kdt_ref, a_ref):
            ref[1] = jnp.zeros(ref.shape[1:], ref.dtype)

    @pl.when(jnp.maximum(s - 1, 0) % tiles_per_batch == 0)
    def _():
        state_ref[...] = jnp.zeros(state_ref.shape, F32)

    r_idx = lax.broadcasted_iota(jnp.int32, (CHUNK, CHUNK), 0)
    c_idx = lax.broadcasted_iota(jnp.int32, (CHUNK, CHUNK), 1)
    causal = c_idx <= r_idx
    strict = c_idx < r_idx

    cum_mat = jnp.where(causal, 1.0, 0.0).astype(BF16)
    for c in range(TM // CHUNK):
        rows = slice(c * CHUNK, (c + 1) * CHUNK)
        gam = _dot_exact_lhs(cum_mat, bg_ref[rows, :])
        gtot = jnp.broadcast_to(gam[CHUNK - 1:CHUNK, :], gam.shape)
        gam_ref[rows, :] = gam
        egam_ref[rows, :] = jnp.exp(gam)
        edec_ref[rows, :] = jnp.exp(gtot - gam)
        egl_ref[cur, rows, :] = jnp.exp(gtot)
    scale = HEAD_DIM ** -0.5
    wide = (CHUNK, HEAD_DIM)
    nw = nw_ref[...]

    def independent_levels(grp):
        probs = []
        for ci_ in range(PASS1_CHUNKS):
            cidx = grp * PASS1_CHUNKS + ci_
            rows = pl.ds(pl.multiple_of(cidx * CHUNK, CHUNK), CHUNK)
            gam_c = gam_ref[rows, :]
            gam_t = gam_c.T
            bg_c = bg_ref[rows, :]
            egam_c = egam_ref[rows, :]
            edec_c = edec_ref[rows, :]
            for hd in range(HEADS):
                lanes = slice(hd * HEAD_DIM, (hd + 1) * HEAD_DIM)
                gl = HEADS + hd
                beta = jnp.broadcast_to(bg_c[:, hd:hd + 1], wide)
                eg = jnp.broadcast_to(egam_c[:, gl:gl + 1], wide)
                ed = jnp.broadcast_to(edec_c[:, gl:gl + 1], wide)
                gcol = jnp.broadcast_to(gam_c[:, gl:gl + 1], wide)[:, :CHUNK]
                decay = jnp.exp(jnp.where(causal, gcol - gam_t[gl:gl + 1, :], -jnp.inf))
                probs.append(dict(cidx=cidx, rows=rows, hd=hd, lanes=lanes, beta=beta, eg=eg, ed=ed,
                                  decay=decay, q=q_ref[rows, lanes], k=k_ref[rows, lanes],
                                  v=v_ref[rows, lanes]))
        for p in probs:
            p["k16"] = p["k"].astype(BF16)
            p["kq16"] = jnp.concatenate([p["k16"], (p["q"] * scale).astype(BF16)], axis=0)
        kkqk = [_dot_nt(p["kq16"], p["k16"]) for p in probs]
        yield
        a_list = []
        for p, kk in zip(probs, kkqk):
            a_list.append(jnp.where(strict, kk[:CHUNK] * p["beta"][:, :CHUNK] * p["decay"], 0.0))
            a_ref[cur, p["hd"], p["rows"], :] = (kk[CHUNK:] * p["decay"]).astype(BF16)
        tinv = []
        yield from _inverse_minus_eye_levels(a_list, tinv)
        for p in probs:
            p["vb"] = p["v"] * p["beta"]
            p["kbg"] = p["k"] * p["beta"] * p["eg"]
        uw = [_dot(t.astype(BF16), jnp.concatenate([p["vb"], p["kbg"]], axis=1).astype(BF16))
              for p, t in zip(probs, tinv)]
        yield
        for p, r in zip(probs, uw):
            rows, lanes = p["rows"], p["lanes"]
            u_ref[cur, rows, lanes] = p["vb"] + r[:, :HEAD_DIM]
            w_ref[cur, rows, lanes] = (p["kbg"] + r[:, HEAD_DIM:]).astype(BF16)
            qd_ref[cur, rows, lanes] = (p["q"] * (p["eg"] * scale)).astype(BF16)
            kdt_ref[cur, p["hd"], p["cidx"]] = (p["k"] * p["ed"]).T.astype(BF16)

    def recurrent_levels(grp):
        heads = range(HEADS)
        lanes = [slice(hd * HEAD_DIM, (hd + 1) * HEAD_DIM) for hd in heads]
        for ci_ in range(PASS1_CHUNKS):
            c = grp * PASS1_CHUNKS + ci_
            rows = pl.ds(pl.multiple_of(c * CHUNK, CHUNK), CHUNK)
            egl_c = egl_ref[prev, rows, :]
            state = [state_ref[hd] for hd in heads]
            wq = [jnp.concatenate([w_ref[prev, rows, lanes[hd]], qd_ref[prev, rows, lanes[hd]]], axis=0)
                  for hd in heads]
            r = [_dot(wq[hd], state[hd].astype(BF16)) for hd in heads]
            yield
            vn16 = [(u_ref[prev, rows, lanes[hd]] - r[hd][:CHUNK]).astype(BF16) for hd in heads]
            av = [_dot(a_ref[prev, hd, rows, :], vn16[hd]) for hd in heads]
            kv = [_dot(kdt_ref[prev, hd, c], vn16[hd]) for hd in heads]
            yield
            for hd in heads:
                state_ref[hd] = state[hd] * egl_c[0:1, HEADS + hd:HEADS + hd + 1] + kv[hd]
                o = r[hd][CHUNK:] + av[hd]
                on = o * lax.rsqrt(jnp.mean(o * o, axis=-1, keepdims=True) + RMS_EPS) * nw
                o_ref[rows, lanes[hd]] = (on * _silu(z_ref[rows, lanes[hd]])).astype(BF16)

    def group_body(grp, carry):
        pending = [recurrent_levels(grp), independent_levels(grp)]
        while pending:
            for gen in list(pending):
                try:
                    next(gen)
                except StopIteration:
                    pending.remove(gen)
        return carry

    lax.fori_loop(0, TM // CHUNK // PASS1_CHUNKS, group_body, 0)


def _dn_core(q, k, v, z, bg, norm_w, tiles_per_batch):
    rows = q.shape[0]
    n_tiles = rows // TM
    in_tile = lambda s: jnp.minimum(s, n_tiles - 1)
    out_tile = lambda s: jnp.maximum(s - 1, 0)
    in_rows = pl.BlockSpec((TM, D_MODEL), lambda s: (in_tile(s), 0))
    out_rows = pl.BlockSpec((TM, D_MODEL), lambda s: (out_tile(s), 0))
    small = pltpu.VMEM((TM, LANES), F32)
    return pl.pallas_call(
        functools.partial(_dn_core_kernel, tiles_per_batch=tiles_per_batch),
        grid=(n_tiles + 1,),
        in_specs=[in_rows, in_rows, in_rows, out_rows,
                  pl.BlockSpec((TM, LANES), lambda s: (in_tile(s), 0)),
                  pl.BlockSpec((1, HEAD_DIM), lambda s: (0, 0))],
        out_specs=out_rows,
        out_shape=jax.ShapeDtypeStruct((rows, D_MODEL), BF16),
        scratch_shapes=[
            pltpu.VMEM((HEADS, HEAD_DIM, HEAD_DIM), F32),
            small, small, small,
            pltpu.VMEM((2, TM, LANES), F32),
            pltpu.VMEM((2, TM, D_MODEL), F32),
            pltpu.VMEM((2, TM, D_MODEL), BF16),
            pltpu.VMEM((2, TM, D_MODEL), BF16),
            pltpu.VMEM((2, HEADS, TM // CHUNK, HEAD_DIM, CHUNK), BF16),
            pltpu.VMEM((2, HEADS, TM, CHUNK), BF16),
        ],
        compiler_params=pltpu.CompilerParams(dimension_semantics=("arbitrary",),
                                             vmem_limit_bytes=VMEM_LIMIT),
        name="dn_core",
    )(q, k, v, z, bg, norm_w)


def _cf_conv_piece(ext_ref, dw_ref, res, piece, not_before):
    n = CF_PIECE + CF_HALO + SUBLANES
    start = pl.multiple_of(piece * CF_PIECE + _zero_after(not_before), SUBLANES)
    window = ext_ref[pl.ds(start, n), :]
    shifted = window if res == 0 else pltpu.roll(window, n - res, axis=0)
    acc = None
    for j in range(CF_KERNEL):
        off = CF_HALO - (CF_KERNEL - 1) + j
        if off % SUBLANES != res:
            continue
        term = dw_ref[j:j + 1, :] * shifted[off - res:off - res + CF_PIECE, :]
        acc = term if acc is None else acc + term
    return acc


def _cf_layer_kernel(x_ref, xres_ref, sh1_ref, sc1_ref, gt1_ref, sh2_ref, sc2_ref, gt2_ref,
                     win_ref, dw_ref, dwb_ref, cg_ref, cb_ref, wo_ref, lng_ref, lnb_ref, w1_ref, w2_ref,
                     o_ref, ext_ref, ext_next_ref, u_ref, u_next_ref, *, n_tiles, tiles_per_batch):
    s = pl.program_id(0)
    t_in = jnp.minimum(s, n_tiles - 1)
    first_of_batch = t_in % tiles_per_batch == 0

    @pl.when(s == 0)
    def _():
        ext_ref[...] = jnp.zeros(ext_ref.shape, F32)
        u_ref[...] = jnp.zeros(u_ref.shape, BF16)
        ext_next_ref[CF_HALO + TM:CF_HALO + TM + SUBLANES, :] = jnp.zeros((SUBLANES, D_MODEL), F32)

    @pl.when(first_of_batch)
    def _():
        ext_next_ref[0:CF_HALO, :] = jnp.zeros((CF_HALO, D_MODEL), F32)

    @pl.when(jnp.logical_not(first_of_batch))
    def _():
        ext_next_ref[0:CF_HALO, :] = ext_ref[TM:TM + CF_HALO, :]

    n_chunks = FF_DIM // FF_TILE
    n_pieces = TM // CF_PIECE
    conv = [None] * n_pieces
    todo = [(res, piece) for res in range(SUBLANES) for piece in range(n_pieces)]
    slots_seen = [0]

    def conv_pieces_after(result):
        for blk in range(n_pieces):
            slots_seen[0] += 1
            if slots_seen[0] % 5 == 0 or not todo:
                continue
            res, piece = todo.pop(0)
            r0 = blk * CF_PIECE
            part = _cf_conv_piece(ext_ref, dw_ref, res, piece, result[r0:r0 + 1, 0:1])
            conv[piece] = part if conv[piece] is None else conv[piece] + part

    y = _dot(u_ref[...], wo_ref[...])
    conv_pieces_after(y)

    h = (x_ref[...] * (1.0 + sc1_ref[0, 0]) + sh1_ref[0, 0]).astype(BF16)
    val = _dot(h, win_ref[:, 0:D_MODEL])
    conv_pieces_after(val)
    gate = _dot(h, win_ref[:, D_MODEL:2 * D_MODEL])
    conv_pieces_after(gate)
    ext_next_ref[CF_HALO:CF_HALO + TM, :] = val * _sigmoid(gate)

    x1 = _layer_norm(ALPHA * xres_ref[...] + (1.0 + gt1_ref[0, 0]) * y, lng_ref[0:1, :], lnb_ref[0:1, :])
    h2 = (x1 * (1.0 + sc2_ref[0, 0]) + sh2_ref[0, 0]).astype(BF16)
    chunk_cols = [slice(c * FF_TILE, (c + 1) * FF_TILE) for c in range(n_chunks)]
    d1 = _dot(h2, w1_ref[:, chunk_cols[0]])
    conv_pieces_after(d1)
    acc = None
    for c in range(n_chunks):
        d1_next = None
        if c + 1 < n_chunks:
            d1_next = _dot(h2, w1_ref[:, chunk_cols[c + 1]])
            conv_pieces_after(d1_next)
        a = jnp.maximum(d1, 0.0)
        p = _dot((a * a).astype(BF16), w2_ref[chunk_cols[c], :])
        conv_pieces_after(p)
        acc = p if acc is None else acc + p
        d1 = d1_next
    assert not todo
    o_ref[...] = _layer_norm(ALPHA * x1 + (1.0 + gt2_ref[0, 0]) * acc, lng_ref[1:2, :], lnb_ref[1:2, :])

    cu = _layer_norm(jnp.concatenate(conv, axis=0) + dwb_ref[...], cg_ref[...], cb_ref[...])
    u_next_ref[...] = _silu(cu).astype(BF16)

    u_ref[...] = u_next_ref[...]
    ext_ref[...] = ext_next_ref[...]


def _cf_layer(x2d, mods, layer, mixer, tiles_per_batch, w_in, dw_w, dw_b, cf_g, cf_b, w_out, ln_g, ln_b,
              w1, w2):
    rows = x2d.shape[0]
    n_tiles = rows // TM
    base = layer * N_MOD
    in_tile = lambda s: jnp.minimum(s, n_tiles - 1)
    mlp_tile = lambda s: jnp.maximum(s - 2, 0)

    def mod_spec(row, tile_of_step):
        return pl.BlockSpec((1, 1, 1, D_MODEL), lambda s: (row, tile_of_step(s) // tiles_per_batch, 0, 0))

    ext_rows = CF_HALO + TM + SUBLANES
    return pl.pallas_call(
        functools.partial(_cf_layer_kernel, n_tiles=n_tiles, tiles_per_batch=tiles_per_batch),
        grid=(n_tiles + 2,),
        in_specs=[
            pl.BlockSpec((TM, D_MODEL), lambda s: (in_tile(s), 0)),
            pl.BlockSpec((TM, D_MODEL), lambda s: (mlp_tile(s), 0)),
            mod_spec(base + 0, in_tile), mod_spec(base + 1, in_tile),
            mod_spec(base + 2, mlp_tile), mod_spec(base + 3, mlp_tile),
            mod_spec(base + 4, mlp_tile), mod_spec(base + 5, mlp_tile),
            _const_spec((D_MODEL, 2 * D_MODEL), mixer), _const_spec((CF_KERNEL, D_MODEL)),
            _const_spec((1, D_MODEL)), _const_spec((1, D_MODEL)), _const_spec((1, D_MODEL)),
            _const_spec((D_MODEL, D_MODEL), mixer), _const_spec((2, D_MODEL)), _const_spec((2, D_MODEL)),
            _const_spec((D_MODEL, FF_DIM), layer), _const_spec((FF_DIM, D_MODEL), layer),
        ],
        out_specs=pl.BlockSpec((TM, D_MODEL), lambda s: (mlp_tile(s), 0)),
        out_shape=jax.ShapeDtypeStruct((rows, D_MODEL), F32),
        scratch_shapes=[pltpu.VMEM((ext_rows, D_MODEL), F32),
                        pltpu.VMEM((ext_rows, D_MODEL), F32),
                        pltpu.VMEM((TM, D_MODEL), BF16),
                        pltpu.VMEM((TM, D_MODEL), BF16)],
        compiler_params=pltpu.CompilerParams(dimension_semantics=("arbitrary",),
                                             vmem_limit_bytes=VMEM_LIMIT),
        name="cf_layer",
    )(x2d, x2d, mods, mods, mods, mods, mods, mods, w_in, dw_w, dw_b, cf_g, cf_b, w_out, ln_g, ln_b, w1, w2)


def kernel(x, c, ada_w, ada_b, ln_g, ln_b, dn_w_in, dn_conv_w, dn_a_log, dn_dt_bias, dn_norm_w,
           dn_w_out, cf_w_in, cf_dw_w, cf_dw_b, cf_ln_g, cf_ln_b, cf_w_out, ff_w1, ff_w2):
    batch, seq, d = x.shape
    assert d == D_MODEL and seq % TM == 0 and TM % CHUNK == 0
    tiles_per_batch = seq // TM
    n_qkv = 3 * D_MODEL
    n_main = 4 * D_MODEL

    mods = _ada_mods(c, ada_w, ada_b)
    w1, w2 = _cast_stack(ff_w1), _cast_stack(ff_w2)
    dn_main, dn_out = _cast_stack(dn_w_in, cols=n_main), _cast_stack(dn_w_out)
    cf_in, cf_out = _cast_stack(cf_w_in), _cast_stack(cf_w_out)
    x2d = x.reshape(batch * seq, D_MODEL)
    for i in range(DEPTH):
        j = i // 2
        if i % 2 == 0:
            w_ba = jnp.pad(dn_w_in[j, :, n_main:], ((0, 0), (0, LANES - 2 * HEADS)))
            gate_pad = (HEADS, LANES - 2 * HEADS)
            alog_vec = jnp.pad(dn_a_log[j], gate_pad).reshape(1, LANES)
            dtb_vec = jnp.pad(dn_dt_bias[j], gate_pad).reshape(1, LANES)
            q, k, v, z, bg = _dn_in(x2d, mods, i, j, tiles_per_batch, dn_main, w_ba,
                                    dn_conv_w[j][:, :n_qkv], alog_vec, dtb_vec)
            u = _dn_core(q, k, v, z, bg, dn_norm_w[j].reshape(1, HEAD_DIM), tiles_per_batch)
            x2d = _post(x2d, u, mods, i, j, tiles_per_batch, dn_out, ln_g[i], ln_b[i], w1, w2)
        else:
            x2d = _cf_layer(x2d, mods, i, j, tiles_per_batch, cf_in, cf_dw_w[j],
                            cf_dw_b[j].reshape(1, D_MODEL), cf_ln_g[j].reshape(1, D_MODEL),
                            cf_ln_b[j].reshape(1, D_MODEL), cf_out, ln_g[i], ln_b[i], w1, w2)
    return x2d.reshape(batch, seq, D_MODEL)
```
